```python
import functools
import jax, jax.numpy as jnp
from jax import lax
import numpy as np

D_MODEL = 1024
BATCH = 8
SEQ = 2048
DEPTH = 4
DEC_BATCH = 32
DEC_SEQ = 8
PAST_LEN = 8192
PAGE_SIZE = 128

HEAD_DIM = 64
MIX_WIDTH = D_MODEL
RWKV_WIDTH = MIX_WIDTH // 2
FOX_WIDTH = MIX_WIDTH - RWKV_WIDTH
RWKV_HEADS = RWKV_WIDTH // HEAD_DIM
FOX_HEADS = FOX_WIDTH // HEAD_DIM
DECAY_RANK = max(32, int(round(1.8 * D_MODEL ** 0.5 / 32)) * 32)
ICLR_RANK = max(32, int(round(1.8 * D_MODEL ** 0.5 / 32)) * 32)
GATE_RANK = max(32, int(round(0.6 * D_MODEL ** 0.8 / 32)) * 32)
RWKV_PROJ = 3 * RWKV_WIDTH + DECAY_RANK + ICLR_RANK + GATE_RANK
FOX_PROJ = 3 * FOX_WIDTH + FOX_HEADS
PROJ_WIDTH = RWKV_PROJ + FOX_PROJ
RWKV_SPLITS = (RWKV_WIDTH, 2 * RWKV_WIDTH, 3 * RWKV_WIDTH,
               3 * RWKV_WIDTH + DECAY_RANK, 3 * RWKV_WIDTH + DECAY_RANK + ICLR_RANK)
FOX_SPLITS = (FOX_WIDTH, 2 * FOX_WIDTH, 3 * FOX_WIDTH)
D_FF = ((8 * D_MODEL // 3 + 127) // 128) * 128
PLE_DIM = 256
QUERY_BLOCK = 128
RMS_EPS = 1e-6
GN_EPS = 64e-5
FORGET_BIAS = 3.0
NEG_INF = -1e30

kernel_name = "hybrid_rwkv7_fox_macaron_ple_step"


def rms_norm(x, g):
    xf = x.astype(jnp.float32)
    y = xf * lax.rsqrt(jnp.mean(xf * xf, axis=-1, keepdims=True) + RMS_EPS)
    return (y * g.astype(jnp.float32)).astype(x.dtype)


def swiglu(x, w_gate, w_up, w_down):
    return (jax.nn.silu(x @ w_gate) * (x @ w_up)) @ w_down


def token_shift(h, prev):
    return jnp.concatenate([prev[:, None, :], h[:, :-1, :]], axis=1)


def rwkv7_recurrence(r, w, k, v, a, b, s0):
    def step(s, inp):
        r_t, w_t, k_t, v_t, a_t, b_t = inp
        sa = jnp.einsum('bhvk,bhk->bhv', s, a_t)
        s = s * w_t[:, :, None, :] + sa[..., None] * b_t[:, :, None, :] + v_t[..., None] * k_t[:, :, None, :]
        y_t = jnp.einsum('bhvk,bhk->bhv', s, r_t)
        return s, y_t
    xs = tuple(jnp.moveaxis(t.astype(jnp.float32), 1, 0) for t in (r, w, k, v, a, b))
    s_fin, ys = lax.scan(step, s0.astype(jnp.float32), xs)
    return jnp.moveaxis(ys, 0, 1).astype(r.dtype), s_fin.astype(s0.dtype)


def rwkv7_group(h, prev, s0, w):
    B, T, _ = h.shape
    f32 = jnp.float32
    hs = h + w["rwkv_mu"] * (token_shift(h, prev) - h)
    r, k, v, d_decay, d_iclr, d_gate = jnp.split(hs, list(RWKV_SPLITS), axis=-1)
    w_log = -jax.nn.softplus(-(w["rwkv_w0"] + jnp.tanh(d_decay) @ w["rwkv_w_decay"]).astype(f32)) - 0.5
    decay = jnp.exp(-jnp.exp(w_log))
    a = jax.nn.sigmoid(w["rwkv_a0"] + d_iclr @ w["rwkv_w_iclr"])
    g = jax.nn.sigmoid(d_gate) @ w["rwkv_w_gate"]
    split_heads = lambda t: t.reshape(B, T, RWKV_HEADS, HEAD_DIM)
    kk = split_heads(k * w["rwkv_k_k"]).astype(f32)
    kk = kk / jnp.maximum(jnp.sqrt(jnp.sum(kk * kk, axis=-1, keepdims=True)), 1e-12)
    k = k * (1.0 + (a - 1.0) * w["rwkv_k_a"])
    r_h, k_h, v_h, a_h = split_heads(r), split_heads(k), split_heads(v), split_heads(a)
    y, s_new = rwkv7_recurrence(r_h, split_heads(decay), k_h, v_h, -kk, kk * a_h, s0)
    yf = y.astype(f32)
    mean = jnp.mean(yf, axis=-1, keepdims=True)
    var = jnp.mean(jnp.square(yf - mean), axis=-1, keepdims=True)
    yn = ((yf - mean) * lax.rsqrt(var + GN_EPS)).reshape(B, T, RWKV_WIDTH) * w["rwkv_lnx_g"] + w["rwkv_lnx_b"]
    bonus = jnp.sum(r_h * k_h * w["rwkv_r_k"], axis=-1, keepdims=True) * v_h
    out = (yn.astype(h.dtype) + bonus.reshape(B, T, RWKV_WIDTH)) * g
    return out, s_new, h[:, -1, :]


def fox_prompt_attention(q, k, v, logf):
    B, S, H, Dh = q.shape
    nb = S // QUERY_BLOCK
    c = jnp.cumsum(logf, axis=1)
    c_keys = jnp.swapaxes(c, 1, 2)
    qb = q.reshape(B, nb, QUERY_BLOCK, H, Dh).transpose(1, 0, 2, 3, 4)
    cb = c.reshape(B, nb, QUERY_BLOCK, H).transpose(1, 0, 2, 3)
    pos_k = jnp.arange(S)
    scale = HEAD_DIM ** -0.5

    def block(args):
        i, q_i, c_i = args
        s = jnp.einsum('bqhd,bkhd->bhqk', q_i, k).astype(jnp.float32) * scale
        s = s + jnp.swapaxes(c_i, 1, 2)[..., None] - c_keys[:, :, None, :]
        pos_q = i * QUERY_BLOCK + jnp.arange(QUERY_BLOCK)
        s = jnp.where(pos_k[None, :] <= pos_q[:, None], s, NEG_INF)
        p = jax.nn.softmax(s, axis=-1)
        return jnp.einsum('bhqk,bkhd->bqhd', p.astype(v.dtype), v)

    out = lax.map(block, (jnp.arange(nb), qb, cb))
    return out.transpose(1, 0, 2, 3, 4).reshape(B, S, H, Dh)


def fox_cached_attention(q, k, v, logf, k_past, v_past, logf_past):
    P = k_past.shape[1]
    T = q.shape[1]
    k_all = jnp.concatenate([k_past, k], axis=1)
    v_all = jnp.concatenate([v_past, v], axis=1)
    c = jnp.cumsum(jnp.concatenate([logf_past.astype(jnp.float32), logf], axis=1), axis=1)
    c_keys = jnp.swapaxes(c, 1, 2)
    c_q = c_keys[:, :, P:]
    s = jnp.einsum('bqhd,bkhd->bhqk', q, k_all).astype(jnp.float32) * (HEAD_DIM ** -0.5)
    s = s + c_q[..., None] - c_keys[:, :, None, :]
    mask = jnp.arange(P + T)[None, :] <= (P + jnp.arange(T))[:, None]
    s = jnp.where(mask, s, NEG_INF)
    p = jax.nn.softmax(s, axis=-1)
    return jnp.einsum('bhqk,bkhd->bqhd', p.astype(v_all.dtype), v_all)


def decoder_layer(x, p, shift_prev, wkv0, attend, w):
    B, T, _ = x.shape
    x = x + 0.5 * swiglu(rms_norm(x, w["ffn1_norm"]), w["ffn1_w_gate"], w["ffn1_w_up"], w["ffn1_w_down"])
    proj = rms_norm(x, w["mix_norm"]) @ w["w_in"]
    h_rwkv = proj[..., :RWKV_PROJ]
    h_fox = proj[..., RWKV_PROJ:]
    y_rwkv, wkv_new, shift_new = rwkv7_group(h_rwkv, shift_prev, wkv0, w)
    q, k, v, f_logit = jnp.split(h_fox, list(FOX_SPLITS), axis=-1)
    q = q.reshape(B, T, FOX_HEADS, HEAD_DIM)
    k = k.reshape(B, T, FOX_HEADS, HEAD_DIM)
    v = v.reshape(B, T, FOX_HEADS, HEAD_DIM)
    logf = jax.nn.log_sigmoid(f_logit.astype(jnp.float32) + w["fox_b_f"].astype(jnp.float32))
    o = attend(q, k, v, logf)
    y_fox = rms_norm(o, w["fox_out_norm"]).reshape(B, T, FOX_WIDTH)
    x = x + jnp.concatenate([y_rwkv, y_fox], axis=-1) @ w["w_out"]
    x = x + 0.5 * swiglu(rms_norm(x, w["ffn2_norm"]), w["ffn2_w_gate"], w["ffn2_w_up"], w["ffn2_w_down"])
    gate = jax.nn.sigmoid(rms_norm(x, w["ple_norm"]) @ w["ple_w_gate"])
    x = x + gate * (p @ w["ple_w_up"])
    return x, (k, v, logf.astype(k.dtype), wkv_new, shift_new)


def setup_inputs(seed: int = 0) -> dict:
    key = jax.random.key(seed)
    keys = iter(jax.random.split(key, 48))
    f32 = jnp.float32
    n_pages = PAST_LEN // PAGE_SIZE
    n_pool = (DEC_BATCH * n_pages * 5) // 4

    def normal(shape, scale=1.0):
        return scale * jax.random.normal(next(keys), shape, f32)

    def gain(shape):
        return 1.0 + 0.1 * normal(shape)

    def mat(shape):
        return normal(shape, shape[-2] ** -0.5)

    inputs = {}
    inputs["x_prompt"] = normal((BATCH, SEQ, D_MODEL))
    inputs["x_sample"] = normal((DEC_BATCH, DEC_SEQ, D_MODEL))
    inputs["cache_k"] = normal((DEPTH, n_pool, PAGE_SIZE, FOX_HEADS, HEAD_DIM))
    inputs["cache_v"] = normal((DEPTH, n_pool, PAGE_SIZE, FOX_HEADS, HEAD_DIM))
    inputs["cache_logf"] = jax.nn.log_sigmoid(FORGET_BIAS + normal((DEPTH, n_pool, PAGE_SIZE, FOX_HEADS)))
    inputs["state_wkv"] = normal((DEPTH, DEC_BATCH, RWKV_HEADS, HEAD_DIM, HEAD_DIM), 0.5)
    inputs["state_shift"] = normal((DEPTH, DEC_BATCH, RWKV_PROJ))
    inputs["page_table"] = jax.random.permutation(next(keys), n_pool)[: DEC_BATCH * n_pages].reshape(
        DEC_BATCH, n_pages).astype(jnp.int32)
    inputs["p_prompt"] = normal((DEPTH, BATCH, SEQ, PLE_DIM))
    inputs["p_sample"] = normal((DEPTH, DEC_BATCH, DEC_SEQ, PLE_DIM))
    inputs["ffn1_norm"] = gain((DEPTH, D_MODEL))
    inputs["ffn1_w_gate"] = mat((DEPTH, D_MODEL, D_FF))
    inputs["ffn1_w_up"] = mat((DEPTH, D_MODEL, D_FF))
    inputs["ffn1_w_down"] = mat((DEPTH, D_FF, D_MODEL))
    inputs["mix_norm"] = gain((DEPTH, D_MODEL))
    inputs["w_in"] = mat((DEPTH, D_MODEL, PROJ_WIDTH))
    inputs["rwkv_mu"] = jax.random.uniform(next(keys), (DEPTH, RWKV_PROJ), f32)
    inputs["rwkv_w0"] = -1.0 + normal((DEPTH, RWKV_WIDTH), 0.5)
    inputs["rwkv_w_decay"] = normal((DEPTH, DECAY_RANK, RWKV_WIDTH), 0.1 * DECAY_RANK ** -0.5)
    inputs["rwkv_a0"] = normal((DEPTH, RWKV_WIDTH), 0.1)
    inputs["rwkv_w_iclr"] = normal((DEPTH, ICLR_RANK, RWKV_WIDTH), 0.1 * ICLR_RANK ** -0.5)
    inputs["rwkv_w_gate"] = mat((DEPTH, GATE_RANK, RWKV_WIDTH))
    inputs["rwkv_k_k"] = 0.85 + normal((DEPTH, RWKV_WIDTH), 0.1)
    inputs["rwkv_k_a"] = gain((DEPTH, RWKV_WIDTH))
    inputs["rwkv_r_k"] = normal((DEPTH, RWKV_HEADS, HEAD_DIM), 0.1)
    inputs["rwkv_lnx_g"] = gain((DEPTH, RWKV_WIDTH))
    inputs["rwkv_lnx_b"] = normal((DEPTH, RWKV_WIDTH), 0.01)
    inputs["fox_b_f"] = FORGET_BIAS + normal((DEPTH, FOX_HEADS), 0.5)
    inputs["fox_out_norm"] = gain((DEPTH, FOX_HEADS, HEAD_DIM))
    inputs["w_out"] = mat((DEPTH, MIX_WIDTH, D_MODEL))
    inputs["ffn2_norm"] = gain((DEPTH, D_MODEL))
    inputs["ffn2_w_gate"] = mat((DEPTH, D_MODEL, D_FF))
    inputs["ffn2_w_up"] = mat((DEPTH, D_MODEL, D_FF))
    inputs["ffn2_w_down"] = mat((DEPTH, D_FF, D_MODEL))
    inputs["ple_norm"] = gain((DEPTH, D_MODEL))
    inputs["ple_w_gate"] = mat((DEPTH, D_MODEL, D_MODEL))
    inputs["ple_w_up"] = mat((DEPTH, PLE_DIM, D_MODEL))
    inputs["final_norm"] = gain((D_MODEL,))
    return inputs


def reference(x_prompt, x_sample, cache_k, cache_v, cache_logf, state_wkv, state_shift, page_table,
              p_prompt, p_sample, ffn1_norm, ffn1_w_gate, ffn1_w_up, ffn1_w_down, mix_norm, w_in,
              rwkv_mu, rwkv_w0, rwkv_w_decay, rwkv_a0, rwkv_w_iclr, rwkv_w_gate, rwkv_k_k, rwkv_k_a,
              rwkv_r_k, rwkv_lnx_g, rwkv_lnx_b, fox_b_f, fox_out_norm, w_out, ffn2_norm, ffn2_w_gate,
              ffn2_w_up, ffn2_w_down, ple_norm, ple_w_gate, ple_w_up, final_norm):
    stacked = dict(ffn1_norm=ffn1_norm, ffn1_w_gate=ffn1_w_gate, ffn1_w_up=ffn1_w_up, ffn1_w_down=ffn1_w_down,
                   mix_norm=mix_norm, w_in=w_in, rwkv_mu=rwkv_mu, rwkv_w0=rwkv_w0, rwkv_w_decay=rwkv_w_decay,
                   rwkv_a0=rwkv_a0, rwkv_w_iclr=rwkv_w_iclr, rwkv_w_gate=rwkv_w_gate, rwkv_k_k=rwkv_k_k,
                   rwkv_k_a=rwkv_k_a, rwkv_r_k=rwkv_r_k, rwkv_lnx_g=rwkv_lnx_g, rwkv_lnx_b=rwkv_lnx_b,
                   fox_b_f=fox_b_f, fox_out_norm=fox_out_norm, w_out=w_out, ffn2_norm=ffn2_norm,
                   ffn2_w_gate=ffn2_w_gate, ffn2_w_up=ffn2_w_up, ffn2_w_down=ffn2_w_down,
                   ple_norm=ple_norm, ple_w_gate=ple_w_gate, ple_w_up=ple_w_up)
    b_p = x_prompt.shape[0]
    b_s = x_sample.shape[0]
    past_len = page_table.shape[1] * cache_k.shape[2]
    shift0 = jnp.zeros((b_p, RWKV_PROJ), x_prompt.dtype)
    wkv0 = jnp.zeros((b_p, RWKV_HEADS, HEAD_DIM, HEAD_DIM), x_prompt.dtype)

    yp, ys = x_prompt, x_sample
    st_p, st_s = [], []
    for l in range(DEPTH):
        w = {name: arr[l] for name, arr in stacked.items()}
        yp, new_p = decoder_layer(yp, p_prompt[l], shift0, wkv0, fox_prompt_attention, w)
        k_past = cache_k[l, page_table].reshape(b_s, past_len, FOX_HEADS, HEAD_DIM)
        v_past = cache_v[l, page_table].reshape(b_s, past_len, FOX_HEADS, HEAD_DIM)
        lf_past = cache_logf[l, page_table].reshape(b_s, past_len, FOX_HEADS)
        attend_s = functools.partial(fox_cached_attention, k_past=k_past, v_past=v_past, logf_past=lf_past)
        ys, new_s = decoder_layer(ys, p_sample[l], state_shift[l], state_wkv[l], attend_s, w)
        st_p.append(new_p)
        st_s.append(new_s)

    yp = rms_norm(yp, final_norm)
    ys = rms_norm(ys, final_norm)
    k_p = jnp.stack([s[0] for s in st_p])
    v_p = jnp.stack([s[1] for s in st_p])
    lf_p = jnp.stack([s[2] for s in st_p])
    wkv_p = jnp.stack([s[3] for s in st_p])
    sh_p = jnp.stack([s[4] for s in st_p])
    k_s = jnp.stack([s[0] for s in st_s])
    v_s = jnp.stack([s[1] for s in st_s])
    lf_s = jnp.stack([s[2] for s in st_s])
    wkv_s = jnp.stack([s[3] for s in st_s])
    sh_s = jnp.stack([s[4] for s in st_s])
    return (yp, ys, k_p, v_p, lf_p, wkv_p, sh_p, k_s, v_s, lf_s, wkv_s, sh_s)
```

```python
import functools

import jax
import jax.numpy as jnp
from jax import lax
from jax.experimental import pallas as pl
from jax.experimental.pallas import tpu as pltpu

F32 = jnp.float32
BF16 = jnp.bfloat16

HEAD_DIM = 64
N_HEADS = 8
MIX_HALF = N_HEADS * HEAD_DIM
PAIR = 2 * HEAD_DIM
N_PAIRS = N_HEADS // 2
DECAY_RANK = 64
ICLR_RANK = 64
GATE_RANK = 160
RWKV_PROJ = 3 * MIX_HALF + DECAY_RANK + ICLR_RANK + GATE_RANK
LANES = 128
DECAY_PAD = 128
ICLR_PAD = 128
GATE_PAD = 256
HR_WIDTH = 3 * MIX_HALF + DECAY_PAD + ICLR_PAD + GATE_PAD
RMS_EPS = 1e-6
GN_EPS = 64e-5
NEG_INF = -1e30
VMEM_LIMIT = 56 * 1024 * 1024

REC_CHUNK = 64
ATT_BLOCK = 256
DEC_PAGES_PER_STEP = 8


def _params(*sem):
    return pltpu.CompilerParams(dimension_semantics=sem, vmem_limit_bytes=VMEM_LIMIT)


def _dot(a, b):
    return jnp.dot(a, b, preferred_element_type=F32)


def _dot_nt(a, b):
    return lax.dot_general(a, b, (((1,), (1,)), ((), ())), preferred_element_type=F32)


def _split3(x):
    x1 = x.astype(BF16)
    r1 = x - x1.astype(F32)
    x2 = r1.astype(BF16)
    r2 = r1 - x2.astype(F32)
    return x1, x2, r2.astype(BF16)


def _split2(x):
    x1 = x.astype(BF16)
    return x1, (x - x1.astype(F32)).astype(BF16)


def _dot_x01(x, m01):
    x1, x2, x3 = _split3(x)
    return _dot(x1, m01) + _dot(x2, m01) + _dot(x3, m01)


def _dot_01x(m01, x):
    x1, x2, x3 = _split3(x)
    return _dot(m01, x1) + _dot(m01, x2) + _dot(m01, x3)


def _dot3(a, b, nt=False):
    a1, a2 = a if isinstance(a, tuple) else _split2(a)
    b1, b2 = b if isinstance(b, tuple) else _split2(b)
    f = _dot_nt if nt else _dot
    return f(a1, b1) + f(a1, b2) + f(a2, b1)


def _rms(x, g):
    ms = jnp.mean(x * x, axis=-1, keepdims=True)
    return x * lax.rsqrt(ms + RMS_EPS) * g


def _sigmoid(x):
    return 1.0 / (1.0 + jnp.exp(-x))


def _softplus(z):
    return jnp.maximum(z, 0.0) + jnp.log(1.0 + jnp.exp(-jnp.abs(z)))


def _ffn_kernel(x_ref, g_ref, wg_ref, wu_ref, wd_ref, o_ref, xn_ref, acc_ref, *, nf):
    j = pl.program_id(1)

    @pl.when(j == 0)
    def _():
        xn_ref[...] = _rms(x_ref[...], g_ref[...]).astype(BF16)
        acc_ref[...] = jnp.zeros_like(acc_ref)

    xn = xn_ref[...]
    gate = _dot(xn, wg_ref[...])
    up = _dot(xn, wu_ref[...])
    h = (gate * _sigmoid(gate) * up).astype(BF16)
    acc_ref[...] += _dot(h, wd_ref[...])

    @pl.when(j == nf - 1)
    def _():
        o_ref[...] = x_ref[...] + 0.5 * acc_ref[...]


def _ffn(x, g, wg, wu, wd, *, tm, tf):
    n, d = x.shape
    f = wg.shape[1]
    nf = f // tf
    return pl.pallas_call(
        functools.partial(_ffn_kernel, nf=nf),
        grid=(n // tm, nf),
        in_specs=[
            pl.BlockSpec((tm, d), lambda i, j: (i, 0)),
            pl.BlockSpec((1, d), lambda i, j: (0, 0)),
            pl.BlockSpec((d, tf), lambda i, j: (0, j)),
            pl.BlockSpec((d, tf), lambda i, j: (0, j)),
            pl.BlockSpec((tf, d), lambda i, j: (j, 0)),
        ],
        out_specs=pl.BlockSpec((tm, d), lambda i, j: (i, 0)),
        out_shape=jax.ShapeDtypeStruct((n, d), F32),
        scratch_shapes=[pltpu.VMEM((tm, d), BF16), pltpu.VMEM((tm, d), F32)],
        compiler_params=_params("parallel", "arbitrary"),
        name="ffn",
    )(x, g, wg, wu, wd)


def _proj_kernel(x_ref, g_ref, w_ref, wt_ref, bf_ref, bfc_ref, *out_refs, prompt):
    xn = _rms(x_ref[...], g_ref[...]).astype(BF16)
    h = MIX_HALF
    if prompt:
        q_ref, k_ref, kt_ref, vt_ref, hr_ref, lf_ref, lft_ref = out_refs
    else:
        q_ref, k_ref, v_ref, hr_ref, lf_ref = out_refs
    q_ref[...] = _dot(xn, w_ref[:, 0:h])
    k_ref[...] = _dot(xn, w_ref[:, h:2 * h])
    hr_ref[...] = _dot(xn, w_ref[:, 3 * h:3 * h + HR_WIDTH])
    lf_ref[...] = -_softplus(-(_dot(xn, w_ref[:, 3 * h + HR_WIDTH:]) + bf_ref[...]))
    if prompt:
        kt_ref[0] = _dot_nt(wt_ref[0:h, :], xn)
        vt_ref[0] = _dot_nt(wt_ref[h:2 * h, :], xn)
        zt = _dot_nt(wt_ref[2 * h:, :], xn)[0:N_HEADS, :] + bfc_ref[...]
        lft_ref[0] = -_softplus(-zt)
    else:
        v_ref[...] = _dot(xn, w_ref[:, 2 * h:3 * h])


def _proj(x, g, w, wt, bf, bfc, *, tm, seq, prompt):
    n, d = x.shape
    h = MIX_HALF
    nj = seq // tm
    row = lambda i: (i, 0)
    fixed = lambda i: (0, 0)
    trans = lambda i: (i // nj, 0, i % nj)
    rows = lambda width: (pl.BlockSpec((tm, width), row), jax.ShapeDtypeStruct((n, width), F32))
    cols = lambda height: (pl.BlockSpec((1, height, tm), trans),
                           jax.ShapeDtypeStruct((n // seq, height, seq), F32))
    if prompt:
        outs = [rows(h), rows(h), cols(h), cols(h), rows(HR_WIDTH), rows(LANES), cols(N_HEADS)]
    else:
        outs = [rows(h), rows(h), rows(h), rows(HR_WIDTH), rows(LANES)]
    return pl.pallas_call(
        functools.partial(_proj_kernel, prompt=prompt),
        grid=(n // tm,),
        in_specs=[
            pl.BlockSpec((tm, d), row),
            pl.BlockSpec((1, d), fixed),
            pl.BlockSpec(w.shape, fixed),
            pl.BlockSpec(wt.shape, fixed),
            pl.BlockSpec((1, LANES), fixed),
            pl.BlockSpec((N_HEADS, 1), fixed),
        ],
        out_specs=[o[0] for o in outs],
        out_shape=[o[1] for o in outs],
        compiler_params=_params("parallel"),
        name="proj",
    )(x, g, w, wt, bf, bfc)


def _rwkv_prep_kernel(h_ref, hp_ref, p0_ref, mu_ref, w0_ref, a0_ref, kk_ref, ka_ref, rk_ref,
                      wd_ref, wi_ref, wg_ref, bd_ref,
                      r_o, lw_o, k_o, v_o, na_o, bb_o, g_o, bo_o):
    j = pl.program_id(1)
    h = h_ref[0]
    prev = jnp.where(j == 0, p0_ref[0], hp_ref[0, 7:8, :])
    row = lax.broadcasted_iota(jnp.int32, h.shape, 0)
    shifted = jnp.where(row == 0, prev, pltpu.roll(h, 1, axis=0))
    hs = h + mu_ref[...] * (shifted - h)
    m = MIX_HALF
    r = hs[:, 0:m]
    k = hs[:, m:2 * m]
    v = hs[:, 2 * m:3 * m]
    o = 3 * m
    d_decay = hs[:, o:o + DECAY_PAD]
    d_iclr = hs[:, o + DECAY_PAD:o + DECAY_PAD + ICLR_PAD]
    d_gate = hs[:, o + DECAY_PAD + ICLR_PAD:]
    w_log = -_softplus(-(w0_ref[...] + _dot(jnp.tanh(d_decay).astype(BF16), wd_ref[...]))) - 0.5
    a = _sigmoid(a0_ref[...] + _dot(d_iclr.astype(BF16), wi_ref[...]))
    g = _dot(_sigmoid(d_gate).astype(BF16), wg_ref[...])
    bd = bd_ref[...]
    kk = k * kk_ref[...]
    kk = kk / jnp.maximum(jnp.sqrt(_dot_x01(kk * kk, bd)), 1e-12)
    k = k * (1.0 + (a - 1.0) * ka_ref[...])
    r_o[0] = r
    lw_o[0] = -jnp.exp(w_log)
    k_o[0] = k
    v_o[0] = v
    na_o[0] = -kk
    bb_o[0] = kk * a
    g_o[0] = g
    bo_o[0] = _dot_x01(r * k * rk_ref[...], bd) * v


def _rwkv_prep(hr, prev0, lp, *, tt):
    b, t, _ = hr.shape
    m = MIX_HALF
    tile = lambda i, j: (i, j, 0)
    fixed = lambda i, j: (0, 0)
    vec = pl.BlockSpec((1, m), fixed)
    out = jax.ShapeDtypeStruct((b, t, m), F32)
    return pl.pallas_call(
        _rwkv_prep_kernel,
        grid=(b, t // tt),
        in_specs=[
            pl.BlockSpec((1, tt, HR_WIDTH), tile),
            pl.BlockSpec((1, 8, HR_WIDTH), lambda i, j: (i, jnp.maximum(j * (tt // 8) - 1, 0), 0)),
            pl.BlockSpec((1, 1, HR_WIDTH), lambda i, j: (i, 0, 0)),
            pl.BlockSpec((1, HR_WIDTH), fixed),
            vec, vec, vec, vec, vec,
            pl.BlockSpec((DECAY_PAD, m), fixed),
            pl.BlockSpec((ICLR_PAD, m), fixed),
            pl.BlockSpec((GATE_PAD, m), fixed),
            pl.BlockSpec((m, m), fixed),
        ],
        out_specs=[pl.BlockSpec((1, tt, m), tile)] * 8,
        out_shape=[out] * 8,
        compiler_params=_params("parallel", "arbitrary"),
        name="rwkv_prep",
    )(hr, hr, prev0, lp["mu"], lp["w0"], lp["a0"], lp["k_k"], lp["k_a"], lp["r_k"],
      lp["w_decay"], lp["w_iclr"], lp["w_gate"], lp["bd512"])


def _rwkv_rec_kernel(r_ref, lw_ref, k_ref, v_ref, a_ref, b_ref, g_ref, bo_ref, s0_ref,
                     lng_ref, lnb_ref, bd_ref, y_ref, sf_ref, s_ref, *, chunk, nchunks):
    c = pl.program_id(1)
    C = chunk
    C2 = 2 * C

    @pl.when(c == 0)
    def _():
        s_ref[...] = s0_ref[0]

    row = lax.broadcasted_iota(jnp.int32, (C2, C2), 0)
    col = lax.broadcasted_iota(jnp.int32, (C2, C2), 1)
    rt = row & (C - 1)
    ct = col & (C - 1)
    strict = rt > ct
    incl = rt >= ct
    eye = (row == col).astype(F32)
    ti = lax.broadcasted_iota(jnp.int32, (C, C), 0)
    tj = lax.broadcasted_iota(jnp.int32, (C, C), 1)
    tril01 = (ti >= tj).astype(BF16)
    lane = lax.broadcasted_iota(jnp.int32, (1, PAIR), 1)
    m0 = (lane < HEAD_DIM).astype(F32)
    m1 = 1.0 - m0
    bd = bd_ref[...]

    def stack(x):
        return jnp.concatenate([x * m0, x * m1], axis=0)

    for hp in range(N_PAIRS):
        sl = slice(hp * PAIR, (hp + 1) * PAIR)
        r = r_ref[0, :, sl]
        lw = lw_ref[0, :, sl]
        k = k_ref[0, :, sl]
        v = v_ref[0, :, sl]
        a = a_ref[0, :, sl]
        b = b_ref[0, :, sl]
        cum = _dot_01x(tril01, lw)
        cum_end = cum[C - 1:C, :]
        e_neg = jnp.exp(-cum)
        e_end = jnp.exp(cum_end - cum)
        at2 = stack(a * jnp.exp(cum - lw))
        rt2 = stack(r * jnp.exp(cum))
        bt2 = stack(b * e_neg)
        kt2 = stack(k * e_neg)
        bh2 = stack(b * e_end)
        kh2 = stack(k * e_end)
        v2 = stack(v)
        s = s_ref[hp]

        ar = _split2(jnp.concatenate([at2, rt2], axis=0))
        bk = _split2(jnp.concatenate([bt2, kt2], axis=0))
        mm = _dot3(ar, bk, nt=True)
        n_ab = jnp.where(strict, mm[0:C2, 0:C2], 0.0)
        a_ak = jnp.where(strict, mm[0:C2, C2:], 0.0)
        a_rb = jnp.where(incl, mm[C2:, 0:C2], 0.0)
        a_rk = jnp.where(incl, mm[C2:, C2:], 0.0)

        t_inv = eye + jnp.where((rt >> 1) == (ct >> 1), n_ab, 0.0)
        lvl = 1
        while (1 << lvl) < C:
            off = jnp.where(((rt >> (lvl + 1)) == (ct >> (lvl + 1))) & ((rt >> lvl) != (ct >> lvl)), n_ab, 0.0)
            ts = _split2(t_inv)
            t_inv = t_inv + _dot3(ts, _split2(_dot3(_split2(off), ts)))
            lvl += 1

        xs = _dot3(ar, _split2(s), nt=True)
        v2s = _split2(v2)
        x2 = xs[0:C2] + _dot3(a_ak, v2s)
        u2 = _dot3(t_inv, x2)
        u2s = _split2(u2)
        y2 = xs[C2:] + _dot3(a_rb, u2s) + _dot3(a_rk, v2s)
        y = y2[0:C] + y2[C:]

        uv_t = jnp.concatenate([u2, v2], axis=0).T
        s_new = s * jnp.exp(cum_end) + _dot3(uv_t, jnp.concatenate([bh2, kh2], axis=0))
        s_ref[hp] = s_new

        mean = _dot_x01(y, bd) * (1.0 / HEAD_DIM)
        yc = y - mean
        var = _dot_x01(yc * yc, bd) * (1.0 / HEAD_DIM)
        yn = yc * lax.rsqrt(var + GN_EPS) * lng_ref[:, sl] + lnb_ref[:, sl]
        y_ref[0, :, sl] = (yn + bo_ref[0, :, sl]) * g_ref[0, :, sl]

    @pl.when(c == nchunks - 1)
    def _():
        sf_ref[0] = s_ref[...]


def _rwkv_rec(r, lw, k, v, a, b, g, bonus, s0, lp, *, chunk):
    bsz, t, m = r.shape
    nchunks = t // chunk
    tile = pl.BlockSpec((1, chunk, m), lambda i, c: (i, c, 0))
    st = pl.BlockSpec((1, N_PAIRS, PAIR, PAIR), lambda i, c: (i, 0, 0, 0))
    vec = pl.BlockSpec((1, m), lambda i, c: (0, 0))
    return pl.pallas_call(
        functools.partial(_rwkv_rec_kernel, chunk=chunk, nchunks=nchunks),
        grid=(bsz, nchunks),
        in_specs=[tile] * 8 + [st, vec, vec, pl.BlockSpec((PAIR, PAIR), lambda i, c: (0, 0))],
        out_specs=[tile, st],
        out_shape=[jax.ShapeDtypeStruct((bsz, t, m), F32),
                   jax.ShapeDtypeStruct((bsz, N_PAIRS, PAIR, PAIR), F32)],
        scratch_shapes=[pltpu.VMEM((N_PAIRS, PAIR, PAIR), F32)],
        compiler_params=_params("parallel", "arbitrary"),
        name="rwkv_rec",
    )(r, lw, k, v, a, b, g, bonus, s0, lp["lnx_g"], lp["lnx_b"], lp["bd128"])


def _cumsum_kernel(lf_ref, lft_ref, c_ref, ct_ref, *, blk, nblk):
    ti = lax.broadcasted_iota(jnp.int32, (blk, blk), 0)
    tj = lax.broadcasted_iota(jnp.int32, (blk, blk), 1)
    tril01 = (ti >= tj).astype(BF16)
    triu01 = (ti <= tj).astype(BF16)
    carry = jnp.zeros((1, LANES), F32)
    carry_t = jnp.zeros((N_HEADS, 1), F32)
    for i in range(nblk):
        sl = slice(i * blk, (i + 1) * blk)
        c = _dot_01x(tril01, lf_ref[0, sl, :]) + carry
        c_ref[0, sl, :] = c
        carry = c[blk - 1:blk, :]
        ct = _dot_x01(lft_ref[0, :, sl], triu01) + carry_t
        ct_ref[0, :, sl] = ct
        carry_t = ct[:, blk - 1:blk]


def _cumsum(lf, lft, *, blk):
    b, s, w = lf.shape
    spec = pl.BlockSpec((1, s, w), lambda i: (i, 0, 0))
    spec_t = pl.BlockSpec((1, N_HEADS, s), lambda i: (i, 0, 0))
    return pl.pallas_call(
        functools.partial(_cumsum_kernel, blk=blk, nblk=s // blk),
        grid=(b,),
        in_specs=[spec, spec_t],
        out_specs=[spec, spec_t],
        out_shape=[jax.ShapeDtypeStruct((b, s, w), F32), jax.ShapeDtypeStruct((b, N_HEADS, s), F32)],
        compiler_params=_params("parallel"),
        name="logf_cumsum",
    )(lf, lft)


def _fox_prompt_kernel(q_ref, k_ref, vt_ref, cq_ref, ck_ref, gain_ref, o_ref, *, blk):
    qi = pl.program_id(1)
    lane = lax.broadcasted_iota(jnp.int32, (1, PAIR), 1)
    ki = lax.broadcasted_iota(jnp.int32, (blk, blk), 0)
    qj = lax.broadcasted_iota(jnp.int32, (blk, blk), 1)
    causal = ki <= qj
    scale = HEAD_DIM ** -0.5

    for hp in range(N_PAIRS):
        sl = slice(hp * PAIR, (hp + 1) * PAIR)
        qp = q_ref[0, :, sl] * scale
        qm = [jnp.where(lane < HEAD_DIM, qp, 0.0).astype(BF16),
              jnp.where(lane >= HEAD_DIM, qp, 0.0).astype(BF16)]
        cq = [cq_ref[0, 2 * hp + i:2 * hp + i + 1, :] for i in range(2)]

        def block(j, carry, diagonal):
            ks = pl.ds(pl.multiple_of(j * blk, blk), blk)
            kb = k_ref[0, ks, sl].astype(BF16)
            vtb = vt_ref[0, sl, ks].astype(BF16)
            out = []
            for i in range(2):
                m, l, acc = carry[i]
                h = 2 * hp + i
                st = _dot_nt(kb, qm[i]) + (cq[i] - ck_ref[0, ks, h:h + 1])
                if diagonal:
                    st = jnp.where(causal, st, NEG_INF)
                m_new = jnp.maximum(m, jnp.max(st, axis=0, keepdims=True))
                alpha = jnp.exp(m - m_new)
                p = jnp.exp(st - m_new)
                l = alpha * l + jnp.sum(p, axis=0, keepdims=True)
                pv = _dot(vtb[i * HEAD_DIM:(i + 1) * HEAD_DIM, :], p.astype(BF16))
                out.append((m_new, l, alpha * acc + pv))
            return tuple(out)

        init = tuple((jnp.full((1, blk), NEG_INF, F32), jnp.zeros((1, blk), F32),
                      jnp.zeros((HEAD_DIM, blk), F32)) for _ in range(2))
        carry = lax.fori_loop(0, qi, lambda j, cr: block(j, cr, False), init)
        carry = block(qi, carry, True)
        halves = []
        for i in range(2):
            _, l, acc = carry[i]
            o = acc / l
            ms = jnp.mean(o * o, axis=0, keepdims=True)
            halves.append(o * lax.rsqrt(ms + RMS_EPS))
        o_ref[0, :, sl] = jnp.concatenate(halves, axis=0).T * gain_ref[:, sl]


def _fox_prompt(q, k, vt, cq_t, ck, gain, *, blk):
    b, s, m = q.shape
    nb = s // blk
    return pl.pallas_call(
        functools.partial(_fox_prompt_kernel, blk=blk),
        grid=(b, nb),
        in_specs=[
            pl.BlockSpec((1, blk, m), lambda i, j: (i, j, 0)),
            pl.BlockSpec((1, s, m), lambda i, j: (i, 0, 0)),
            pl.BlockSpec((1, m, s), lambda i, j: (i, 0, 0)),
            pl.BlockSpec((1, N_HEADS, blk), lambda i, j: (i, 0, j)),
            pl.BlockSpec((1, s, LANES), lambda i, j: (i, 0, 0)),
            pl.BlockSpec((1, m), lambda i, j: (0, 0)),
        ],
        out_specs=pl.BlockSpec((1, blk, m), lambda i, j: (i, j, 0)),
        out_shape=jax.ShapeDtypeStruct((b, s, m), F32),
        compiler_params=_params("parallel", "arbitrary"),
        name="fox_prompt",
    )(q, k, vt, cq_t, ck, gain)


def _fox_decode_kernel(pt_ref, q_ref, kn_ref, vn_ref, lfn_ref, lfnt_ref, gain_ref, hm_ref, bd_ref, *rest,
                       npg, nsteps, tnew):
    k_pages = rest[0:npg]
    v_pages = rest[npg:2 * npg]
    lf_pages = rest[2 * npg:3 * npg]
    o_ref, m_s, l_s, acc_s, carry_s, qbd_s, cn_s = rest[3 * npg:]
    g = pl.program_id(1)
    nrow = tnew * N_HEADS
    page = k_pages[0].shape[3]
    li = lax.broadcasted_iota(jnp.int32, (LANES, LANES), 0)
    lj = lax.broadcasted_iota(jnp.int32, (LANES, LANES), 1)
    hm = hm_ref[...]

    def tile_rows(x):
        return jnp.concatenate([x] * tnew, axis=0)

    @pl.when(g == 0)
    def _():
        q = q_ref[0] * (HEAD_DIM ** -0.5)
        qbd = jnp.concatenate([jnp.broadcast_to(q[t:t + 1, :], hm.shape) * hm for t in range(tnew)], axis=0)
        qbd_s[...] = qbd.astype(BF16)
        cn = lfn_ref[0]
        trow = lax.broadcasted_iota(jnp.int32, cn.shape, 0)
        sh = 1
        while sh < tnew:
            cn = cn + jnp.where(trow >= sh, pltpu.roll(cn, sh, axis=0), 0.0)
            sh *= 2
        hsel = (lax.broadcasted_iota(jnp.int32, (N_HEADS, LANES), 0)
                == lax.broadcasted_iota(jnp.int32, (N_HEADS, LANES), 1)).astype(F32)
        cn_rows = jnp.concatenate(
            [jnp.sum(jnp.broadcast_to(cn[t:t + 1, :], hsel.shape) * hsel, axis=1, keepdims=True)
             for t in range(tnew)], axis=0)
        cn_s[...] = cn_rows
        cnt = _dot_x01(lfnt_ref[0], (li <= lj).astype(BF16))
        pad = jnp.zeros((page - tnew, MIX_HALF), F32)
        kn = jnp.concatenate([kn_ref[0], pad], axis=0).astype(BF16)
        vn = jnp.concatenate([vn_ref[0], pad], axis=0).astype(BF16)
        s = _dot_nt(qbd_s[...], kn) + (cn_rows - tile_rows(cnt))
        rtok = lax.broadcasted_iota(jnp.int32, (nrow, LANES), 0) >> 3
        ktok = lax.broadcasted_iota(jnp.int32, (nrow, LANES), 1)
        s = jnp.where(ktok <= rtok, s, NEG_INF)
        m = jnp.max(s, axis=1, keepdims=True)
        p = jnp.exp(s - m)
        m_s[...] = m
        l_s[...] = jnp.sum(p, axis=1, keepdims=True)
        acc_s[...] = _dot(p.astype(BF16), vn)
        carry_s[...] = jnp.zeros_like(carry_s)

    triu_strict = (li > lj).astype(BF16)
    run = carry_s[...]
    qbd = qbd_s[...]
    cn_rows = cn_s[...]
    scores = [None] * npg
    for i in range(npg - 1, -1, -1):
        lf = lf_pages[i][0, 0]
        bias = _dot_x01(lf, triu_strict) + run
        run = run + jnp.sum(lf, axis=1, keepdims=True)
        scores[i] = _dot(qbd, k_pages[i][0, 0].astype(BF16)) + (tile_rows(bias) + cn_rows)
    carry_s[...] = run
    s = jnp.concatenate(scores, axis=1)
    m_old = m_s[...]
    m_new = jnp.maximum(m_old, jnp.max(s, axis=1, keepdims=True))
    alpha = jnp.exp(m_old - m_new)
    p = jnp.exp(s - m_new)
    m_s[...] = m_new
    l_s[...] = alpha * l_s[...] + jnp.sum(p, axis=1, keepdims=True)
    pv = _dot_nt(p[:, 0:page].astype(BF16), v_pages[0][0, 0].astype(BF16))
    for i in range(1, npg):
        pv = pv + _dot_nt(p[:, i * page:(i + 1) * page].astype(BF16), v_pages[i][0, 0].astype(BF16))
    acc_s[...] = alpha * acc_s[...] + pv

    @pl.when(g == nsteps - 1)
    def _():
        o_rows = acc_s[...] / l_s[...]
        o = jnp.concatenate(
            [jnp.sum(o_rows[t * N_HEADS:(t + 1) * N_HEADS, :] * hm, axis=0, keepdims=True) for t in range(tnew)],
            axis=0)
        ms = _dot_x01(o * o, bd_ref[...]) * (1.0 / HEAD_DIM)
        o_ref[0] = o * lax.rsqrt(ms + RMS_EPS) * gain_ref[...]


def _fox_decode(layer, page_table, q, kn, vn, lfn, lfn_t, cache_k, cache_v, cache_lft, gain, hm, bd, *, npg):
    b, tnew, m = q.shape
    n_pages = page_table.shape[1]
    page = cache_k.shape[3]
    nsteps = n_pages // npg
    nrow = tnew * N_HEADS

    def tok(i, g, pt):
        return (i, 0, 0)

    def fixed(i, g, pt):
        return (0, 0)

    def page_map(slot):
        return lambda i, g, pt: (layer, pt[i, (nsteps - 1 - g) * npg + slot], 0, 0)

    kv_specs = [pl.BlockSpec((1, 1, m, page), page_map(s)) for s in range(npg)]
    lf_specs = [pl.BlockSpec((1, 1, N_HEADS, page), page_map(s)) for s in range(npg)]
    grid_spec = pltpu.PrefetchScalarGridSpec(
        num_scalar_prefetch=1,
        grid=(b, nsteps),
        in_specs=[
            pl.BlockSpec((1, tnew, m), tok),
            pl.BlockSpec((1, tnew, m), tok),
            pl.BlockSpec((1, tnew, m), tok),
            pl.BlockSpec((1, tnew, LANES), tok),
            pl.BlockSpec((1, N_HEADS, LANES), tok),
            pl.BlockSpec((1, m), fixed),
            pl.BlockSpec((N_HEADS, m), fixed),
            pl.BlockSpec((m, m), fixed),
        ] + kv_specs + kv_specs + lf_specs,
        out_specs=pl.BlockSpec((1, tnew, m), tok),
        scratch_shapes=[
            pltpu.VMEM((nrow, 1), F32),
            pltpu.VMEM((nrow, 1), F32),
            pltpu.VMEM((nrow, m), F32),
            pltpu.VMEM((N_HEADS, 1), F32),
            pltpu.VMEM((nrow, m), BF16),
            pltpu.VMEM((nrow, 1), F32),
        ],
    )
    return pl.pallas_call(
        functools.partial(_fox_decode_kernel, npg=npg, nsteps=nsteps, tnew=tnew),
        grid_spec=grid_spec,
        out_shape=jax.ShapeDtypeStruct((b, tnew, m), F32),
        compiler_params=_params("parallel", "arbitrary"),
        name="fox_decode",
    )(page_table, q, kn, vn, lfn, lfn_t, gain, hm, bd,
      *([cache_k] * npg), *([cache_v] * npg), *([cache_lft] * npg))


def _out_proj_kernel(x_ref, yr_ref, yf_ref, wr_ref, wf_ref, o_ref):
    o_ref[...] = (x_ref[...] + _dot(yr_ref[...].astype(BF16), wr_ref[...])
                  + _dot(yf_ref[...].astype(BF16), wf_ref[...]))


def _out_proj(x, yr, yf, wr, wf, *, tm):
    n, d = x.shape
    m = yr.shape[1]
    row = lambda i: (i, 0)
    fixed = lambda i: (0, 0)
    return pl.pallas_call(
        _out_proj_kernel,
        grid=(n // tm,),
        in_specs=[pl.BlockSpec((tm, d), row), pl.BlockSpec((tm, m), row), pl.BlockSpec((tm, m), row),
                  pl.BlockSpec((m, d), fixed), pl.BlockSpec((m, d), fixed)],
        out_specs=pl.BlockSpec((tm, d), row),
        out_shape=jax.ShapeDtypeStruct((n, d), F32),
        compiler_params=_params("parallel"),
        name="out_proj",
    )(x, yr, yf, wr, wf)


def _ple_kernel(x_ref, p_ref, g_ref, wg_ref, wu_ref, fg_ref, o_ref, *, final):
    x = x_ref[...]
    gate = _sigmoid(_dot(_rms(x, g_ref[...]).astype(BF16), wg_ref[...]))
    y = x + gate * _dot(p_ref[...].astype(BF16), wu_ref[...])
    o_ref[...] = _rms(y, fg_ref[...]) if final else y


def _ple(x, p, g, wg, wu, fg, *, tm, final):
    n, d = x.shape
    pd = p.shape[1]
    row = lambda i: (i, 0)
    fixed = lambda i: (0, 0)
    return pl.pallas_call(
        functools.partial(_ple_kernel, final=final),
        grid=(n // tm,),
        in_specs=[pl.BlockSpec((tm, d), row), pl.BlockSpec((tm, pd), row), pl.BlockSpec((1, d), fixed),
                  pl.BlockSpec((d, d), fixed), pl.BlockSpec((pd, d), fixed), pl.BlockSpec((1, d), fixed)],
        out_specs=pl.BlockSpec((tm, d), row),
        out_shape=jax.ShapeDtypeStruct((n, d), F32),
        compiler_params=_params("parallel"),
        name="ple",
    )(x, p, g, wg, wu, fg)


def _pad_cols(w, width):
    return jnp.pad(w, [(0, 0)] * (w.ndim - 1) + [(0, width - w.shape[-1])])


def _pad_rows(w, height):
    return jnp.pad(w, [(0, 0)] * (w.ndim - 2) + [(0, height - w.shape[-2]), (0, 0)])


def _pack_hr(h):
    m = MIX_HALF
    o = 3 * m
    return jnp.concatenate([
        h[..., :o],
        _pad_cols(h[..., o:o + DECAY_RANK], DECAY_PAD),
        _pad_cols(h[..., o + DECAY_RANK:o + DECAY_RANK + ICLR_RANK], ICLR_PAD),
        _pad_cols(h[..., o + DECAY_RANK + ICLR_RANK:], GATE_PAD)], axis=-1)


def _unpack_hr(h):
    m = MIX_HALF
    o = 3 * m
    return jnp.concatenate([
        h[..., :o],
        h[..., o:o + DECAY_RANK],
        h[..., o + DECAY_PAD:o + DECAY_PAD + ICLR_RANK],
        h[..., o + DECAY_PAD + ICLR_PAD:o + DECAY_PAD + ICLR_PAD + GATE_RANK]], axis=-1)


def _pair_state(s):
    b = s.shape[0]
    s = s.reshape(b, N_PAIRS, 2, HEAD_DIM, HEAD_DIM)
    z = jnp.zeros_like(s[:, :, 0])
    top = jnp.concatenate([s[:, :, 0], z], axis=-1)
    bot = jnp.concatenate([z, s[:, :, 1]], axis=-1)
    return jnp.concatenate([top, bot], axis=-2)


def _unpair_state(s):
    b = s.shape[0]
    d = HEAD_DIM
    return jnp.stack([s[:, :, :d, :d], s[:, :, d:, d:]], axis=2).reshape(b, N_HEADS, d, d)


def _block_diag01(n):
    i = jnp.arange(n) // HEAD_DIM
    return (i[:, None] == i[None, :]).astype(BF16)


def kernel(x_prompt, x_sample, cache_k, cache_v, cache_logf, state_wkv, state_shift, page_table, p_prompt, p_sample, ffn1_norm, ffn1_w_gate, ffn1_w_up, ffn1_w_down, mix_norm, w_in, rwkv_mu, rwkv_w0, rwkv_w_decay, rwkv_a0, rwkv_w_iclr, rwkv_w_gate, rwkv_k_k, rwkv_k_a, rwkv_r_k, rwkv_lnx_g, rwkv_lnx_b, fox_b_f, fox_out_norm, w_out, ffn2_norm, ffn2_w_gate, ffn2_w_up, ffn2_w_down, ple_norm, ple_w_gate, ple_w_up, final_norm):
    depth = w_in.shape[0]
    bp, sp, d = x_prompt.shape
    bs, ts, _ = x_sample.shape
    m = MIX_HALF
    n_pool, page = cache_k.shape[1], cache_k.shape[2]
    npr, nsm = bp * sp, bs * ts
    pd = p_prompt.shape[-1]

    fox_cols = w_in[:, :, RWKV_PROJ:]
    w_pack = jnp.concatenate([
        fox_cols[:, :, :3 * m],
        _pack_hr(w_in[:, :, :RWKV_PROJ]),
        _pad_cols(fox_cols[:, :, 3 * m:], LANES)], axis=-1).astype(BF16)
    w_t = jnp.swapaxes(_pad_cols(fox_cols[:, :, m:], 2 * m + LANES), 1, 2).astype(BF16)
    b_f = _pad_cols(fox_b_f, LANES)[:, None, :]
    b_fc = fox_b_f[:, :, None]
    bd512 = _block_diag01(m)
    bd128 = _block_diag01(PAIR)
    hm = (jnp.arange(m)[None, :] // HEAD_DIM == jnp.arange(N_HEADS)[:, None]).astype(F32)
    bf = lambda w: w.astype(BF16)
    f1g, f1u, f1d = bf(ffn1_w_gate), bf(ffn1_w_up), bf(ffn1_w_down)
    f2g, f2u, f2d = bf(ffn2_w_gate), bf(ffn2_w_up), bf(ffn2_w_down)
    wo = bf(w_out)
    pg, pu = bf(ple_w_gate), bf(ple_w_up)
    wdec = bf(_pad_rows(rwkv_w_decay, DECAY_PAD))
    wicl = bf(_pad_rows(rwkv_w_iclr, ICLR_PAD))
    wgat = bf(_pad_rows(rwkv_w_gate, GATE_PAD))
    mu = _pack_hr(rwkv_mu)
    cache_kt = jnp.transpose(cache_k, (0, 1, 3, 4, 2)).reshape(depth, n_pool, m, page)
    cache_vt = jnp.transpose(cache_v, (0, 1, 3, 4, 2)).reshape(depth, n_pool, m, page)
    cache_lft = jnp.swapaxes(cache_logf, 2, 3)
    fg = final_norm[None, :]

    tm_p = 1024 if npr % 1024 == 0 else ATT_BLOCK
    tf = 256
    blk = ATT_BLOCK
    chunk = REC_CHUNK

    xp = x_prompt.reshape(npr, d)
    xs = x_sample.reshape(nsm, d)
    outs = [[] for _ in range(10)]
    for l in range(depth):
        lp = dict(mu=mu[l][None], w0=rwkv_w0[l][None], a0=rwkv_a0[l][None], k_k=rwkv_k_k[l][None],
                  k_a=rwkv_k_a[l][None], r_k=rwkv_r_k[l].reshape(1, m), w_decay=wdec[l], w_iclr=wicl[l],
                  w_gate=wgat[l], bd512=bd512, bd128=bd128, lnx_g=rwkv_lnx_g[l][None],
                  lnx_b=rwkv_lnx_b[l][None])
        gain = fox_out_norm[l].reshape(1, m)
        last = l == depth - 1

        xp = _ffn(xp, ffn1_norm[l][None], f1g[l], f1u[l], f1d[l], tm=tm_p, tf=tf)
        q, k, kt, vt, hr, lf, lft = _proj(xp, mix_norm[l][None], w_pack[l], w_t[l], b_f[l], b_fc[l],
                                          tm=blk, seq=sp, prompt=True)
        hr3 = hr.reshape(bp, sp, HR_WIDTH)
        prep = _rwkv_prep(hr3, jnp.zeros((bp, 1, HR_WIDTH), F32), lp, tt=min(sp, 256))
        y_r, s_fin = _rwkv_rec(*prep, jnp.zeros((bp, N_PAIRS, PAIR, PAIR), F32), lp, chunk=chunk)
        c, c_t = _cumsum(lf.reshape(bp, sp, LANES), lft, blk=blk)
        y_f = _fox_prompt(q.reshape(bp, sp, m), k.reshape(bp, sp, m), vt, c_t, c, gain, blk=blk)
        xp = _out_proj(xp, y_r.reshape(npr, m), y_f.reshape(npr, m), wo[l, :m], wo[l, m:], tm=tm_p)
        xp = _ffn(xp, ffn2_norm[l][None], f2g[l], f2u[l], f2d[l], tm=tm_p, tf=tf)
        xp = _ple(xp, p_prompt[l].reshape(npr, pd), ple_norm[l][None], pg[l], pu[l], fg, tm=tm_p, final=last)
        outs[0].append(kt)
        outs[1].append(vt)
        outs[2].append(lft)
        outs[3].append(_unpair_state(s_fin))
        outs[4].append(_unpack_hr(hr3[:, -1, :]))

        xs = _ffn(xs, ffn1_norm[l][None], f1g[l], f1u[l], f1d[l], tm=nsm, tf=tf)
        q, k, v, hr, lf = _proj(xs, mix_norm[l][None], w_pack[l], w_t[l], b_f[l], b_fc[l],
                                tm=nsm, seq=nsm, prompt=False)
        hr3 = hr.reshape(bs, ts, HR_WIDTH)
        prep = _rwkv_prep(hr3, _pack_hr(state_shift[l])[:, None, :], lp, tt=ts)
        prep = [jnp.pad(t, ((0, 0), (0, chunk - ts), (0, 0))) for t in prep]
        y_r, s_fin = _rwkv_rec(*prep, _pair_state(state_wkv[l]), lp, chunk=chunk)
        lf3 = lf.reshape(bs, ts, LANES)
        lf_t = _pad_cols(jnp.swapaxes(lf3[:, :, :N_HEADS], 1, 2), LANES)
        y_f = _fox_decode(l, page_table, q.reshape(bs, ts, m), k.reshape(bs, ts, m), v.reshape(bs, ts, m),
                          lf3, lf_t, cache_kt, cache_vt, cache_lft, gain, hm, bd512, npg=DEC_PAGES_PER_STEP)
        xs = _out_proj(xs, y_r[:, :ts].reshape(nsm, m), y_f.reshape(nsm, m), wo[l, :m], wo[l, m:], tm=nsm)
        xs = _ffn(xs, ffn2_norm[l][None], f2g[l], f2u[l], f2d[l], tm=nsm, tf=tf)
        xs = _ple(xs, p_sample[l].reshape(nsm, pd), ple_norm[l][None], pg[l], pu[l], fg, tm=nsm, final=last)
        outs[5].append(k.reshape(bs, ts, N_HEADS, HEAD_DIM))
        outs[6].append(v.reshape(bs, ts, N_HEADS, HEAD_DIM))
        outs[7].append(lf3[:, :, :N_HEADS])
        outs[8].append(_unpair_state(s_fin))
        outs[9].append(_unpack_hr(hr3[:, -1, :]))

    st = [jnp.stack(o) for o in outs]
    for i in (0, 1):
        st[i] = jnp.transpose(st[i].reshape(depth, bp, N_HEADS, HEAD_DIM, sp), (0, 1, 4, 2, 3))
    st[2] = jnp.swapaxes(st[2], 2, 3)
    return (xp.reshape(bp, sp, d), xs.reshape(bs, ts, d), *st)
```

```python
import functools

import jax
import jax.numpy as jnp
from jax import lax
from jax.experimental import pallas as pl
from jax.experimental.pallas import tpu as pltpu

F32 = jnp.float32
BF16 = jnp.bfloat16

HEAD_DIM = 64
N_HEADS = 8
MIX_HALF = N_HEADS * HEAD_DIM
PAIR = 2 * HEAD_DIM
N_PAIRS = N_HEADS // 2
DECAY_RANK = 64
ICLR_RANK = 64
GATE_RANK = 160
RWKV_PROJ = 3 * MIX_HALF + DECAY_RANK + ICLR_RANK + GATE_RANK
LANES = 128
DECAY_PAD = 128
ICLR_PAD = 128
GATE_PAD = 256
HR_WIDTH = 3 * MIX_HALF + DECAY_PAD + ICLR_PAD + GATE_PAD
RMS_EPS = 1e-6
GN_EPS = 64e-5
NEG_INF = -1e30
VMEM_LIMIT = 56 * 1024 * 1024

REC_CHUNK = 64
REC_TILE = 256
STATE_TILE = 128
STATE_BATCH = 8
ATT_BLOCK = 256
DEC_PAGES_PER_STEP = 16


def _params(*sem):
    return pltpu.CompilerParams(dimension_semantics=sem, vmem_limit_bytes=VMEM_LIMIT)


def _dot(a, b):
    return jnp.dot(a, b, preferred_element_type=F32)


def _dot_nt(a, b):
    return lax.dot_general(a, b, (((1,), (1,)), ((), ())), preferred_element_type=F32)


def _split3(x):
    x1 = x.astype(BF16)
    r1 = x - x1.astype(F32)
    x2 = r1.astype(BF16)
    r2 = r1 - x2.astype(F32)
    return x1, x2, r2.astype(BF16)


def _dot_x01(x, m01):
    x1, x2, x3 = _split3(x)
    return _dot(x1, m01) + _dot(x2, m01) + _dot(x3, m01)


def _dot_01x(m01, x):
    x1, x2, x3 = _split3(x)
    return _dot(m01, x1) + _dot(m01, x2) + _dot(m01, x3)


def _rms(x, g):
    ms = jnp.mean(x * x, axis=-1, keepdims=True)
    return x * lax.rsqrt(ms + RMS_EPS) * g


def _sigmoid(x):
    return 1.0 / (1.0 + jnp.exp(-x))


def _softplus(z):
    return jnp.maximum(z, 0.0) + jnp.log(1.0 + jnp.exp(-jnp.abs(z)))


def _ffn_kernel(x_ref, g_ref, wg_ref, wu_ref, wd_ref, o_ref, xn_ref, acc_ref, *, nf):
    j = pl.program_id(1)

    @pl.when(j == 0)
    def _():
        xn_ref[...] = _rms(x_ref[...], g_ref[...]).astype(BF16)
        acc_ref[...] = jnp.zeros_like(acc_ref)

    xn = xn_ref[...]
    gate = _dot(xn, wg_ref[...])
    up = _dot(xn, wu_ref[...])
    h = (gate * _sigmoid(gate) * up).astype(BF16)
    acc_ref[...] += _dot(h, wd_ref[...])

    @pl.when(j == nf - 1)
    def _():
        o_ref[...] = x_ref[...] + 0.5 * acc_ref[...]


def _ffn(x, g, wg, wu, wd, *, tm, tf):
    n, d = x.shape
    f = wg.shape[1]
    nf = f // tf
    return pl.pallas_call(
        functools.partial(_ffn_kernel, nf=nf),
        grid=(n // tm, nf),
        in_specs=[
            pl.BlockSpec((tm, d), lambda i, j: (i, 0)),
            pl.BlockSpec((1, d), lambda i, j: (0, 0)),
            pl.BlockSpec((d, tf), lambda i, j: (0, j)),
            pl.BlockSpec((d, tf), lambda i, j: (0, j)),
            pl.BlockSpec((tf, d), lambda i, j: (j, 0)),
        ],
        out_specs=pl.BlockSpec((tm, d), lambda i, j: (i, 0)),
        out_shape=jax.ShapeDtypeStruct((n, d), F32),
        scratch_shapes=[pltpu.VMEM((tm, d), BF16), pltpu.VMEM((tm, d), F32)],
        compiler_params=_params("parallel", "arbitrary"),
        name="ffn",
    )(x, g, wg, wu, wd)


def _proj_kernel(x_ref, g_ref, w_ref, wt_ref, bf_ref, bfc_ref, *out_refs, prompt):
    xn = _rms(x_ref[...], g_ref[...]).astype(BF16)
    h = MIX_HALF
    if prompt:
        q_ref, k_ref, kt_ref, vt_ref, hr_ref, lf_ref, lft_ref = out_refs
    else:
        q_ref, k_ref, v_ref, hr_ref, lf_ref = out_refs
    q_ref[...] = _dot(xn, w_ref[:, 0:h])
    k_ref[...] = _dot(xn, w_ref[:, h:2 * h])
    hr_ref[...] = _dot(xn, w_ref[:, 3 * h:3 * h + HR_WIDTH])
    lf_ref[...] = -_softplus(-(_dot(xn, w_ref[:, 3 * h + HR_WIDTH:]) + bf_ref[...]))
    if prompt:
        kt_ref[0] = _dot_nt(wt_ref[0:h, :], xn)
        vt_ref[0] = _dot_nt(wt_ref[h:2 * h, :], xn)
        zt = _dot_nt(wt_ref[2 * h:, :], xn)[0:N_HEADS, :] + bfc_ref[...]
        lft_ref[0] = -_softplus(-zt)
    else:
        v_ref[...] = _dot(xn, w_ref[:, 2 * h:3 * h])


def _proj(x, g, w, wt, bf, bfc, *, tm, seq, prompt):
    n, d = x.shape
    h = MIX_HALF
    nj = seq // tm
    row = lambda i: (i, 0)
    fixed = lambda i: (0, 0)
    trans = lambda i: (i // nj, 0, i % nj)
    rows = lambda width: (pl.BlockSpec((tm, width), row), jax.ShapeDtypeStruct((n, width), F32))
    cols = lambda height: (pl.BlockSpec((1, height, tm), trans),
                           jax.ShapeDtypeStruct((n // seq, height, seq), F32))
    if prompt:
        outs = [rows(h), rows(h), cols(h), cols(h), rows(HR_WIDTH), rows(LANES), cols(N_HEADS)]
    else:
        outs = [rows(h), rows(h), rows(h), rows(HR_WIDTH), rows(LANES)]
    return pl.pallas_call(
        functools.partial(_proj_kernel, prompt=prompt),
        grid=(n // tm,),
        in_specs=[
            pl.BlockSpec((tm, d), row),
            pl.BlockSpec((1, d), fixed),
            pl.BlockSpec(w.shape, fixed),
            pl.BlockSpec(wt.shape, fixed),
            pl.BlockSpec((1, LANES), fixed),
            pl.BlockSpec((N_HEADS, 1), fixed),
        ],
        out_specs=[o[0] for o in outs],
        out_shape=[o[1] for o in outs],
        compiler_params=_params("parallel"),
        name="proj",
    )(x, g, w, wt, bf, bfc)


def _rwkv_chunk_kernel(h_ref, hp_ref, p0_ref, mu_ref, w0_ref, a0_ref, kk_ref, ka_ref, rk_ref,
                       wd_ref, wi_ref, wg_ref, bd_ref,
                       qc_o, y0_o, g_o, bo_o, m_o, hh_o, *, chunk, n_valid):
    j = pl.program_id(1)
    C = chunk
    C2 = 2 * C
    h = h_ref[0]
    tt = h.shape[0]
    prev = jnp.where(j == 0, p0_ref[0], hp_ref[0, 7:8, :])
    trow = lax.broadcasted_iota(jnp.int32, h.shape, 0)
    shifted = jnp.where(trow == 0, prev, pltpu.roll(h, 1, axis=0))
    hs = h + mu_ref[...] * (shifted - h)
    m = MIX_HALF
    r = hs[:, 0:m]
    k = hs[:, m:2 * m]
    v = hs[:, 2 * m:3 * m]
    o = 3 * m
    d_decay = hs[:, o:o + DECAY_PAD]
    d_iclr = hs[:, o + DECAY_PAD:o + DECAY_PAD + ICLR_PAD]
    d_gate = hs[:, o + DECAY_PAD + ICLR_PAD:]
    w_log = -_softplus(-(w0_ref[...] + _dot(jnp.tanh(d_decay).astype(BF16), wd_ref[...]))) - 0.5
    lw = -jnp.exp(w_log)
    a = _sigmoid(a0_ref[...] + _dot(d_iclr.astype(BF16), wi_ref[...]))
    g_o[0] = _dot(_sigmoid(d_gate).astype(BF16), wg_ref[...])
    bd = bd_ref[...]
    kk = k * kk_ref[...]
    kk = kk / jnp.maximum(jnp.sqrt(_dot_x01(kk * kk, bd)), 1e-12)
    k = k * (1.0 + (a - 1.0) * ka_ref[...])
    bo_o[0] = _dot_x01(r * k * rk_ref[...], bd) * v
    na = -kk
    bb = kk * a
    if n_valid < tt:
        ok = lax.broadcasted_iota(jnp.int32, (tt, m), 0) < n_valid
        zero = lambda x: jnp.where(ok, x, 0.0)
        r, k, v, lw, na, bb = zero(r), zero(k), zero(v), zero(lw), zero(na), zero(bb)

    row = lax.broadcasted_iota(jnp.int32, (C2, C2), 0)
    col = lax.broadcasted_iota(jnp.int32, (C2, C2), 1)
    rt = row & (C - 1)
    ct = col & (C - 1)
    strict = rt > ct
    incl = rt >= ct
    eye = (row == col).astype(F32)
    ti = lax.broadcasted_iota(jnp.int32, (C, C), 0)
    tj = lax.broadcasted_iota(jnp.int32, (C, C), 1)
    tril01 = (ti >= tj).astype(BF16)
    lane_lo = lax.broadcasted_iota(jnp.int32, (1, PAIR), 1) < HEAD_DIM

    def stack(x):
        return jnp.concatenate([jnp.where(lane_lo, x, 0.0), jnp.where(lane_lo, 0.0, x)], axis=0)

    prob = []
    for ci in range(tt // C):
        rows = slice(ci * C, (ci + 1) * C)
        lwc = lw[rows]
        cum = _dot_01x(tril01, lwc)
        cum_end = cum[C - 1:C, :]
        e_neg = jnp.exp(-cum)
        e_end = jnp.exp(cum_end - cum)
        at = na[rows] * jnp.exp(cum - lwc)
        rt_ = r[rows] * jnp.exp(cum)
        bt = bb[rows] * e_neg
        kt = k[rows] * e_neg
        bh = bb[rows] * e_end
        kh = k[rows] * e_end
        wc = jnp.exp(cum_end)
        for hp in range(N_PAIRS):
            sl = slice(hp * PAIR, (hp + 1) * PAIR)
            v2 = stack(v[rows, sl])
            prob.append(dict(
                ci=ci, sl=sl, hp=hp, rq=rt_[:, sl], wc=wc[:, sl],
                at=stack(at[:, sl]).astype(BF16), rt=stack(rt_[:, sl]).astype(BF16),
                bt=stack(bt[:, sl]).astype(BF16), kt=stack(kt[:, sl]).astype(BF16),
                bh=stack(bh[:, sl]).astype(BF16), kh=stack(kh[:, sl]).astype(BF16),
                v=v2.astype(BF16), vt=v2.T.astype(BF16)))

    mm = [_dot_nt(jnp.concatenate([p["at"], p["rt"]], axis=0), jnp.concatenate([p["bt"], p["kt"]], axis=0))
          for p in prob]
    n_ab = [jnp.where(strict, x[0:C2, 0:C2], 0.0) for x in mm]
    a_kr = [jnp.concatenate([jnp.where(strict, x[0:C2, C2:], 0.0),
                             jnp.where(incl, x[C2:, C2:], 0.0)], axis=0).astype(BF16) for x in mm]
    a_rb = [jnp.where(incl, x[C2:, 0:C2], 0.0).astype(BF16) for x in mm]
    kv = [_dot(x, p["v"]) for x, p in zip(a_kr, prob)]
    vtk = [_dot(p["vt"], p["kh"]) for p in prob]

    t_inv = [eye + jnp.where((rt >> 1) == (ct >> 1), x, 0.0) for x in n_ab]
    lvl = 1
    while (1 << lvl) < C:
        sel = ((rt >> (lvl + 1)) == (ct >> (lvl + 1))) & ((rt >> lvl) != (ct >> lvl))
        off = [jnp.where(sel, x, 0.0).astype(BF16) for x in n_ab]
        tb = [x.astype(BF16) for x in t_inv]
        xx = [_dot(x, y).astype(BF16) for x, y in zip(off, tb)]
        t_inv = [x + _dot(y, z) for x, y, z in zip(t_inv, tb, xx)]
        lvl += 1

    pu = [_dot(t.astype(BF16), jnp.concatenate([p["at"], x[0:C2].astype(BF16)], axis=1))
          for t, p, x in zip(t_inv, prob, kv)]
    ab = [_dot(x, y.astype(BF16)) for x, y in zip(a_rb, pu)]
    gb = [_dot(x.T.astype(BF16), p["bh"]) for x, p in zip(pu, prob)]
    for p, x_ab, x_kv, x_gb, x_vtk in zip(prob, ab, kv, gb, vtk):
        rows = slice(p["ci"] * C, (p["ci"] + 1) * C)
        qc_o[0, rows, p["sl"]] = p["rq"] + x_ab[0:C, 0:PAIR] + x_ab[C:, 0:PAIR]
        y0 = x_ab[:, PAIR:] + x_kv[C2:]
        y0_o[0, rows, p["sl"]] = y0[0:C] + y0[C:]
        m_o[0, p["ci"], p["hp"]] = (x_gb[0:PAIR] + eye * p["wc"]).astype(BF16)
        hh_o[0, p["ci"], p["hp"]] = x_gb[PAIR:] + x_vtk


def _rwkv_chunk(hr, prev0, lp, *, tt, chunk, n_valid):
    b, t, _ = hr.shape
    m = MIX_HALF
    g = tt // chunk
    tile = lambda i, j: (i, j, 0)
    fixed = lambda i, j: (0, 0)
    vec = pl.BlockSpec((1, m), fixed)
    tok = (pl.BlockSpec((1, tt, m), tile), jax.ShapeDtypeStruct((b, t, m), F32))
    mat = lambda dt: (pl.BlockSpec((1, g, N_PAIRS, PAIR, PAIR), lambda i, j: (i, j, 0, 0, 0)),
                      jax.ShapeDtypeStruct((b, t // chunk, N_PAIRS, PAIR, PAIR), dt))
    outs = [tok, tok, tok, tok, mat(BF16), mat(F32)]
    return pl.pallas_call(
        functools.partial(_rwkv_chunk_kernel, chunk=chunk, n_valid=n_valid),
        grid=(b, t // tt),
        in_specs=[
            pl.BlockSpec((1, tt, HR_WIDTH), tile),
            pl.BlockSpec((1, 8, HR_WIDTH), lambda i, j: (i, jnp.maximum(j * (tt // 8) - 1, 0), 0)),
            pl.BlockSpec((1, 1, HR_WIDTH), lambda i, j: (i, 0, 0)),
            pl.BlockSpec((1, HR_WIDTH), fixed),
            vec, vec, vec, vec, vec,
            pl.BlockSpec((DECAY_PAD, m), fixed),
            pl.BlockSpec((ICLR_PAD, m), fixed),
            pl.BlockSpec((GATE_PAD, m), fixed),
            pl.BlockSpec((m, m), fixed),
        ],
        out_specs=[o[0] for o in outs],
        out_shape=[o[1] for o in outs],
        compiler_params=_params("parallel", "arbitrary"),
        name="rwkv_chunk",
    )(hr, hr, prev0, lp["mu"], lp["w0"], lp["a0"], lp["k_k"], lp["k_a"], lp["r_k"],
      lp["w_decay"], lp["w_iclr"], lp["w_gate"], lp["bd512"])


def _rwkv_state_kernel(qc_ref, y0_ref, g_ref, bo_ref, m_ref, hh_ref, s0_ref, lng_ref, lnb_ref, bd_ref,
                       y_ref, sf_ref, s_ref, *, chunk, nsteps):
    j = pl.program_id(1)
    C = chunk
    nb, tt, _ = qc_ref.shape
    bd = bd_ref[...]

    @pl.when(j == 0)
    def _():
        s_ref[...] = s0_ref[...]

    cells = [(b, hp) for b in range(nb) for hp in range(N_PAIRS)]
    state = [s_ref[b, hp] for b, hp in cells]
    for ci in range(tt // C):
        rows = slice(ci * C, (ci + 1) * C)
        sb = [s.astype(BF16) for s in state]
        ys = [_dot_nt(qc_ref[b, rows, hp * PAIR:(hp + 1) * PAIR].astype(BF16), s)
              + y0_ref[b, rows, hp * PAIR:(hp + 1) * PAIR] for (b, hp), s in zip(cells, sb)]
        state = [_dot(s, m_ref[b, ci, hp]) + hh_ref[b, ci, hp] for (b, hp), s in zip(cells, sb)]
        means = [_dot_x01(y, bd) * (1.0 / HEAD_DIM) for y in ys]
        yc = [y - mu for y, mu in zip(ys, means)]
        var = [_dot_x01(y * y, bd) * (1.0 / HEAD_DIM) for y in yc]
        for (b, hp), y, vr in zip(cells, yc, var):
            sl = slice(hp * PAIR, (hp + 1) * PAIR)
            yn = y * lax.rsqrt(vr + GN_EPS) * lng_ref[:, sl] + lnb_ref[:, sl]
            y_ref[b, rows, sl] = (yn + bo_ref[b, rows, sl]) * g_ref[b, rows, sl]
    for (b, hp), s in zip(cells, state):
        s_ref[b, hp] = s

    @pl.when(j == nsteps - 1)
    def _():
        sf_ref[...] = s_ref[...]


def _rwkv_state(qc, y0, g, bonus, mm, hh, s0, lp, *, nb, tt, chunk):
    bsz, t, m = qc.shape
    gch = tt // chunk
    nsteps = t // tt
    tile = pl.BlockSpec((nb, tt, m), lambda i, j: (i, j, 0))
    mat = pl.BlockSpec((nb, gch, N_PAIRS, PAIR, PAIR), lambda i, j: (i, j, 0, 0, 0))
    st = pl.BlockSpec((nb, N_PAIRS, PAIR, PAIR), lambda i, j: (i, 0, 0, 0))
    vec = pl.BlockSpec((1, m), lambda i, j: (0, 0))
    return pl.pallas_call(
        functools.partial(_rwkv_state_kernel, chunk=chunk, nsteps=nsteps),
        grid=(bsz // nb, nsteps),
        in_specs=[tile, tile, tile, tile, mat, mat, st, vec, vec,
                  pl.BlockSpec((PAIR, PAIR), lambda i, j: (0, 0))],
        out_specs=[tile, st],
        out_shape=[jax.ShapeDtypeStruct((bsz, t, m), F32),
                   jax.ShapeDtypeStruct((bsz, N_PAIRS, PAIR, PAIR), F32)],
        scratch_shapes=[pltpu.VMEM((nb, N_PAIRS, PAIR, PAIR), F32)],
        compiler_params=_params("parallel", "arbitrary"),
        name="rwkv_state",
    )(qc, y0, g, bonus, mm, hh, s0, lp["lnx_g"], lp["lnx_b"], lp["bd128"])


def _cumsum_kernel(lf_ref, lft_ref, c_ref, ct_ref, *, blk, nblk):
    ti = lax.broadcasted_iota(jnp.int32, (blk, blk), 0)
    tj = lax.broadcasted_iota(jnp.int32, (blk, blk), 1)
    tril01 = (ti >= tj).astype(BF16)
    triu01 = (ti <= tj).astype(BF16)
    carry = jnp.zeros((1, LANES), F32)
    carry_t = jnp.zeros((N_HEADS, 1), F32)
    for i in range(nblk):
        sl = slice(i * blk, (i + 1) * blk)
        c = _dot_01x(tril01, lf_ref[0, sl, :]) + carry
        c_ref[0, sl, :] = c
        carry = c[blk - 1:blk, :]
        ct = _dot_x01(lft_ref[0, :, sl], triu01) + carry_t
        ct_ref[0, :, sl] = ct
        carry_t = ct[:, blk - 1:blk]


def _cumsum(lf, lft, *, blk):
    b, s, w = lf.shape
    spec = pl.BlockSpec((1, s, w), lambda i: (i, 0, 0))
    spec_t = pl.BlockSpec((1, N_HEADS, s), lambda i: (i, 0, 0))
    return pl.pallas_call(
        functools.partial(_cumsum_kernel, blk=blk, nblk=s // blk),
        grid=(b,),
        in_specs=[spec, spec_t],
        out_specs=[spec, spec_t],
        out_shape=[jax.ShapeDtypeStruct((b, s, w), F32), jax.ShapeDtypeStruct((b, N_HEADS, s), F32)],
        compiler_params=_params("parallel"),
        name="logf_cumsum",
    )(lf, lft)


def _fox_prompt_kernel(q_ref, k_ref, vt_ref, cq_ref, ck_ref, gain_ref, o_ref, qm_s, m_s, l_s, acc_s, *, blk):
    qi = pl.program_id(1)
    lane = lax.broadcasted_iota(jnp.int32, (1, PAIR), 1)
    scale = HEAD_DIM ** -0.5
    heads = range(N_HEADS)

    for hp in range(N_PAIRS):
        qp = q_ref[0, :, hp * PAIR:(hp + 1) * PAIR] * scale
        qm_s[2 * hp] = jnp.where(lane < HEAD_DIM, qp, 0.0).astype(BF16)
        qm_s[2 * hp + 1] = jnp.where(lane >= HEAD_DIM, qp, 0.0).astype(BF16)
    m_s[...] = jnp.full(m_s.shape, NEG_INF, F32)
    l_s[...] = jnp.zeros_like(l_s)
    acc_s[...] = jnp.zeros_like(acc_s)

    def block(j, diagonal):
        ks = pl.ds(pl.multiple_of(j * blk, blk), blk)
        kb = [k_ref[0, ks, hp * PAIR:(hp + 1) * PAIR].astype(BF16) for hp in range(N_PAIRS)]
        vtb = [vt_ref[0, h * HEAD_DIM:(h + 1) * HEAD_DIM, ks].astype(BF16) for h in heads]
        st = [_dot_nt(kb[h // 2], qm_s[h]) - ck_ref[0, ks, h:h + 1] for h in heads]
        if diagonal:
            ki = lax.broadcasted_iota(jnp.int32, (blk, blk), 0)
            qj = lax.broadcasted_iota(jnp.int32, (blk, blk), 1)
            st = [jnp.where(ki <= qj, x, NEG_INF) for x in st]
        cq = [cq_ref[0, h:h + 1, :] for h in heads]
        m_old = [m_s[h:h + 1, :] for h in heads]
        m_new = [jnp.maximum(mo, jnp.max(x, axis=0, keepdims=True) + c) for mo, x, c in zip(m_old, st, cq)]
        p = [jnp.exp(x + (c - mn)) for x, c, mn in zip(st, cq, m_new)]
        pv = [_dot(vb, x.astype(BF16)) for vb, x in zip(vtb, p)]
        for h in heads:
            alpha = jnp.exp(m_old[h] - m_new[h])
            m_s[h:h + 1, :] = m_new[h]
            l_s[h:h + 1, :] = alpha * l_s[h:h + 1, :] + jnp.sum(p[h], axis=0, keepdims=True)
            rows = slice(h * HEAD_DIM, (h + 1) * HEAD_DIM)
            acc_s[rows, :] = alpha * acc_s[rows, :] + pv[h]

    def body(j, carry):
        block(j, False)
        return carry

    lax.fori_loop(0, qi, body, 0)
    block(qi, True)
    for hp in range(N_PAIRS):
        halves = []
        for h in (2 * hp, 2 * hp + 1):
            o = acc_s[h * HEAD_DIM:(h + 1) * HEAD_DIM, :] / l_s[h:h + 1, :]
            ms = jnp.mean(o * o, axis=0, keepdims=True)
            halves.append(o * lax.rsqrt(ms + RMS_EPS))
        sl = slice(hp * PAIR, (hp + 1) * PAIR)
        o_ref[0, :, sl] = jnp.concatenate(halves, axis=0).T * gain_ref[:, sl]


def _fox_prompt(q, k, vt, cq_t, ck, gain, *, blk):
    b, s, m = q.shape
    nb = s // blk
    return pl.pallas_call(
        functools.partial(_fox_prompt_kernel, blk=blk),
        grid=(b, nb),
        in_specs=[
            pl.BlockSpec((1, blk, m), lambda i, j: (i, j, 0)),
            pl.BlockSpec((1, s, m), lambda i, j: (i, 0, 0)),
            pl.BlockSpec((1, m, s), lambda i, j: (i, 0, 0)),
            pl.BlockSpec((1, N_HEADS, blk), lambda i, j: (i, 0, j)),
            pl.BlockSpec((1, s, LANES), lambda i, j: (i, 0, 0)),
            pl.BlockSpec((1, m), lambda i, j: (0, 0)),
        ],
        out_specs=pl.BlockSpec((1, blk, m), lambda i, j: (i, j, 0)),
        out_shape=jax.ShapeDtypeStruct((b, s, m), F32),
        scratch_shapes=[
            pltpu.VMEM((N_HEADS, blk, PAIR), BF16),
            pltpu.VMEM((N_HEADS, blk), F32),
            pltpu.VMEM((N_HEADS, blk), F32),
            pltpu.VMEM((m, blk), F32),
        ],
        compiler_params=_params("parallel", "arbitrary"),
        name="fox_prompt",
    )(q, k, vt, cq_t, ck, gain)


def _fox_decode_kernel(pt_ref, q_ref, kn_ref, vn_ref, lfn_ref, lfnt_ref, gain_ref, hm_ref, bd_ref, *rest,
                       npg, nsteps, tnew):
    k_pages = rest[0:npg]
    v_pages = rest[npg:2 * npg]
    lf_pages = rest[2 * npg:3 * npg]
    o_ref, m_s, l_s, acc_s, carry_s, qbd_s, cn_s = rest[3 * npg:]
    g = pl.program_id(1)
    nrow = tnew * N_HEADS
    page = k_pages[0].shape[3]
    li = lax.broadcasted_iota(jnp.int32, (LANES, LANES), 0)
    lj = lax.broadcasted_iota(jnp.int32, (LANES, LANES), 1)
    hm = hm_ref[...]

    def tile_rows(x):
        return jnp.concatenate([x] * tnew, axis=0)

    @pl.when(g == 0)
    def _():
        q = q_ref[0] * (HEAD_DIM ** -0.5)
        qbd = jnp.concatenate([jnp.broadcast_to(q[t:t + 1, :], hm.shape) * hm for t in range(tnew)], axis=0)
        qbd_s[...] = qbd.astype(BF16)
        cn = lfn_ref[0]
        trow = lax.broadcasted_iota(jnp.int32, cn.shape, 0)
        sh = 1
        while sh < tnew:
            cn = cn + jnp.where(trow >= sh, pltpu.roll(cn, sh, axis=0), 0.0)
            sh *= 2
        hsel = (lax.broadcasted_iota(jnp.int32, (N_HEADS, LANES), 0)
                == lax.broadcasted_iota(jnp.int32, (N_HEADS, LANES), 1)).astype(F32)
        cn_rows = jnp.concatenate(
            [jnp.sum(jnp.broadcast_to(cn[t:t + 1, :], hsel.shape) * hsel, axis=1, keepdims=True)
             for t in range(tnew)], axis=0)
        cn_s[...] = cn_rows
        cnt = _dot_x01(lfnt_ref[0], (li <= lj).astype(BF16))
        pad = jnp.zeros((page - tnew, MIX_HALF), F32)
        kn = jnp.concatenate([kn_ref[0], pad], axis=0).astype(BF16)
        vn = jnp.concatenate([vn_ref[0], pad], axis=0).astype(BF16)
        s = _dot_nt(qbd_s[...], kn) + (cn_rows - tile_rows(cnt))
        rtok = lax.broadcasted_iota(jnp.int32, (nrow, LANES), 0) >> 3
        ktok = lax.broadcasted_iota(jnp.int32, (nrow, LANES), 1)
        s = jnp.where(ktok <= rtok, s, NEG_INF)
        m = jnp.max(s, axis=1, keepdims=True)
        p = jnp.exp(s - m)
        m_s[...] = m
        l_s[...] = jnp.sum(p, axis=1, keepdims=True)
        acc_s[...] = _dot(p.astype(BF16), vn)
        carry_s[...] = jnp.zeros_like(carry_s)

    triu_strict = (li > lj).astype(BF16)
    run = carry_s[...]
    qbd = qbd_s[...]
    cn_rows = cn_s[...]
    scores = [None] * npg
    for i in range(npg - 1, -1, -1):
        lf = lf_pages[i][0, 0]
        bias = _dot_x01(lf, triu_strict) + run
        run = run + jnp.sum(lf, axis=1, keepdims=True)
        scores[i] = _dot(qbd, k_pages[i][0, 0].astype(BF16)) + (tile_rows(bias) + cn_rows)
    carry_s[...] = run
    s = jnp.concatenate(scores, axis=1)
    m_old = m_s[...]
    m_new = jnp.maximum(m_old, jnp.max(s, axis=1, keepdims=True))
    alpha = jnp.exp(m_old - m_new)
    p = jnp.exp(s - m_new)
    m_s[...] = m_new
    l_s[...] = alpha * l_s[...] + jnp.sum(p, axis=1, keepdims=True)
    pv = _dot_nt(p[:, 0:page].astype(BF16), v_pages[0][0, 0].astype(BF16))
    for i in range(1, npg):
        pv = pv + _dot_nt(p[:, i * page:(i + 1) * page].astype(BF16), v_pages[i][0, 0].astype(BF16))
    acc_s[...] = alpha * acc_s[...] + pv

    @pl.when(g == nsteps - 1)
    def _():
        o_rows = acc_s[...] / l_s[...]
        o = jnp.concatenate(
            [jnp.sum(o_rows[t * N_HEADS:(t + 1) * N_HEADS, :] * hm, axis=0, keepdims=True) for t in range(tnew)],
            axis=0)
        ms = _dot_x01(o * o, bd_ref[...]) * (1.0 / HEAD_DIM)
        o_ref[0] = o * lax.rsqrt(ms + RMS_EPS) * gain_ref[...]


def _fox_decode(layer, page_table, q, kn, vn, lfn, lfn_t, cache_k, cache_v, cache_lft, gain, hm, bd, *, npg):
    b, tnew, m = q.shape
    n_pages = page_table.shape[1]
    page = cache_k.shape[3]
    nsteps = n_pages // npg
    nrow = tnew * N_HEADS

    def tok(i, g, pt):
        return (i, 0, 0)

    def fixed(i, g, pt):
        return (0, 0)

    def page_map(slot):
        return lambda i, g, pt: (layer, pt[i, (nsteps - 1 - g) * npg + slot], 0, 0)

    kv_specs = [pl.BlockSpec((1, 1, m, page), page_map(s)) for s in range(npg)]
    lf_specs = [pl.BlockSpec((1, 1, N_HEADS, page), page_map(s)) for s in range(npg)]
    grid_spec = pltpu.PrefetchScalarGridSpec(
        num_scalar_prefetch=1,
        grid=(b, nsteps),
        in_specs=[
            pl.BlockSpec((1, tnew, m), tok),
            pl.BlockSpec((1, tnew, m), tok),
            pl.BlockSpec((1, tnew, m), tok),
            pl.BlockSpec((1, tnew, LANES), tok),
            pl.BlockSpec((1, N_HEADS, LANES), tok),
            pl.BlockSpec((1, m), fixed),
            pl.BlockSpec((N_HEADS, m), fixed),
            pl.BlockSpec((m, m), fixed),
        ] + kv_specs + kv_specs + lf_specs,
        out_specs=pl.BlockSpec((1, tnew, m), tok),
        scratch_shapes=[
            pltpu.VMEM((nrow, 1), F32),
            pltpu.VMEM((nrow, 1), F32),
            pltpu.VMEM((nrow, m), F32),
            pltpu.VMEM((N_HEADS, 1), F32),
            pltpu.VMEM((nrow, m), BF16),
            pltpu.VMEM((nrow, 1), F32),
        ],
    )
    return pl.pallas_call(
        functools.partial(_fox_decode_kernel, npg=npg, nsteps=nsteps, tnew=tnew),
        grid_spec=grid_spec,
        out_shape=jax.ShapeDtypeStruct((b, tnew, m), F32),
        compiler_params=_params("parallel", "arbitrary"),
        name="fox_decode",
    )(page_table, q, kn, vn, lfn, lfn_t, gain, hm, bd,
      *([cache_k] * npg), *([cache_v] * npg), *([cache_lft] * npg))


def _out_proj_kernel(x_ref, yr_ref, yf_ref, wr_ref, wf_ref, o_ref):
    o_ref[...] = (x_ref[...] + _dot(yr_ref[...].astype(BF16), wr_ref[...])
                  + _dot(yf_ref[...].astype(BF16), wf_ref[...]))


def _out_proj(x, yr, yf, wr, wf, *, tm):
    n, d = x.shape
    m = yr.shape[1]
    row = lambda i: (i, 0)
    fixed = lambda i: (0, 0)
    return pl.pallas_call(
        _out_proj_kernel,
        grid=(n // tm,),
        in_specs=[pl.BlockSpec((tm, d), row), pl.BlockSpec((tm, m), row), pl.BlockSpec((tm, m), row),
                  pl.BlockSpec((m, d), fixed), pl.BlockSpec((m, d), fixed)],
        out_specs=pl.BlockSpec((tm, d), row),
        out_shape=jax.ShapeDtypeStruct((n, d), F32),
        compiler_params=_params("parallel"),
        name="out_proj",
    )(x, yr, yf, wr, wf)


def _ple_kernel(x_ref, p_ref, g_ref, wg_ref, wu_ref, fg_ref, o_ref, *, final):
    x = x_ref[...]
    gate = _sigmoid(_dot(_rms(x, g_ref[...]).astype(BF16), wg_ref[...]))
    y = x + gate * _dot(p_ref[...].astype(BF16), wu_ref[...])
    o_ref[...] = _rms(y, fg_ref[...]) if final else y


def _ple(x, p, g, wg, wu, fg, *, tm, final):
    n, d = x.shape
    pd = p.shape[1]
    row = lambda i: (i, 0)
    fixed = lambda i: (0, 0)
    return pl.pallas_call(
        functools.partial(_ple_kernel, final=final),
        grid=(n // tm,),
        in_specs=[pl.BlockSpec((tm, d), row), pl.BlockSpec((tm, pd), row), pl.BlockSpec((1, d), fixed),
                  pl.BlockSpec((d, d), fixed), pl.BlockSpec((pd, d), fixed), pl.BlockSpec((1, d), fixed)],
        out_specs=pl.BlockSpec((tm, d), row),
        out_shape=jax.ShapeDtypeStruct((n, d), F32),
        compiler_params=_params("parallel"),
        name="ple",
    )(x, p, g, wg, wu, fg)


def _pad_cols(w, width):
    return jnp.pad(w, [(0, 0)] * (w.ndim - 1) + [(0, width - w.shape[-1])])


def _pad_rows(w, height):
    return jnp.pad(w, [(0, 0)] * (w.ndim - 2) + [(0, height - w.shape[-2]), (0, 0)])


def _pack_hr(h):
    m = MIX_HALF
    o = 3 * m
    return jnp.concatenate([
        h[..., :o],
        _pad_cols(h[..., o:o + DECAY_RANK], DECAY_PAD),
        _pad_cols(h[..., o + DECAY_RANK:o + DECAY_RANK + ICLR_RANK], ICLR_PAD),
        _pad_cols(h[..., o + DECAY_RANK + ICLR_RANK:], GATE_PAD)], axis=-1)


def _unpack_hr(h):
    m = MIX_HALF
    o = 3 * m
    return jnp.concatenate([
        h[..., :o],
        h[..., o:o + DECAY_RANK],
        h[..., o + DECAY_PAD:o + DECAY_PAD + ICLR_RANK],
        h[..., o + DECAY_PAD + ICLR_PAD:o + DECAY_PAD + ICLR_PAD + GATE_RANK]], axis=-1)


def _pair_state(s):
    b = s.shape[0]
    s = s.reshape(b, N_PAIRS, 2, HEAD_DIM, HEAD_DIM)
    z = jnp.zeros_like(s[:, :, 0])
    top = jnp.concatenate([s[:, :, 0], z], axis=-1)
    bot = jnp.concatenate([z, s[:, :, 1]], axis=-1)
    return jnp.concatenate([top, bot], axis=-2)


def _unpair_state(s):
    b = s.shape[0]
    d = HEAD_DIM
    return jnp.stack([s[:, :, :d, :d], s[:, :, d:, d:]], axis=2).reshape(b, N_HEADS, d, d)


def _block_diag01(n):
    i = jnp.arange(n) // HEAD_DIM
    return (i[:, None] == i[None, :]).astype(BF16)


def _rwkv_mix(hr3, prev0, s0, lp, *, n_valid):
    bsz, t, _ = hr3.shape
    chunk = REC_CHUNK
    qc, y0, g, bonus, mm, hh = _rwkv_chunk(hr3, prev0, lp, tt=min(t, REC_TILE), chunk=chunk, n_valid=n_valid)
    return _rwkv_state(qc, y0, g, bonus, mm, hh, s0, lp, nb=min(bsz, STATE_BATCH), tt=min(t, STATE_TILE),
                       chunk=chunk)


def kernel(x_prompt, x_sample, cache_k, cache_v, cache_logf, state_wkv, state_shift, page_table, p_prompt, p_sample, ffn1_norm, ffn1_w_gate, ffn1_w_up, ffn1_w_down, mix_norm, w_in, rwkv_mu, rwkv_w0, rwkv_w_decay, rwkv_a0, rwkv_w_iclr, rwkv_w_gate, rwkv_k_k, rwkv_k_a, rwkv_r_k, rwkv_lnx_g, rwkv_lnx_b, fox_b_f, fox_out_norm, w_out, ffn2_norm, ffn2_w_gate, ffn2_w_up, ffn2_w_down, ple_norm, ple_w_gate, ple_w_up, final_norm):
    depth = w_in.shape[0]
    bp, sp, d = x_prompt.shape
    bs, ts, _ = x_sample.shape
    m = MIX_HALF
    n_pool, page = cache_k.shape[1], cache_k.shape[2]
    npr, nsm = bp * sp, bs * ts
    pd = p_prompt.shape[-1]

    fox_cols = w_in[:, :, RWKV_PROJ:]
    w_pack = jnp.concatenate([
        fox_cols[:, :, :3 * m],
        _pack_hr(w_in[:, :, :RWKV_PROJ]),
        _pad_cols(fox_cols[:, :, 3 * m:], LANES)], axis=-1).astype(BF16)
    w_t = jnp.swapaxes(_pad_cols(fox_cols[:, :, m:], 2 * m + LANES), 1, 2).astype(BF16)
    b_f = _pad_cols(fox_b_f, LANES)[:, None, :]
    b_fc = fox_b_f[:, :, None]
    bd512 = _block_diag01(m)
    bd128 = _block_diag01(PAIR)
    hm = (jnp.arange(m)[None, :] // HEAD_DIM == jnp.arange(N_HEADS)[:, None]).astype(F32)
    bf = lambda w: w.astype(BF16)
    f1g, f1u, f1d = bf(ffn1_w_gate), bf(ffn1_w_up), bf(ffn1_w_down)
    f2g, f2u, f2d = bf(ffn2_w_gate), bf(ffn2_w_up), bf(ffn2_w_down)
    wo = bf(w_out)
    pg, pu = bf(ple_w_gate), bf(ple_w_up)
    wdec = bf(_pad_rows(rwkv_w_decay, DECAY_PAD))
    wicl = bf(_pad_rows(rwkv_w_iclr, ICLR_PAD))
    wgat = bf(_pad_rows(rwkv_w_gate, GATE_PAD))
    mu = _pack_hr(rwkv_mu)
    cache_kt = jnp.transpose(cache_k, (0, 1, 3, 4, 2)).reshape(depth, n_pool, m, page)
    cache_vt = jnp.transpose(cache_v, (0, 1, 3, 4, 2)).reshape(depth, n_pool, m, page)
    cache_lft = jnp.swapaxes(cache_logf, 2, 3)
    fg = final_norm[None, :]

    tm_p = 1024 if npr % 1024 == 0 else ATT_BLOCK
    tf = 256
    blk = ATT_BLOCK
    chunk = REC_CHUNK
    npg = min(DEC_PAGES_PER_STEP, page_table.shape[1])

    xp = x_prompt.reshape(npr, d)
    xs = x_sample.reshape(nsm, d)
    outs = [[] for _ in range(10)]
    for l in range(depth):
        lp = dict(mu=mu[l][None], w0=rwkv_w0[l][None], a0=rwkv_a0[l][None], k_k=rwkv_k_k[l][None],
                  k_a=rwkv_k_a[l][None], r_k=rwkv_r_k[l].reshape(1, m), w_decay=wdec[l], w_iclr=wicl[l],
                  w_gate=wgat[l], bd512=bd512, bd128=bd128, lnx_g=rwkv_lnx_g[l][None],
                  lnx_b=rwkv_lnx_b[l][None])
        gain = fox_out_norm[l].reshape(1, m)
        last = l == depth - 1

        xp = _ffn(xp, ffn1_norm[l][None], f1g[l], f1u[l], f1d[l], tm=tm_p, tf=tf)
        q, k, kt, vt, hr, lf, lft = _proj(xp, mix_norm[l][None], w_pack[l], w_t[l], b_f[l], b_fc[l],
                                          tm=blk, seq=sp, prompt=True)
        hr3 = hr.reshape(bp, sp, HR_WIDTH)
        y_r, s_fin = _rwkv_mix(hr3, jnp.zeros((bp, 1, HR_WIDTH), F32),
                               jnp.zeros((bp, N_PAIRS, PAIR, PAIR), F32), lp, n_valid=sp)
        c, c_t = _cumsum(lf.reshape(bp, sp, LANES), lft, blk=blk)
        y_f = _fox_prompt(q.reshape(bp, sp, m), k.reshape(bp, sp, m), vt, c_t, c, gain, blk=blk)
        xp = _out_proj(xp, y_r.reshape(npr, m), y_f.reshape(npr, m), wo[l, :m], wo[l, m:], tm=tm_p)
        xp = _ffn(xp, ffn2_norm[l][None], f2g[l], f2u[l], f2d[l], tm=tm_p, tf=tf)
        xp = _ple(xp, p_prompt[l].reshape(npr, pd), ple_norm[l][None], pg[l], pu[l], fg, tm=tm_p, final=last)
        outs[0].append(kt)
        outs[1].append(vt)
        outs[2].append(lft)
        outs[3].append(_unpair_state(s_fin))
        outs[4].append(_unpack_hr(hr3[:, -1, :]))

        xs = _ffn(xs, ffn1_norm[l][None], f1g[l], f1u[l], f1d[l], tm=nsm, tf=tf)
        q, k, v, hr, lf = _proj(xs, mix_norm[l][None], w_pack[l], w_t[l], b_f[l], b_fc[l],
                                tm=nsm, seq=nsm, prompt=False)
        hr3 = hr.reshape(bs, ts, HR_WIDTH)
        hr_pad = jnp.pad(hr3, ((0, 0), (0, chunk - ts), (0, 0)))
        y_r, s_fin = _rwkv_mix(hr_pad, _pack_hr(state_shift[l])[:, None, :], _pair_state(state_wkv[l]), lp,
                               n_valid=ts)
        lf3 = lf.reshape(bs, ts, LANES)
        lf_t = _pad_cols(jnp.swapaxes(lf3[:, :, :N_HEADS], 1, 2), LANES)
        y_f = _fox_decode(l, page_table, q.reshape(bs, ts, m), k.reshape(bs, ts, m), v.reshape(bs, ts, m),
                          lf3, lf_t, cache_kt, cache_vt, cache_lft, gain, hm, bd512, npg=npg)
        xs = _out_proj(xs, y_r[:, :ts].reshape(nsm, m), y_f.reshape(nsm, m), wo[l, :m], wo[l, m:], tm=nsm)
        xs = _ffn(xs, ffn2_norm[l][None], f2g[l], f2u[l], f2d[l], tm=nsm, tf=tf)
        xs = _ple(xs, p_sample[l].reshape(nsm, pd), ple_norm[l][None], pg[l], pu[l], fg, tm=nsm, final=last)
        outs[5].append(k.reshape(bs, ts, N_HEADS, HEAD_DIM))
        outs[6].append(v.reshape(bs, ts, N_HEADS, HEAD_DIM))
        outs[7].append(lf3[:, :, :N_HEADS])
        outs[8].append(_unpair_state(s_fin))
        outs[9].append(_unpack_hr(hr3[:, -1, :]))

    st = [jnp.stack(o) for o in outs]
    for i in (0, 1):
        st[i] = jnp.transpose(st[i].reshape(depth, bp, N_HEADS, HEAD_DIM, sp), (0, 1, 4, 2, 3))
    st[2] = jnp.swapaxes(st[2], 2, 3)
    return (xp.reshape(bp, sp, d), xs.reshape(bs, ts, d), *st)
```

```python
import functools

import jax
import jax.numpy as jnp
from jax import lax
from jax.experimental import pallas as pl
from jax.experimental.pallas import tpu as pltpu

F32 = jnp.float32
BF16 = jnp.bfloat16

HEAD_DIM = 64
N_HEADS = 8
MIX_HALF = N_HEADS * HEAD_DIM
PAIR = 2 * HEAD_DIM
N_PAIRS = N_HEADS // 2
DECAY_RANK = 64
ICLR_RANK = 64
GATE_RANK = 160
RWKV_PROJ = 3 * MIX_HALF + DECAY_RANK + ICLR_RANK + GATE_RANK
LANES = 128
DECAY_PAD = 128
ICLR_PAD = 128
GATE_PAD = 256
HR_WIDTH = 3 * MIX_HALF + DECAY_PAD + ICLR_PAD + GATE_PAD
RMS_EPS = 1e-6
GN_EPS = 64e-5
NEG_INF = -1e30
LOG2E = 1.4426950408889634
VMEM_LIMIT = 56 * 1024 * 1024

REC_CHUNK = 64
REC_TILE = 256
STATE_TILE = 128
STATE_BATCH = 8
ATT_BLOCK = 256
FFN_TILE = 2048
DEC_PAGES_PER_STEP = 16


def _params(*sem):
    return pltpu.CompilerParams(dimension_semantics=sem, vmem_limit_bytes=VMEM_LIMIT)


def _dot(a, b):
    return jnp.dot(a, b, preferred_element_type=F32)


def _dot_nt(a, b):
    return lax.dot_general(a, b, (((1,), (1,)), ((), ())), preferred_element_type=F32)


def _split3(x):
    x1 = x.astype(BF16)
    r1 = x - x1.astype(F32)
    x2 = r1.astype(BF16)
    r2 = r1 - x2.astype(F32)
    return x1, x2, r2.astype(BF16)


def _dot_x01(x, m01):
    x1, x2, x3 = _split3(x)
    return _dot(x1, m01) + _dot(x2, m01) + _dot(x3, m01)


def _group_sum(x, m01):
    x1 = x.astype(BF16)
    return _dot(x1, m01) + _dot((x - x1.astype(F32)).astype(BF16), m01)


def _dot_01x(m01, x):
    x1, x2, x3 = _split3(x)
    return _dot(m01, x1) + _dot(m01, x2) + _dot(m01, x3)


def _rms(x, g):
    ms = jnp.mean(x * x, axis=-1, keepdims=True)
    return x * lax.rsqrt(ms + RMS_EPS) * g


def _sigmoid(x):
    return 1.0 / (1.0 + jnp.exp(-x))


def _softplus(z):
    return jnp.maximum(z, 0.0) + jnp.log(1.0 + jnp.exp(-jnp.abs(z)))


def _ffn_kernel(x_ref, g_ref, wg_ref, wu_ref, wd_ref, o_ref, xn_ref, *, nf):
    j = pl.program_id(1)

    @pl.when(j == 0)
    def _():
        xn_ref[...] = _rms(x_ref[...], g_ref[...]).astype(BF16)
        o_ref[...] = jnp.zeros_like(o_ref)

    xn = xn_ref[...]
    gate = _dot(xn, wg_ref[...])
    up = _dot(xn, wu_ref[...])
    h = (gate * _sigmoid(gate) * up).astype(BF16)
    o_ref[...] += _dot(h, wd_ref[...])

    @pl.when(j == nf - 1)
    def _():
        o_ref[...] = x_ref[...] + 0.5 * o_ref[...]


def _ffn(x, g, wg, wu, wd, *, tm, tf):
    n, d = x.shape
    f = wg.shape[1]
    nf = f // tf
    return pl.pallas_call(
        functools.partial(_ffn_kernel, nf=nf),
        grid=(n // tm, nf),
        in_specs=[
            pl.BlockSpec((tm, d), lambda i, j: (i, 0)),
            pl.BlockSpec((1, d), lambda i, j: (0, 0)),
            pl.BlockSpec((d, tf), lambda i, j: (0, j)),
            pl.BlockSpec((d, tf), lambda i, j: (0, j)),
            pl.BlockSpec((tf, d), lambda i, j: (j, 0)),
        ],
        out_specs=pl.BlockSpec((tm, d), lambda i, j: (i, 0)),
        out_shape=jax.ShapeDtypeStruct((n, d), F32),
        scratch_shapes=[pltpu.VMEM((tm, d), BF16)],
        compiler_params=_params("parallel", "arbitrary"),
        name="ffn",
    )(x, g, wg, wu, wd)


def _proj_kernel(x_ref, g_ref, w_ref, wt_ref, bf_ref, bfc_ref, *refs, prompt, n_alias):
    out_refs = refs[n_alias:]
    xn = _rms(x_ref[...], g_ref[...]).astype(BF16)
    h = MIX_HALF
    if prompt:
        q_ref, k_ref, kt_ref, vt_ref, hr_ref, lf_ref, lft_ref = out_refs
    else:
        q_ref, k_ref, v_ref, hr_ref, lf_ref = out_refs
    q_ref[...] = _dot(xn, w_ref[:, 0:h])
    k_ref[...] = _dot(xn, w_ref[:, h:2 * h])
    hr_ref[...] = _dot(xn, w_ref[:, 3 * h:3 * h + HR_WIDTH])
    lf_ref[...] = -_softplus(-(_dot(xn, w_ref[:, 3 * h + HR_WIDTH:]) + bf_ref[...]))
    if prompt:
        kt_ref[0, 0] = _dot_nt(wt_ref[0:h, :], xn)
        vt_ref[0, 0] = _dot_nt(wt_ref[h:2 * h, :], xn)
        zt = _dot_nt(wt_ref[2 * h:, :], xn)[0:N_HEADS, :] + bfc_ref[...]
        lft_ref[0, 0] = -_softplus(-zt)
    else:
        v_ref[...] = _dot(xn, w_ref[:, 2 * h:3 * h])


def _proj(x, g, w, wt, bf, bfc, *, tm, seq, prompt, layer=0, depth=1, stacked=None):
    n, d = x.shape
    h = MIX_HALF
    nj = seq // tm
    row = lambda i: (i, 0)
    fixed = lambda i: (0, 0)
    trans = lambda i: (layer, i // nj, 0, i % nj)
    rows = lambda width: (pl.BlockSpec((tm, width), row), jax.ShapeDtypeStruct((n, width), F32))
    cols = lambda height: (pl.BlockSpec((1, 1, height, tm), trans),
                           jax.ShapeDtypeStruct((depth, n // seq, height, seq), F32))
    if prompt:
        outs = [rows(h), rows(h), cols(h), cols(h), rows(HR_WIDTH), rows(LANES), cols(N_HEADS)]
    else:
        outs = [rows(h), rows(h), rows(h), rows(HR_WIDTH), rows(LANES)]
    stacked = tuple(stacked or ())
    n_in = 6
    aliases = {n_in + i: o for i, o in enumerate((2, 3, 6)[:len(stacked)])}
    return pl.pallas_call(
        functools.partial(_proj_kernel, prompt=prompt, n_alias=len(stacked)),
        grid=(n // tm,),
        in_specs=[
            pl.BlockSpec((tm, d), row),
            pl.BlockSpec((1, d), fixed),
            pl.BlockSpec(w.shape, fixed),
            pl.BlockSpec(wt.shape, fixed),
            pl.BlockSpec((1, LANES), fixed),
            pl.BlockSpec((N_HEADS, 1), fixed),
        ] + [pl.BlockSpec(memory_space=pl.ANY)] * len(stacked),
        out_specs=[o[0] for o in outs],
        out_shape=[o[1] for o in outs],
        input_output_aliases=aliases,
        compiler_params=_params("parallel"),
        name="proj",
    )(x, g, w, wt, bf, bfc, *stacked)


def _rwkv_chunk_kernel(h_ref, hp_ref, p0_ref, mu_ref, w0_ref, a0_ref, kk_ref, ka_ref, rk_ref,
                       wd_ref, wi_ref, wg_ref, bd_ref,
                       qc_o, y0_o, g_o, bo_o, m_o, hh_o, *, chunk, n_valid):
    j = pl.program_id(1)
    C = chunk
    C2 = 2 * C
    nb, tt, _ = h_ref.shape
    m = MIX_HALF
    bd = bd_ref[...]

    def mix_inputs(b):
        h = h_ref[b]
        prev = jnp.where(j == 0, p0_ref[b], hp_ref[b, 7:8, :])
        trow = lax.broadcasted_iota(jnp.int32, h.shape, 0)
        shifted = jnp.where(trow == 0, prev, pltpu.roll(h, 1, axis=0))
        hs = h + mu_ref[...] * (shifted - h)
        r = hs[:, 0:m]
        k = hs[:, m:2 * m]
        v = hs[:, 2 * m:3 * m]
        o = 3 * m
        d_decay = hs[:, o:o + DECAY_PAD]
        d_iclr = hs[:, o + DECAY_PAD:o + DECAY_PAD + ICLR_PAD]
        d_gate = hs[:, o + DECAY_PAD + ICLR_PAD:]
        w_log = -_softplus(-(w0_ref[...] + _dot(jnp.tanh(d_decay).astype(BF16), wd_ref[...]))) - 0.5
        lw = -jnp.exp(w_log)
        a = _sigmoid(a0_ref[...] + _dot(d_iclr.astype(BF16), wi_ref[...]))
        g_o[b] = _dot(_sigmoid(d_gate).astype(BF16), wg_ref[...])
        kk = k * kk_ref[...]
        kk = kk / jnp.maximum(jnp.sqrt(_group_sum(kk * kk, bd)), 1e-12)
        k = k * (1.0 + (a - 1.0) * ka_ref[...])
        bo_o[b] = _group_sum(r * k * rk_ref[...], bd) * v
        na = -kk
        bb = kk * a
        if n_valid < tt:
            ok = lax.broadcasted_iota(jnp.int32, (tt, m), 0) < n_valid
            zero = lambda x: jnp.where(ok, x, 0.0)
            r, k, v, lw, na, bb = zero(r), zero(k), zero(v), zero(lw), zero(na), zero(bb)
        return r, k, v, lw, na, bb

    row = lax.broadcasted_iota(jnp.int32, (C2, C2), 0)
    col = lax.broadcasted_iota(jnp.int32, (C2, C2), 1)
    rt = row & (C - 1)
    ct = col & (C - 1)
    strict = rt > ct
    incl = rt >= ct
    eye = (row == col).astype(F32)
    ti = lax.broadcasted_iota(jnp.int32, (C, C), 0)
    tj = lax.broadcasted_iota(jnp.int32, (C, C), 1)
    tril01 = (ti >= tj).astype(BF16)
    lane_lo = lax.broadcasted_iota(jnp.int32, (1, PAIR), 1) < HEAD_DIM

    def stack(x):
        return jnp.concatenate([jnp.where(lane_lo, x, 0.0), jnp.where(lane_lo, 0.0, x)], axis=0)

    prob = []
    for b in range(nb):
        r, k, v, lw, na, bb = mix_inputs(b)
        for ci in range(tt // C):
            rows = slice(ci * C, (ci + 1) * C)
            lwc = lw[rows]
            cum = _dot_01x(tril01, lwc)
            cum_end = cum[C - 1:C, :]
            e_neg = jnp.exp(-cum)
            e_end = jnp.exp(cum_end - cum)
            at = na[rows] * jnp.exp(cum - lwc)
            rt_ = r[rows] * jnp.exp(cum)
            bt = bb[rows] * e_neg
            kt = k[rows] * e_neg
            bh = bb[rows] * e_end
            kh = k[rows] * e_end
            wc = jnp.exp(cum_end)
            for hp in range(N_PAIRS):
                sl = slice(hp * PAIR, (hp + 1) * PAIR)
                v2 = stack(v[rows, sl])
                prob.append(dict(
                    b=b, ci=ci, sl=sl, hp=hp, rq=rt_[:, sl], wc=wc[:, sl],
                    at=stack(at[:, sl]).astype(BF16), rt=stack(rt_[:, sl]).astype(BF16),
                    bt=stack(bt[:, sl]).astype(BF16), kt=stack(kt[:, sl]).astype(BF16),
                    bh=stack(bh[:, sl]).astype(BF16), kh=stack(kh[:, sl]).astype(BF16),
                    v=v2.astype(BF16), vt=v2.T.astype(BF16)))

    mm = [_dot_nt(jnp.concatenate([p["at"], p["rt"]], axis=0), jnp.concatenate([p["bt"], p["kt"]], axis=0))
          for p in prob]
    n_ab = [jnp.where(strict, x[0:C2, 0:C2], 0.0) for x in mm]
    a_kr = [jnp.concatenate([jnp.where(strict, x[0:C2, C2:], 0.0),
                             jnp.where(incl, x[C2:, C2:], 0.0)], axis=0).astype(BF16) for x in mm]
    a_rb = [jnp.where(incl, x[C2:, 0:C2], 0.0).astype(BF16) for x in mm]
    kv = [_dot(x, p["v"]) for x, p in zip(a_kr, prob)]
    vtk = [_dot(p["vt"], p["kh"]) for p in prob]

    t_inv = [eye + jnp.where((rt >> 1) == (ct >> 1), x, 0.0) for x in n_ab]
    lvl = 1
    while (1 << lvl) < C:
        sel = ((rt >> (lvl + 1)) == (ct >> (lvl + 1))) & ((rt >> lvl) != (ct >> lvl))
        off = [jnp.where(sel, x, 0.0).astype(BF16) for x in n_ab]
        tb = [x.astype(BF16) for x in t_inv]
        xx = [_dot(x, y).astype(BF16) for x, y in zip(off, tb)]
        t_inv = [x + _dot(y, z) for x, y, z in zip(t_inv, tb, xx)]
        lvl += 1

    pu = [_dot(t.astype(BF16), jnp.concatenate([p["at"], x[0:C2].astype(BF16)], axis=1))
          for t, p, x in zip(t_inv, prob, kv)]
    ab = [_dot(x, y.astype(BF16)) for x, y in zip(a_rb, pu)]
    gb = [_dot(x.T.astype(BF16), p["bh"]) for x, p in zip(pu, prob)]
    for p, x_ab, x_kv, x_gb, x_vtk in zip(prob, ab, kv, gb, vtk):
        b = p["b"]
        rows = slice(p["ci"] * C, (p["ci"] + 1) * C)
        qc_o[b, rows, p["sl"]] = p["rq"] + x_ab[0:C, 0:PAIR] + x_ab[C:, 0:PAIR]
        y0 = x_ab[:, PAIR:] + x_kv[C2:]
        y0_o[b, rows, p["sl"]] = y0[0:C] + y0[C:]
        m_o[b, p["ci"], p["hp"]] = (x_gb[0:PAIR] + eye * p["wc"]).astype(BF16)
        hh_o[b, p["ci"], p["hp"]] = x_gb[PAIR:] + x_vtk


def _rwkv_chunk(hr, prev0, lp, *, nb, tt, chunk, n_valid):
    b, t, _ = hr.shape
    m = MIX_HALF
    g = tt // chunk
    tile = lambda i, j: (i, j, 0)
    fixed = lambda i, j: (0, 0)
    vec = pl.BlockSpec((1, m), fixed)
    tok = (pl.BlockSpec((nb, tt, m), tile), jax.ShapeDtypeStruct((b, t, m), F32))
    mat = lambda dt: (pl.BlockSpec((nb, g, N_PAIRS, PAIR, PAIR), lambda i, j: (i, j, 0, 0, 0)),
                      jax.ShapeDtypeStruct((b, t // chunk, N_PAIRS, PAIR, PAIR), dt))
    outs = [tok, tok, tok, tok, mat(BF16), mat(F32)]
    return pl.pallas_call(
        functools.partial(_rwkv_chunk_kernel, chunk=chunk, n_valid=n_valid),
        grid=(b // nb, t // tt),
        in_specs=[
            pl.BlockSpec((nb, tt, HR_WIDTH), tile),
            pl.BlockSpec((nb, 8, HR_WIDTH), lambda i, j: (i, jnp.maximum(j * (tt // 8) - 1, 0), 0)),
            pl.BlockSpec((nb, 1, HR_WIDTH), lambda i, j: (i, 0, 0)),
            pl.BlockSpec((1, HR_WIDTH), fixed),
            vec, vec, vec, vec, vec,
            pl.BlockSpec((DECAY_PAD, m), fixed),
            pl.BlockSpec((ICLR_PAD, m), fixed),
            pl.BlockSpec((GATE_PAD, m), fixed),
            pl.BlockSpec((m, m), fixed),
        ],
        out_specs=[o[0] for o in outs],
        out_shape=[o[1] for o in outs],
        compiler_params=_params("parallel", "arbitrary"),
        name="rwkv_chunk",
    )(hr, hr, prev0, lp["mu"], lp["w0"], lp["a0"], lp["k_k"], lp["k_a"], lp["r_k"],
      lp["w_decay"], lp["w_iclr"], lp["w_gate"], lp["bd512"])


def _rwkv_state_kernel(qc_ref, y0_ref, g_ref, bo_ref, m_ref, hh_ref, s0_ref, lng_ref, lnb_ref, bd_ref,
                       y_ref, sf_ref, s_ref, *, chunk, nsteps):
    j = pl.program_id(1)
    C = chunk
    nb, tt, _ = qc_ref.shape
    bd = bd_ref[...]

    @pl.when(j == 0)
    def _():
        s_ref[...] = s0_ref[...]

    cells = [(b, hp) for b in range(nb) for hp in range(N_PAIRS)]
    state = [s_ref[b, hp] for b, hp in cells]
    for ci in range(tt // C):
        rows = slice(ci * C, (ci + 1) * C)
        sb = [s.astype(BF16) for s in state]
        ys = [_dot_nt(qc_ref[b, rows, hp * PAIR:(hp + 1) * PAIR].astype(BF16), s)
              + y0_ref[b, rows, hp * PAIR:(hp + 1) * PAIR] for (b, hp), s in zip(cells, sb)]
        state = [_dot(s, m_ref[b, ci, hp]) + hh_ref[b, ci, hp] for (b, hp), s in zip(cells, sb)]
        means = [_group_sum(y, bd) * (1.0 / HEAD_DIM) for y in ys]
        yc = [y - mu for y, mu in zip(ys, means)]
        var = [_group_sum(y * y, bd) * (1.0 / HEAD_DIM) for y in yc]
        for (b, hp), y, vr in zip(cells, yc, var):
            sl = slice(hp * PAIR, (hp + 1) * PAIR)
            yn = y * lax.rsqrt(vr + GN_EPS) * lng_ref[:, sl] + lnb_ref[:, sl]
            y_ref[b, rows, sl] = (yn + bo_ref[b, rows, sl]) * g_ref[b, rows, sl]
    for (b, hp), s in zip(cells, state):
        s_ref[b, hp] = s

    @pl.when(j == nsteps - 1)
    def _():
        sf_ref[...] = s_ref[...]


def _rwkv_state(qc, y0, g, bonus, mm, hh, s0, lp, *, nb, tt, chunk):
    bsz, t, m = qc.shape
    gch = tt // chunk
    nsteps = t // tt
    tile = pl.BlockSpec((nb, tt, m), lambda i, j: (i, j, 0))
    mat = pl.BlockSpec((nb, gch, N_PAIRS, PAIR, PAIR), lambda i, j: (i, j, 0, 0, 0))
    st = pl.BlockSpec((nb, N_PAIRS, PAIR, PAIR), lambda i, j: (i, 0, 0, 0))
    vec = pl.BlockSpec((1, m), lambda i, j: (0, 0))
    return pl.pallas_call(
        functools.partial(_rwkv_state_kernel, chunk=chunk, nsteps=nsteps),
        grid=(bsz // nb, nsteps),
        in_specs=[tile, tile, tile, tile, mat, mat, st, vec, vec,
                  pl.BlockSpec((PAIR, PAIR), lambda i, j: (0, 0))],
        out_specs=[tile, st],
        out_shape=[jax.ShapeDtypeStruct((bsz, t, m), F32),
                   jax.ShapeDtypeStruct((bsz, N_PAIRS, PAIR, PAIR), F32)],
        scratch_shapes=[pltpu.VMEM((nb, N_PAIRS, PAIR, PAIR), F32)],
        compiler_params=_params("parallel", "arbitrary"),
        name="rwkv_state",
    )(qc, y0, g, bonus, mm, hh, s0, lp["lnx_g"], lp["lnx_b"], lp["bd128"])


def _bias_lane(head):
    return HEAD_DIM if head % 2 == 0 else 0


def _fox_bias_kernel(q_ref, k_ref, lf_ref, pk_ref, pq_ref, qa_ref, ka_ref, *, blk, nblk):
    ti = lax.broadcasted_iota(jnp.int32, (blk, blk), 0)
    tj = lax.broadcasted_iota(jnp.int32, (blk, blk), 1)
    tril01 = (ti >= tj).astype(BF16)
    lane = lax.broadcasted_iota(jnp.int32, (1, PAIR), 1)
    carry = jnp.zeros((1, LANES), F32)
    for i in range(nblk):
        sl = slice(i * blk, (i + 1) * blk)
        c = _dot_01x(tril01, lf_ref[0, sl, :]) + carry
        carry = c[blk - 1:blk, :]
        p1, p2, p3 = (jnp.where(lane < N_HEADS, x.astype(F32), 0.0) for x in _split3(c * LOG2E))
        pieces = (p1 + pltpu.roll(p2, N_HEADS, axis=1) + pltpu.roll(p3, 2 * N_HEADS, axis=1)).astype(BF16)
        for hp in range(N_PAIRS):
            ps = slice(hp * PAIR, (hp + 1) * PAIR)
            kp = k_ref[0, sl, ps]
            qp = q_ref[0, sl, ps] * (HEAD_DIM ** -0.5 * LOG2E)
            k_aug = _dot(pieces, pk_ref[hp])
            q_aug = _dot(pieces, pq_ref[hp])
            for half in range(2):
                h = 2 * hp + half
                own = (lane < HEAD_DIM) if half == 0 else (lane >= HEAD_DIM)
                off = lane - _bias_lane(h)
                hs = slice(h * PAIR, (h + 1) * PAIR)
                aug = slice(half * PAIR, (half + 1) * PAIR)
                ka_ref[0, sl, hs] = jnp.where(
                    own, kp, jnp.where((off >= 3) & (off < 6), 1.0, k_aug[:, aug])).astype(BF16)
                qa_ref[0, sl, hs] = jnp.where(
                    own, qp, jnp.where((off >= 0) & (off < 3), 1.0, q_aug[:, aug])).astype(BF16)


def _bias_placement():
    src = jnp.arange(LANES)
    piece, head = src // N_HEADS, src % N_HEADS
    dst = jnp.arange(2 * PAIR)
    pk, pq = [], []
    for hp in range(N_PAIRS):
        k_hit = jnp.zeros((LANES, 2 * PAIR), bool)
        q_hit = jnp.zeros((LANES, 2 * PAIR), bool)
        for half in range(2):
            h = 2 * hp + half
            mine = ((head == h) & (piece < 3))[:, None]
            lane0 = half * PAIR + _bias_lane(h)
            k_hit |= mine & (dst[None, :] == lane0 + piece[:, None])
            q_hit |= mine & (dst[None, :] == lane0 + 3 + piece[:, None])
        pk.append(jnp.where(k_hit, -1.0, 0.0))
        pq.append(jnp.where(q_hit, 1.0, 0.0))
    return jnp.stack(pk).astype(BF16), jnp.stack(pq).astype(BF16)


def _fox_bias(q, k, lf, place_k, place_q, *, blk):
    b, s, m = q.shape
    wide = N_HEADS * PAIR
    row = lambda width: pl.BlockSpec((1, s, width), lambda i: (i, 0, 0))
    place = pl.BlockSpec(place_k.shape, lambda i: (0, 0, 0))
    out = jax.ShapeDtypeStruct((b, s, wide), BF16)
    return pl.pallas_call(
        functools.partial(_fox_bias_kernel, blk=blk, nblk=s // blk),
        grid=(b,),
        in_specs=[row(m), row(m), row(LANES), place, place],
        out_specs=[row(wide), row(wide)],
        out_shape=[out, out],
        compiler_params=_params("parallel"),
        name="fox_bias",
    )(q, k, lf, place_k, place_q)


def _fox_prompt_kernel(qa_ref, ka_ref, vt_ref, gain_ref, o_ref, m_s, l_s, acc_s, *, blk):
    qi = pl.program_id(1)
    heads = range(N_HEADS)
    m_s[...] = jnp.full(m_s.shape, NEG_INF, F32)
    l_s[...] = jnp.zeros_like(l_s)
    acc_s[...] = jnp.zeros_like(acc_s)

    def block(j, diagonal):
        ks = pl.ds(pl.multiple_of(j * blk, blk), blk)
        st = [_dot_nt(ka_ref[0, ks, h * PAIR:(h + 1) * PAIR], qa_ref[0, :, h * PAIR:(h + 1) * PAIR])
              for h in heads]
        if diagonal:
            ki = lax.broadcasted_iota(jnp.int32, (blk, blk), 0)
            qj = lax.broadcasted_iota(jnp.int32, (blk, blk), 1)
            st = [jnp.where(ki <= qj, x, NEG_INF) for x in st]
        m_old = [m_s[h:h + 1, :] for h in heads]
        m_new = [jnp.maximum(mo, jnp.max(x, axis=0, keepdims=True)) for mo, x in zip(m_old, st)]
        p = [jnp.exp2(x - mn) for x, mn in zip(st, m_new)]
        pv = [_dot(vt_ref[0, 0, h * HEAD_DIM:(h + 1) * HEAD_DIM, ks].astype(BF16), x.astype(BF16))
              for h, x in zip(heads, p)]
        for h in heads:
            alpha = jnp.exp2(m_old[h] - m_new[h])
            m_s[h:h + 1, :] = m_new[h]
            l_s[h:h + 1, :] = alpha * l_s[h:h + 1, :] + jnp.sum(p[h], axis=0, keepdims=True)
            rows = slice(h * HEAD_DIM, (h + 1) * HEAD_DIM)
            acc_s[rows, :] = alpha * acc_s[rows, :] + pv[h]

    def body(j, carry):
        block(j, False)
        return carry

    lax.fori_loop(0, qi, body, 0)
    block(qi, True)
    for hp in range(N_PAIRS):
        halves = []
        for h in (2 * hp, 2 * hp + 1):
            o = acc_s[h * HEAD_DIM:(h + 1) * HEAD_DIM, :] / l_s[h:h + 1, :]
            ms = jnp.mean(o * o, axis=0, keepdims=True)
            halves.append(o * lax.rsqrt(ms + RMS_EPS))
        sl = slice(hp * PAIR, (hp + 1) * PAIR)
        o_ref[0, :, sl] = jnp.concatenate(halves, axis=0).T * gain_ref[:, sl]


def _fox_prompt(qa, ka, vt_all, gain, *, layer, blk):
    b, s, wide = qa.shape
    m = MIX_HALF
    nb = s // blk
    return pl.pallas_call(
        functools.partial(_fox_prompt_kernel, blk=blk),
        grid=(b, nb),
        in_specs=[
            pl.BlockSpec((1, blk, wide), lambda i, j: (i, j, 0)),
            pl.BlockSpec((1, s, wide), lambda i, j: (i, 0, 0)),
            pl.BlockSpec((1, 1, m, s), lambda i, j: (layer, i, 0, 0)),
            pl.BlockSpec((1, m), lambda i, j: (0, 0)),
        ],
        out_specs=pl.BlockSpec((1, blk, m), lambda i, j: (i, j, 0)),
        out_shape=jax.ShapeDtypeStruct((b, s, m), F32),
        scratch_shapes=[
            pltpu.VMEM((N_HEADS, blk), F32),
            pltpu.VMEM((N_HEADS, blk), F32),
            pltpu.VMEM((m, blk), F32),
        ],
        compiler_params=_params("parallel", "arbitrary"),
        name="fox_prompt",
    )(qa, ka, vt_all, gain)


def _fox_decode_kernel(pt_ref, q_ref, kn_ref, vn_ref, lfn_ref, lfnt_ref, gain_ref, hm_ref, bd_ref, *rest,
                       npg, nsteps, tnew):
    k_pages = rest[0:npg]
    v_pages = rest[npg:2 * npg]
    lft_ref = rest[2 * npg]
    o_ref, m_s, l_s, acc_s, carry_s, qbd_s, cn_s = rest[2 * npg + 1:]
    seq = pl.program_id(0)
    g = pl.program_id(1)
    nrow = tnew * N_HEADS
    page = k_pages[0].shape[3]
    li = lax.broadcasted_iota(jnp.int32, (LANES, LANES), 0)
    lj = lax.broadcasted_iota(jnp.int32, (LANES, LANES), 1)
    hm = hm_ref[...]

    def tile_rows(x):
        return jnp.concatenate([x] * tnew, axis=0)

    @pl.when(g == 0)
    def _():
        q = q_ref[0] * (HEAD_DIM ** -0.5)
        qbd = jnp.concatenate([jnp.broadcast_to(q[t:t + 1, :], hm.shape) * hm for t in range(tnew)], axis=0)
        qbd_s[...] = qbd.astype(BF16)
        cn = lfn_ref[0]
        trow = lax.broadcasted_iota(jnp.int32, cn.shape, 0)
        sh = 1
        while sh < tnew:
            cn = cn + jnp.where(trow >= sh, pltpu.roll(cn, sh, axis=0), 0.0)
            sh *= 2
        hsel = (lax.broadcasted_iota(jnp.int32, (N_HEADS, LANES), 0)
                == lax.broadcasted_iota(jnp.int32, (N_HEADS, LANES), 1)).astype(F32)
        cn_rows = jnp.concatenate(
            [jnp.sum(jnp.broadcast_to(cn[t:t + 1, :], hsel.shape) * hsel, axis=1, keepdims=True)
             for t in range(tnew)], axis=0)
        cn_s[...] = cn_rows
        cnt = _dot_x01(lfnt_ref[0], (li <= lj).astype(BF16))
        pad = jnp.zeros((page - tnew, MIX_HALF), F32)
        kn = jnp.concatenate([kn_ref[0], pad], axis=0).astype(BF16)
        vn = jnp.concatenate([vn_ref[0], pad], axis=0).astype(BF16)
        s = _dot_nt(qbd_s[...], kn) + (cn_rows - tile_rows(cnt))
        rtok = lax.broadcasted_iota(jnp.int32, (nrow, LANES), 0) >> 3
        ktok = lax.broadcasted_iota(jnp.int32, (nrow, LANES), 1)
        s = jnp.where(ktok <= rtok, s, NEG_INF)
        m = jnp.max(s, axis=1, keepdims=True)
        p = jnp.exp(s - m)
        m_s[...] = m
        l_s[...] = jnp.sum(p, axis=1, keepdims=True)
        acc_s[...] = _dot(p.astype(BF16), vn)
        carry_s[...] = jnp.zeros_like(carry_s)

    triu_strict = (li > lj).astype(BF16)
    run = carry_s[...]
    qbd = qbd_s[...]
    cn_rows = cn_s[...]
    scores = [None] * npg
    for i in range(npg - 1, -1, -1):
        lf = lft_ref[0, pt_ref[seq, (nsteps - 1 - g) * npg + i]]
        bias = _dot_x01(lf, triu_strict) + run
        run = run + jnp.sum(lf, axis=1, keepdims=True)
        scores[i] = _dot(qbd, k_pages[i][0, 0].astype(BF16)) + (tile_rows(bias) + cn_rows)
    carry_s[...] = run
    s = jnp.concatenate(scores, axis=1)
    m_old = m_s[...]
    m_new = jnp.maximum(m_old, jnp.max(s, axis=1, keepdims=True))
    alpha = jnp.exp(m_old - m_new)
    p = jnp.exp(s - m_new)
    m_s[...] = m_new
    l_s[...] = alpha * l_s[...] + jnp.sum(p, axis=1, keepdims=True)
    pv = _dot_nt(p[:, 0:page].astype(BF16), v_pages[0][0, 0].astype(BF16))
    for i in range(1, npg):
        pv = pv + _dot_nt(p[:, i * page:(i + 1) * page].astype(BF16), v_pages[i][0, 0].astype(BF16))
    acc_s[...] = alpha * acc_s[...] + pv

    @pl.when(g == nsteps - 1)
    def _():
        o_rows = acc_s[...] / l_s[...]
        o = jnp.concatenate(
            [jnp.sum(o_rows[t * N_HEADS:(t + 1) * N_HEADS, :] * hm, axis=0, keepdims=True) for t in range(tnew)],
            axis=0)
        ms = _dot_x01(o * o, bd_ref[...]) * (1.0 / HEAD_DIM)
        o_ref[0] = o * lax.rsqrt(ms + RMS_EPS) * gain_ref[...]


def _fox_decode(layer, page_table, q, kn, vn, lfn, lfn_t, cache_k, cache_v, cache_lft, gain, hm, bd, *, npg):
    b, tnew, m = q.shape
    n_pages = page_table.shape[1]
    page = cache_k.shape[3]
    nsteps = n_pages // npg
    nrow = tnew * N_HEADS

    def tok(i, g, pt):
        return (i, 0, 0)

    def fixed(i, g, pt):
        return (0, 0)

    def page_map(slot):
        return lambda i, g, pt: (layer, pt[i, (nsteps - 1 - g) * npg + slot], 0, 0)

    kv_specs = [pl.BlockSpec((1, 1, m, page), page_map(s)) for s in range(npg)]
    n_pool = cache_lft.shape[1]
    lf_spec = pl.BlockSpec((1, n_pool, N_HEADS, page), lambda i, g, pt: (layer, 0, 0, 0))
    grid_spec = pltpu.PrefetchScalarGridSpec(
        num_scalar_prefetch=1,
        grid=(b, nsteps),
        in_specs=[
            pl.BlockSpec((1, tnew, m), tok),
            pl.BlockSpec((1, tnew, m), tok),
            pl.BlockSpec((1, tnew, m), tok),
            pl.BlockSpec((1, tnew, LANES), tok),
            pl.BlockSpec((1, N_HEADS, LANES), tok),
            pl.BlockSpec((1, m), fixed),
            pl.BlockSpec((N_HEADS, m), fixed),
            pl.BlockSpec((m, m), fixed),
        ] + kv_specs + kv_specs + [lf_spec],
        out_specs=pl.BlockSpec((1, tnew, m), tok),
        scratch_shapes=[
            pltpu.VMEM((nrow, 1), F32),
            pltpu.VMEM((nrow, 1), F32),
            pltpu.VMEM((nrow, m), F32),
            pltpu.VMEM((N_HEADS, 1), F32),
            pltpu.VMEM((nrow, m), BF16),
            pltpu.VMEM((nrow, 1), F32),
        ],
    )
    return pl.pallas_call(
        functools.partial(_fox_decode_kernel, npg=npg, nsteps=nsteps, tnew=tnew),
        grid_spec=grid_spec,
        out_shape=jax.ShapeDtypeStruct((b, tnew, m), F32),
        compiler_params=_params("parallel", "arbitrary"),
        name="fox_decode",
    )(page_table, q, kn, vn, lfn, lfn_t, gain, hm, bd,
      *([cache_k] * npg), *([cache_v] * npg), cache_lft)


def _out_proj_kernel(x_ref, yr_ref, yf_ref, wr_ref, wf_ref, o_ref):
    o_ref[...] = (x_ref[...] + _dot(yr_ref[...].astype(BF16), wr_ref[...])
                  + _dot(yf_ref[...].astype(BF16), wf_ref[...]))


def _out_proj(x, yr, yf, wr, wf, *, tm):
    n, d = x.shape
    m = yr.shape[1]
    row = lambda i: (i, 0)
    fixed = lambda i: (0, 0)
    return pl.pallas_call(
        _out_proj_kernel,
        grid=(n // tm,),
        in_specs=[pl.BlockSpec((tm, d), row), pl.BlockSpec((tm, m), row), pl.BlockSpec((tm, m), row),
                  pl.BlockSpec((m, d), fixed), pl.BlockSpec((m, d), fixed)],
        out_specs=pl.BlockSpec((tm, d), row),
        out_shape=jax.ShapeDtypeStruct((n, d), F32),
        compiler_params=_params("parallel"),
        name="out_proj",
    )(x, yr, yf, wr, wf)


def _ple_kernel(x_ref, p_ref, g_ref, wg_ref, wu_ref, fg_ref, o_ref, *, final):
    x = x_ref[...]
    gate = _sigmoid(_dot(_rms(x, g_ref[...]).astype(BF16), wg_ref[...]))
    y = x + gate * _dot(p_ref[...].astype(BF16), wu_ref[...])
    o_ref[...] = _rms(y, fg_ref[...]) if final else y


def _ple(x, p, g, wg, wu, fg, *, tm, final):
    n, d = x.shape
    pd = p.shape[1]
    row = lambda i: (i, 0)
    fixed = lambda i: (0, 0)
    return pl.pallas_call(
        functools.partial(_ple_kernel, final=final),
        grid=(n // tm,),
        in_specs=[pl.BlockSpec((tm, d), row), pl.BlockSpec((tm, pd), row), pl.BlockSpec((1, d), fixed),
                  pl.BlockSpec((d, d), fixed), pl.BlockSpec((pd, d), fixed), pl.BlockSpec((1, d), fixed)],
        out_specs=pl.BlockSpec((tm, d), row),
        out_shape=jax.ShapeDtypeStruct((n, d), F32),
        compiler_params=_params("parallel"),
        name="ple",
    )(x, p, g, wg, wu, fg)


def _pad_cols(w, width):
    return jnp.pad(w, [(0, 0)] * (w.ndim - 1) + [(0, width - w.shape[-1])])


def _pad_rows(w, height):
    return jnp.pad(w, [(0, 0)] * (w.ndim - 2) + [(0, height - w.shape[-2]), (0, 0)])


def _pack_hr(h):
    m = MIX_HALF
    o = 3 * m
    return jnp.concatenate([
        h[..., :o],
        _pad_cols(h[..., o:o + DECAY_RANK], DECAY_PAD),
        _pad_cols(h[..., o + DECAY_RANK:o + DECAY_RANK + ICLR_RANK], ICLR_PAD),
        _pad_cols(h[..., o + DECAY_RANK + ICLR_RANK:], GATE_PAD)], axis=-1)


def _unpack_hr(h):
    m = MIX_HALF
    o = 3 * m
    return jnp.concatenate([
        h[..., :o],
        h[..., o:o + DECAY_RANK],
        h[..., o + DECAY_PAD:o + DECAY_PAD + ICLR_RANK],
        h[..., o + DECAY_PAD + ICLR_PAD:o + DECAY_PAD + ICLR_PAD + GATE_RANK]], axis=-1)


def _pair_state(s):
    b = s.shape[0]
    s = s.reshape(b, N_PAIRS, 2, HEAD_DIM, HEAD_DIM)
    z = jnp.zeros_like(s[:, :, 0])
    top = jnp.concatenate([s[:, :, 0], z], axis=-1)
    bot = jnp.concatenate([z, s[:, :, 1]], axis=-1)
    return jnp.concatenate([top, bot], axis=-2)


def _unpair_state(s):
    b = s.shape[0]
    d = HEAD_DIM
    return jnp.stack([s[:, :, :d, :d], s[:, :, d:, d:]], axis=2).reshape(b, N_HEADS, d, d)


def _block_diag01(n):
    i = jnp.arange(n) // HEAD_DIM
    return (i[:, None] == i[None, :]).astype(BF16)


def _rwkv_mix(hr3, prev0, s0, lp, *, n_valid):
    bsz, t, _ = hr3.shape
    chunk = REC_CHUNK
    tt = min(t, REC_TILE)
    nb = max(1, min(bsz, REC_TILE // tt))
    qc, y0, g, bonus, mm, hh = _rwkv_chunk(hr3, prev0, lp, nb=nb, tt=tt, chunk=chunk, n_valid=n_valid)
    return _rwkv_state(qc, y0, g, bonus, mm, hh, s0, lp, nb=min(bsz, STATE_BATCH), tt=min(t, STATE_TILE),
                       chunk=chunk)


def kernel(x_prompt, x_sample, cache_k, cache_v, cache_logf, state_wkv, state_shift, page_table, p_prompt, p_sample, ffn1_norm, ffn1_w_gate, ffn1_w_up, ffn1_w_down, mix_norm, w_in, rwkv_mu, rwkv_w0, rwkv_w_decay, rwkv_a0, rwkv_w_iclr, rwkv_w_gate, rwkv_k_k, rwkv_k_a, rwkv_r_k, rwkv_lnx_g, rwkv_lnx_b, fox_b_f, fox_out_norm, w_out, ffn2_norm, ffn2_w_gate, ffn2_w_up, ffn2_w_down, ple_norm, ple_w_gate, ple_w_up, final_norm):
    depth = w_in.shape[0]
    bp, sp, d = x_prompt.shape
    bs, ts, _ = x_sample.shape
    m = MIX_HALF
    n_pool, page = cache_k.shape[1], cache_k.shape[2]
    npr, nsm = bp * sp, bs * ts
    pd = p_prompt.shape[-1]

    fox_cols = w_in[:, :, RWKV_PROJ:]
    w_pack = jnp.concatenate([
        fox_cols[:, :, :3 * m],
        _pack_hr(w_in[:, :, :RWKV_PROJ]),
        _pad_cols(fox_cols[:, :, 3 * m:], LANES)], axis=-1).astype(BF16)
    w_t = jnp.swapaxes(_pad_cols(fox_cols[:, :, m:], 2 * m + LANES), 1, 2).astype(BF16)
    b_f = _pad_cols(fox_b_f, LANES)[:, None, :]
    b_fc = fox_b_f[:, :, None]
    bd512 = _block_diag01(m)
    bd128 = _block_diag01(PAIR)
    hm = (jnp.arange(m)[None, :] // HEAD_DIM == jnp.arange(N_HEADS)[:, None]).astype(F32)
    bf = lambda w: w.astype(BF16)
    f1g, f1u, f1d = bf(ffn1_w_gate), bf(ffn1_w_up), bf(ffn1_w_down)
    f2g, f2u, f2d = bf(ffn2_w_gate), bf(ffn2_w_up), bf(ffn2_w_down)
    wo = bf(w_out)
    pg, pu = bf(ple_w_gate), bf(ple_w_up)
    wdec = bf(_pad_rows(rwkv_w_decay, DECAY_PAD))
    wicl = bf(_pad_rows(rwkv_w_iclr, ICLR_PAD))
    wgat = bf(_pad_rows(rwkv_w_gate, GATE_PAD))
    mu = _pack_hr(rwkv_mu)
    cache_kt = jnp.transpose(cache_k, (0, 1, 3, 4, 2)).reshape(depth, n_pool, m, page)
    cache_vt = jnp.transpose(cache_v, (0, 1, 3, 4, 2)).reshape(depth, n_pool, m, page)
    cache_lft = jnp.swapaxes(cache_logf, 2, 3)
    fg = final_norm[None, :]
    place_k, place_q = _bias_placement()

    tm_p = 1024 if npr % 1024 == 0 else ATT_BLOCK
    tm_ffn = FFN_TILE if npr % FFN_TILE == 0 else tm_p
    stacked = None
    tf = 256
    blk = ATT_BLOCK
    chunk = REC_CHUNK
    npg = min(DEC_PAGES_PER_STEP, page_table.shape[1])

    xp = x_prompt.reshape(npr, d)
    xs = x_sample.reshape(nsm, d)
    outs = [[] for _ in range(10)]
    for l in range(depth):
        lp = dict(mu=mu[l][None], w0=rwkv_w0[l][None], a0=rwkv_a0[l][None], k_k=rwkv_k_k[l][None],
                  k_a=rwkv_k_a[l][None], r_k=rwkv_r_k[l].reshape(1, m), w_decay=wdec[l], w_iclr=wicl[l],
                  w_gate=wgat[l], bd512=bd512, bd128=bd128, lnx_g=rwkv_lnx_g[l][None],
                  lnx_b=rwkv_lnx_b[l][None])
        gain = fox_out_norm[l].reshape(1, m)
        last = l == depth - 1

        xp = _ffn(xp, ffn1_norm[l][None], f1g[l], f1u[l], f1d[l], tm=tm_ffn, tf=tf)
        q, k, kt_all, vt_all, hr, lf, lft_all = _proj(
            xp, mix_norm[l][None], w_pack[l], w_t[l], b_f[l], b_fc[l], tm=blk, seq=sp, prompt=True,
            layer=l, depth=depth, stacked=stacked)
        stacked = (kt_all, vt_all, lft_all)
        hr3 = hr.reshape(bp, sp, HR_WIDTH)
        y_r, s_fin = _rwkv_mix(hr3, jnp.zeros((bp, 1, HR_WIDTH), F32),
                               jnp.zeros((bp, N_PAIRS, PAIR, PAIR), F32), lp, n_valid=sp)
        qa, ka = _fox_bias(q.reshape(bp, sp, m), k.reshape(bp, sp, m), lf.reshape(bp, sp, LANES),
                           place_k, place_q, blk=blk)
        y_f = _fox_prompt(qa, ka, vt_all, gain, layer=l, blk=blk)
        xp = _out_proj(xp, y_r.reshape(npr, m), y_f.reshape(npr, m), wo[l, :m], wo[l, m:], tm=tm_p)
        xp = _ffn(xp, ffn2_norm[l][None], f2g[l], f2u[l], f2d[l], tm=tm_ffn, tf=tf)
        xp = _ple(xp, p_prompt[l].reshape(npr, pd), ple_norm[l][None], pg[l], pu[l], fg, tm=tm_p, final=last)
        outs[3].append(_unpair_state(s_fin))
        outs[4].append(_unpack_hr(hr3[:, -1, :]))

        xs = _ffn(xs, ffn1_norm[l][None], f1g[l], f1u[l], f1d[l], tm=nsm, tf=tf)
        q, k, v, hr, lf = _proj(xs, mix_norm[l][None], w_pack[l], w_t[l], b_f[l], b_fc[l],
                                tm=nsm, seq=nsm, prompt=False)
        hr3 = hr.reshape(bs, ts, HR_WIDTH)
        hr_pad = jnp.pad(hr3, ((0, 0), (0, chunk - ts), (0, 0)))
        y_r, s_fin = _rwkv_mix(hr_pad, _pack_hr(state_shift[l])[:, None, :], _pair_state(state_wkv[l]), lp,
                               n_valid=ts)
        lf3 = lf.reshape(bs, ts, LANES)
        lf_t = _pad_cols(jnp.swapaxes(lf3[:, :, :N_HEADS], 1, 2), LANES)
        y_f = _fox_decode(l, page_table, q.reshape(bs, ts, m), k.reshape(bs, ts, m), v.reshape(bs, ts, m),
                          lf3, lf_t, cache_kt, cache_vt, cache_lft, gain, hm, bd512, npg=npg)
        xs = _out_proj(xs, y_r[:, :ts].reshape(nsm, m), y_f.reshape(nsm, m), wo[l, :m], wo[l, m:], tm=nsm)
        xs = _ffn(xs, ffn2_norm[l][None], f2g[l], f2u[l], f2d[l], tm=nsm, tf=tf)
        xs = _ple(xs, p_sample[l].reshape(nsm, pd), ple_norm[l][None], pg[l], pu[l], fg, tm=nsm, final=last)
        outs[5].append(k.reshape(bs, ts, N_HEADS, HEAD_DIM))
        outs[6].append(v.reshape(bs, ts, N_HEADS, HEAD_DIM))
        outs[7].append(lf3[:, :, :N_HEADS])
        outs[8].append(_unpair_state(s_fin))
        outs[9].append(_unpack_hr(hr3[:, -1, :]))

    st = [jnp.stack(o) for o in outs[3:]]
    k_p, v_p = (jnp.transpose(t.reshape(depth, bp, N_HEADS, HEAD_DIM, sp), (0, 1, 4, 2, 3)) for t in stacked[:2])
    lf_p = jnp.swapaxes(stacked[2], 2, 3)
    return (xp.reshape(bp, sp, d), xs.reshape(bs, ts, d), k_p, v_p, lf_p, *st)
```

```python
import functools

import jax
import jax.numpy as jnp
from jax import lax
from jax.experimental import pallas as pl
from jax.experimental.pallas import tpu as pltpu

F32 = jnp.float32
BF16 = jnp.bfloat16

HEAD_DIM = 64
N_HEADS = 8
MIX_HALF = N_HEADS * HEAD_DIM
PAIR = 2 * HEAD_DIM
N_PAIRS = N_HEADS // 2
DECAY_RANK = 64
ICLR_RANK = 64
GATE_RANK = 160
RWKV_PROJ = 3 * MIX_HALF + DECAY_RANK + ICLR_RANK + GATE_RANK
LANES = 128
DECAY_PAD = 128
ICLR_PAD = 128
GATE_PAD = 256
HR_WIDTH = 3 * MIX_HALF + DECAY_PAD + ICLR_PAD + GATE_PAD
RMS_EPS = 1e-6
GN_EPS = 64e-5
NEG_INF = -1e30
LOG2E = 1.4426950408889634
VMEM_LIMIT = 56 * 1024 * 1024

REC_CHUNK = 64
REC_TILE = 256
STATE_TILE = 128
STATE_BATCH = 8
ATT_BLOCK = 256
FFN_TILE = 2048
DEC_PAGES_PER_STEP = 16


def _params(*sem):
    return pltpu.CompilerParams(dimension_semantics=sem, vmem_limit_bytes=VMEM_LIMIT)


def _dot(a, b):
    return jnp.dot(a, b, preferred_element_type=F32)


def _dot_nt(a, b):
    return lax.dot_general(a, b, (((1,), (1,)), ((), ())), preferred_element_type=F32)


def _split3(x):
    x1 = x.astype(BF16)
    r1 = x - x1.astype(F32)
    x2 = r1.astype(BF16)
    r2 = r1 - x2.astype(F32)
    return x1, x2, r2.astype(BF16)


def _dot_x01(x, m01):
    x1, x2, x3 = _split3(x)
    return _dot(x1, m01) + _dot(x2, m01) + _dot(x3, m01)


def _group_sum(x, m01):
    x1 = x.astype(BF16)
    return _dot(x1, m01) + _dot((x - x1.astype(F32)).astype(BF16), m01)


def _dot_01x(m01, x):
    x1, x2, x3 = _split3(x)
    return _dot(m01, x1) + _dot(m01, x2) + _dot(m01, x3)


def _rms(x, g):
    ms = jnp.mean(x * x, axis=-1, keepdims=True)
    return x * lax.rsqrt(ms + RMS_EPS) * g


def _sigmoid(x):
    return 1.0 / (1.0 + jnp.exp(-x))


def _softplus(z):
    return jnp.maximum(z, 0.0) + jnp.log(1.0 + jnp.exp(-jnp.abs(z)))


def _ffn_kernel(x_ref, g_ref, wg_ref, wu_ref, wd_ref, o_ref, xn_ref, *, nf):
    j = pl.program_id(1)

    @pl.when(j == 0)
    def _():
        xn_ref[...] = _rms(x_ref[...], g_ref[...]).astype(BF16)
        o_ref[...] = jnp.zeros_like(o_ref)

    xn = xn_ref[...]
    gate = _dot(xn, wg_ref[...].astype(BF16))
    up = _dot(xn, wu_ref[...].astype(BF16))
    h = (gate * _sigmoid(gate) * up).astype(BF16)
    o_ref[...] += _dot(h, wd_ref[...].astype(BF16))

    @pl.when(j == nf - 1)
    def _():
        o_ref[...] = x_ref[...] + 0.5 * o_ref[...]


def _ffn(x, g, wg, wu, wd, *, tm, tf):
    n, d = x.shape
    f = wg.shape[1]
    nf = f // tf
    return pl.pallas_call(
        functools.partial(_ffn_kernel, nf=nf),
        grid=(n // tm, nf),
        in_specs=[
            pl.BlockSpec((tm, d), lambda i, j: (i, 0)),
            pl.BlockSpec((1, d), lambda i, j: (0, 0)),
            pl.BlockSpec((d, tf), lambda i, j: (0, j)),
            pl.BlockSpec((d, tf), lambda i, j: (0, j)),
            pl.BlockSpec((tf, d), lambda i, j: (j, 0)),
        ],
        out_specs=pl.BlockSpec((tm, d), lambda i, j: (i, 0)),
        out_shape=jax.ShapeDtypeStruct((n, d), F32),
        scratch_shapes=[pltpu.VMEM((tm, d), BF16)],
        compiler_params=_params("parallel", "arbitrary"),
        name="ffn",
    )(x, g, wg, wu, wd)


def _proj_kernel(x_ref, g_ref, w_ref, bf_ref, *refs, prompt, n_alias):
    out_refs = refs[n_alias:]
    xn = _rms(x_ref[...], g_ref[...]).astype(BF16)
    h = MIX_HALF
    if prompt:
        q_ref, k_ref, kt_ref, vt_ref, hr_ref, lf_ref, lft_ref, v_ref = out_refs
    else:
        q_ref, k_ref, v_ref, hr_ref, lf_ref = out_refs
    q_ref[...] = _dot(xn, w_ref[:, 0:h])
    k_ref[...] = _dot(xn, w_ref[:, h:2 * h])
    v_ref[...] = _dot(xn, w_ref[:, 2 * h:3 * h])
    hr_ref[...] = _dot(xn, w_ref[:, 3 * h:3 * h + HR_WIDTH])
    lf_ref[...] = -_softplus(-(_dot(xn, w_ref[:, 3 * h + HR_WIDTH:]) + bf_ref[...]))
    if prompt:
        kt_ref[0, 0] = k_ref[...].T
        vt_ref[0, 0] = v_ref[...].T
        lft_ref[0, 0] = lf_ref[...].T[0:N_HEADS, :]


def _proj(x, g, w, bf, *, tm, seq, prompt, layer=0, depth=1, stacked=None):
    n, d = x.shape
    h = MIX_HALF
    nj = seq // tm
    row = lambda i: (i, 0)
    fixed = lambda i: (0, 0)
    trans = lambda i: (layer, i // nj, 0, i % nj)
    rows = lambda width: (pl.BlockSpec((tm, width), row), jax.ShapeDtypeStruct((n, width), F32))
    cols = lambda height: (pl.BlockSpec((1, 1, height, tm), trans),
                           jax.ShapeDtypeStruct((depth, n // seq, height, seq), F32))
    if prompt:
        outs = [rows(h), rows(h), cols(h), cols(h), rows(HR_WIDTH), rows(LANES), cols(N_HEADS)]
    else:
        outs = [rows(h), rows(h), rows(h), rows(HR_WIDTH), rows(LANES)]
    stacked = tuple(stacked or ())
    n_in = 4
    aliases = {n_in + i: o for i, o in enumerate((2, 3, 6)[:len(stacked)])}
    return pl.pallas_call(
        functools.partial(_proj_kernel, prompt=prompt, n_alias=len(stacked)),
        grid=(n // tm,),
        in_specs=[
            pl.BlockSpec((tm, d), row),
            pl.BlockSpec((1, d), fixed),
            pl.BlockSpec(w.shape, fixed),
            pl.BlockSpec((1, LANES), fixed),
        ] + [pl.BlockSpec(memory_space=pl.ANY)] * len(stacked),
        out_specs=[o[0] for o in outs],
        out_shape=[o[1] for o in outs],
        scratch_shapes=[pltpu.VMEM((tm, h), F32)] if prompt else [],
        input_output_aliases=aliases,
        compiler_params=_params("parallel"),
        name="proj",
    )(x, g, w, bf, *stacked)


def _rwkv_chunk_kernel(h_ref, hp_ref, p0_ref, mu_ref, w0_ref, a0_ref, kk_ref, ka_ref, rk_ref,
                       wd_ref, wi_ref, wg_ref, bd_ref,
                       qc_o, y0_o, g_o, bo_o, m_o, hh_o, *, chunk, n_valid):
    j = pl.program_id(1)
    C = chunk
    C2 = 2 * C
    nb, tt, _ = h_ref.shape
    m = MIX_HALF
    bd = bd_ref[...]

    def mix_inputs(b):
        h = h_ref[b]
        prev = jnp.where(j == 0, p0_ref[b], hp_ref[b, 7:8, :])
        trow = lax.broadcasted_iota(jnp.int32, h.shape, 0)
        shifted = jnp.where(trow == 0, prev, pltpu.roll(h, 1, axis=0))
        hs = h + mu_ref[...] * (shifted - h)
        r = hs[:, 0:m]
        k = hs[:, m:2 * m]
        v = hs[:, 2 * m:3 * m]
        o = 3 * m
        d_decay = hs[:, o:o + DECAY_PAD]
        d_iclr = hs[:, o + DECAY_PAD:o + DECAY_PAD + ICLR_PAD]
        d_gate = hs[:, o + DECAY_PAD + ICLR_PAD:]
        w_log = -_softplus(-(w0_ref[...] + _dot(jnp.tanh(d_decay).astype(BF16), wd_ref[...]))) - 0.5
        lw = -jnp.exp(w_log)
        a = _sigmoid(a0_ref[...] + _dot(d_iclr.astype(BF16), wi_ref[...]))
        g_o[b] = _dot(_sigmoid(d_gate).astype(BF16), wg_ref[...])
        kk = k * kk_ref[...]
        kk = kk / jnp.maximum(jnp.sqrt(_group_sum(kk * kk, bd)), 1e-12)
        k = k * (1.0 + (a - 1.0) * ka_ref[...])
        bo_o[b] = _group_sum(r * k * rk_ref[...], bd) * v
        na = -kk
        bb = kk * a
        if n_valid < tt:
            ok = lax.broadcasted_iota(jnp.int32, (tt, m), 0) < n_valid
            zero = lambda x: jnp.where(ok, x, 0.0)
            r, k, v, lw, na, bb = zero(r), zero(k), zero(v), zero(lw), zero(na), zero(bb)
        return r, k, v, lw, na, bb

    row = lax.broadcasted_iota(jnp.int32, (C2, C2), 0)
    col = lax.broadcasted_iota(jnp.int32, (C2, C2), 1)
    rt = row & (C - 1)
    ct = col & (C - 1)
    strict = rt > ct
    incl = rt >= ct
    eye = (row == col).astype(F32)
    ti = lax.broadcasted_iota(jnp.int32, (C, C), 0)
    tj = lax.broadcasted_iota(jnp.int32, (C, C), 1)
    tril01 = (ti >= tj).astype(BF16)
    lane_lo = lax.broadcasted_iota(jnp.int32, (1, PAIR), 1) < HEAD_DIM

    def stack(x):
        return jnp.concatenate([jnp.where(lane_lo, x, 0.0), jnp.where(lane_lo, 0.0, x)], axis=0)

    prob = []
    for b in range(nb):
        r, k, v, lw, na, bb = mix_inputs(b)
        for ci in range(tt // C):
            rows = slice(ci * C, (ci + 1) * C)
            lwc = lw[rows]
            cum = _dot_01x(tril01, lwc)
            cum_end = cum[C - 1:C, :]
            e_neg = jnp.exp(-cum)
            e_end = jnp.exp(cum_end - cum)
            at = na[rows] * jnp.exp(cum - lwc)
            rt_ = r[rows] * jnp.exp(cum)
            bt = bb[rows] * e_neg
            kt = k[rows] * e_neg
            bh = bb[rows] * e_end
            kh = k[rows] * e_end
            wc = jnp.exp(cum_end)
            for hp in range(N_PAIRS):
                sl = slice(hp * PAIR, (hp + 1) * PAIR)
                v2 = stack(v[rows, sl])
                prob.append(dict(
                    b=b, ci=ci, sl=sl, hp=hp, rq=rt_[:, sl], wc=wc[:, sl],
                    at=stack(at[:, sl]).astype(BF16), rt=stack(rt_[:, sl]).astype(BF16),
                    bt=stack(bt[:, sl]).astype(BF16), kt=stack(kt[:, sl]).astype(BF16),
                    bh=stack(bh[:, sl]).astype(BF16), kh=stack(kh[:, sl]).astype(BF16),
                    v=v2.astype(BF16), vt=v2.T.astype(BF16)))

    mm = [_dot_nt(jnp.concatenate([p["at"], p["rt"]], axis=0), jnp.concatenate([p["bt"], p["kt"]], axis=0))
          for p in prob]
    n_ab = [jnp.where(strict, x[0:C2, 0:C2], 0.0) for x in mm]
    a_kr = [jnp.concatenate([jnp.where(strict, x[0:C2, C2:], 0.0),
                             jnp.where(incl, x[C2:, C2:], 0.0)], axis=0).astype(BF16) for x in mm]
    a_rb = [jnp.where(incl, x[C2:, 0:C2], 0.0).astype(BF16) for x in mm]
    kv = [_dot(x, p["v"]) for x, p in zip(a_kr, prob)]
    vtk = [_dot(p["vt"], p["kh"]) for p in prob]

    t_inv = [eye + jnp.where((rt >> 1) == (ct >> 1), x, 0.0) for x in n_ab]
    lvl = 1
    while (1 << lvl) < C:
        sel = ((rt >> (lvl + 1)) == (ct >> (lvl + 1))) & ((rt >> lvl) != (ct >> lvl))
        off = [jnp.where(sel, x, 0.0).astype(BF16) for x in n_ab]
        tb = [x.astype(BF16) for x in t_inv]
        xx = [_dot(x, y).astype(BF16) for x, y in zip(off, tb)]
        t_inv = [x + _dot(y, z) for x, y, z in zip(t_inv, tb, xx)]
        lvl += 1

    pu = [_dot(t.astype(BF16), jnp.concatenate([p["at"], x[0:C2].astype(BF16)], axis=1))
          for t, p, x in zip(t_inv, prob, kv)]
    ab = [_dot(x, y.astype(BF16)) for x, y in zip(a_rb, pu)]
    gb = [_dot(x.T.astype(BF16), p["bh"]) for x, p in zip(pu, prob)]
    for p, x_ab, x_kv, x_gb, x_vtk in zip(prob, ab, kv, gb, vtk):
        b = p["b"]
        rows = slice(p["ci"] * C, (p["ci"] + 1) * C)
        qc_o[b, rows, p["sl"]] = p["rq"] + x_ab[0:C, 0:PAIR] + x_ab[C:, 0:PAIR]
        y0 = x_ab[:, PAIR:] + x_kv[C2:]
        y0_o[b, rows, p["sl"]] = y0[0:C] + y0[C:]
        m_o[b, p["ci"], p["hp"]] = (x_gb[0:PAIR] + eye * p["wc"]).astype(BF16)
        hh_o[b, p["ci"], p["hp"]] = x_gb[PAIR:] + x_vtk


def _rwkv_chunk(hr, prev0, lp, *, nb, tt, chunk, n_valid):
    b, t, _ = hr.shape
    m = MIX_HALF
    g = tt // chunk
    tile = lambda i, j: (i, j, 0)
    fixed = lambda i, j: (0, 0)
    vec = pl.BlockSpec((1, m), fixed)
    tok = (pl.BlockSpec((nb, tt, m), tile), jax.ShapeDtypeStruct((b, t, m), F32))
    mat = lambda dt: (pl.BlockSpec((nb, g, N_PAIRS, PAIR, PAIR), lambda i, j: (i, j, 0, 0, 0)),
                      jax.ShapeDtypeStruct((b, t // chunk, N_PAIRS, PAIR, PAIR), dt))
    outs = [tok, tok, tok, tok, mat(BF16), mat(F32)]
    return pl.pallas_call(
        functools.partial(_rwkv_chunk_kernel, chunk=chunk, n_valid=n_valid),
        grid=(b // nb, t // tt),
        in_specs=[
            pl.BlockSpec((nb, tt, HR_WIDTH), tile),
            pl.BlockSpec((nb, 8, HR_WIDTH), lambda i, j: (i, jnp.maximum(j * (tt // 8) - 1, 0), 0)),
            pl.BlockSpec((nb, 1, HR_WIDTH), lambda i, j: (i, 0, 0)),
            pl.BlockSpec((1, HR_WIDTH), fixed),
            vec, vec, vec, vec, vec,
            pl.BlockSpec((DECAY_PAD, m), fixed),
            pl.BlockSpec((ICLR_PAD, m), fixed),
            pl.BlockSpec((GATE_PAD, m), fixed),
            pl.BlockSpec((m, m), fixed),
        ],
        out_specs=[o[0] for o in outs],
        out_shape=[o[1] for o in outs],
        compiler_params=_params("parallel", "arbitrary"),
        name="rwkv_chunk",
    )(hr, hr, prev0, lp["mu"], lp["w0"], lp["a0"], lp["k_k"], lp["k_a"], lp["r_k"],
      lp["w_decay"], lp["w_iclr"], lp["w_gate"], lp["bd512"])


def _rwkv_state_kernel(qc_ref, y0_ref, g_ref, bo_ref, m_ref, hh_ref, s0_ref, lng_ref, lnb_ref, bd_ref,
                       y_ref, sf_ref, s_ref, *, chunk, nsteps):
    j = pl.program_id(1)
    C = chunk
    nb, tt, _ = qc_ref.shape
    bd = bd_ref[...]

    @pl.when(j == 0)
    def _():
        s_ref[...] = s0_ref[...]

    cells = [(b, hp) for b in range(nb) for hp in range(N_PAIRS)]
    state = [s_ref[b, hp] for b, hp in cells]
    for ci in range(tt // C):
        rows = slice(ci * C, (ci + 1) * C)
        sb = [s.astype(BF16) for s in state]
        ys = [_dot_nt(qc_ref[b, rows, hp * PAIR:(hp + 1) * PAIR].astype(BF16), s)
              + y0_ref[b, rows, hp * PAIR:(hp + 1) * PAIR] for (b, hp), s in zip(cells, sb)]
        state = [_dot(s, m_ref[b, ci, hp]) + hh_ref[b, ci, hp] for (b, hp), s in zip(cells, sb)]
        means = [_group_sum(y, bd) * (1.0 / HEAD_DIM) for y in ys]
        yc = [y - mu for y, mu in zip(ys, means)]
        var = [_group_sum(y * y, bd) * (1.0 / HEAD_DIM) for y in yc]
        for (b, hp), y, vr in zip(cells, yc, var):
            sl = slice(hp * PAIR, (hp + 1) * PAIR)
            yn = y * lax.rsqrt(vr + GN_EPS) * lng_ref[:, sl] + lnb_ref[:, sl]
            y_ref[b, rows, sl] = (yn + bo_ref[b, rows, sl]) * g_ref[b, rows, sl]
    for (b, hp), s in zip(cells, state):
        s_ref[b, hp] = s

    @pl.when(j == nsteps - 1)
    def _():
        sf_ref[...] = s_ref[...]


def _rwkv_state(qc, y0, g, bonus, mm, hh, s0, lp, *, nb, tt, chunk):
    bsz, t, m = qc.shape
    gch = tt // chunk
    nsteps = t // tt
    tile = pl.BlockSpec((nb, tt, m), lambda i, j: (i, j, 0))
    mat = pl.BlockSpec((nb, gch, N_PAIRS, PAIR, PAIR), lambda i, j: (i, j, 0, 0, 0))
    st = pl.BlockSpec((nb, N_PAIRS, PAIR, PAIR), lambda i, j: (i, 0, 0, 0))
    vec = pl.BlockSpec((1, m), lambda i, j: (0, 0))
    return pl.pallas_call(
        functools.partial(_rwkv_state_kernel, chunk=chunk, nsteps=nsteps),
        grid=(bsz // nb, nsteps),
        in_specs=[tile, tile, tile, tile, mat, mat, st, vec, vec,
                  pl.BlockSpec((PAIR, PAIR), lambda i, j: (0, 0))],
        out_specs=[tile, st],
        out_shape=[jax.ShapeDtypeStruct((bsz, t, m), F32),
                   jax.ShapeDtypeStruct((bsz, N_PAIRS, PAIR, PAIR), F32)],
        scratch_shapes=[pltpu.VMEM((nb, N_PAIRS, PAIR, PAIR), F32)],
        compiler_params=_params("parallel", "arbitrary"),
        name="rwkv_state",
    )(qc, y0, g, bonus, mm, hh, s0, lp["lnx_g"], lp["lnx_b"], lp["bd128"])


def _bias_lane(head):
    return HEAD_DIM if head % 2 == 0 else 0


def _fox_bias_kernel(q_ref, k_ref, lf_ref, pk_ref, pq_ref, qa_ref, ka_ref, *, blk, nblk):
    ti = lax.broadcasted_iota(jnp.int32, (blk, blk), 0)
    tj = lax.broadcasted_iota(jnp.int32, (blk, blk), 1)
    tril01 = (ti >= tj).astype(BF16)
    lane = lax.broadcasted_iota(jnp.int32, (1, PAIR), 1)
    carry = jnp.zeros((1, LANES), F32)
    for i in range(nblk):
        sl = slice(i * blk, (i + 1) * blk)
        c = _dot_01x(tril01, lf_ref[0, sl, :]) + carry
        carry = c[blk - 1:blk, :]
        p1, p2, p3 = (jnp.where(lane < N_HEADS, x.astype(F32), 0.0) for x in _split3(c * LOG2E))
        pieces = (p1 + pltpu.roll(p2, N_HEADS, axis=1) + pltpu.roll(p3, 2 * N_HEADS, axis=1)).astype(BF16)
        for hp in range(N_PAIRS):
            ps = slice(hp * PAIR, (hp + 1) * PAIR)
            kp = k_ref[0, sl, ps]
            qp = q_ref[0, sl, ps] * (HEAD_DIM ** -0.5 * LOG2E)
            k_aug = _dot(pieces, pk_ref[hp])
            q_aug = _dot(pieces, pq_ref[hp])
            for half in range(2):
                h = 2 * hp + half
                own = (lane < HEAD_DIM) if half == 0 else (lane >= HEAD_DIM)
                off = lane - _bias_lane(h)
                hs = slice(h * PAIR, (h + 1) * PAIR)
                aug = slice(half * PAIR, (half + 1) * PAIR)
                ka_ref[0, sl, hs] = jnp.where(
                    own, kp, jnp.where((off >= 3) & (off < 6), 1.0, k_aug[:, aug])).astype(BF16)
                qa_ref[0, sl, hs] = jnp.where(
                    own, qp, jnp.where((off >= 0) & (off < 3), 1.0, q_aug[:, aug])).astype(BF16)


def _bias_placement():
    src = jnp.arange(LANES)
    piece, head = src // N_HEADS, src % N_HEADS
    dst = jnp.arange(2 * PAIR)
    pk, pq = [], []
    for hp in range(N_PAIRS):
        k_hit = jnp.zeros((LANES, 2 * PAIR), bool)
        q_hit = jnp.zeros((LANES, 2 * PAIR), bool)
        for half in range(2):
            h = 2 * hp + half
            mine = ((head == h) & (piece < 3))[:, None]
            lane0 = half * PAIR + _bias_lane(h)
            k_hit |= mine & (dst[None, :] == lane0 + piece[:, None])
            q_hit |= mine & (dst[None, :] == lane0 + 3 + piece[:, None])
        pk.append(jnp.where(k_hit, -1.0, 0.0))
        pq.append(jnp.where(q_hit, 1.0, 0.0))
    return jnp.stack(pk).astype(BF16), jnp.stack(pq).astype(BF16)


def _fox_bias(q, k, lf, place_k, place_q, *, blk):
    b, s, m = q.shape
    wide = N_HEADS * PAIR
    row = lambda width: pl.BlockSpec((1, s, width), lambda i: (i, 0, 0))
    place = pl.BlockSpec(place_k.shape, lambda i: (0, 0, 0))
    out = jax.ShapeDtypeStruct((b, s, wide), BF16)
    return pl.pallas_call(
        functools.partial(_fox_bias_kernel, blk=blk, nblk=s // blk),
        grid=(b,),
        in_specs=[row(m), row(m), row(LANES), place, place],
        out_specs=[row(wide), row(wide)],
        out_shape=[out, out],
        compiler_params=_params("parallel"),
        name="fox_bias",
    )(q, k, lf, place_k, place_q)


def _fox_prompt_kernel(qa_ref, ka_ref, vt_ref, gain_ref, o_ref, m_s, l_s, acc_s, st_s, *, blk):
    qi = pl.program_id(1)
    heads = range(N_HEADS)
    m_s[...] = jnp.full(m_s.shape, NEG_INF, F32)
    l_s[...] = jnp.zeros_like(l_s)
    acc_s[...] = jnp.zeros_like(acc_s)

    def keys(j):
        return pl.ds(pl.multiple_of(j * blk, blk), blk)

    def scores(j, slot, diagonal):
        ks = keys(j)
        st = [_dot_nt(ka_ref[0, ks, h * PAIR:(h + 1) * PAIR], qa_ref[0, :, h * PAIR:(h + 1) * PAIR])
              for h in heads]
        if diagonal:
            ki = lax.broadcasted_iota(jnp.int32, (blk, blk), 0)
            qj = lax.broadcasted_iota(jnp.int32, (blk, blk), 1)
            st = [jnp.where(ki <= qj, x, NEG_INF) for x in st]
        for h in heads:
            st_s[slot, h] = st[h]

    def consume(j, slot):
        ks = keys(j)
        st = [st_s[slot, h] for h in heads]
        m_old = [m_s[h:h + 1, :] for h in heads]
        m_new = [jnp.maximum(mo, jnp.max(x, axis=0, keepdims=True)) for mo, x in zip(m_old, st)]
        p = [jnp.exp2(x - mn) for x, mn in zip(st, m_new)]
        pv = [_dot(vt_ref[0, 0, h * HEAD_DIM:(h + 1) * HEAD_DIM, ks].astype(BF16), x.astype(BF16))
              for h, x in zip(heads, p)]
        for h in heads:
            alpha = jnp.exp2(m_old[h] - m_new[h])
            m_s[h:h + 1, :] = m_new[h]
            l_s[h:h + 1, :] = alpha * l_s[h:h + 1, :] + jnp.sum(p[h], axis=0, keepdims=True)
            rows = slice(h * HEAD_DIM, (h + 1) * HEAD_DIM)
            acc_s[rows, :] = alpha * acc_s[rows, :] + pv[h]

    scores(qi, 0, True)

    def body(u, carry):
        j = 2 * u
        scores(j, 1, False)
        consume(jnp.where(u == 0, qi, j - 1), 0)
        scores(j + 1, 0, False)
        consume(j, 1)
        return carry

    lax.fori_loop(0, qi // 2, body, 0)
    odd = (qi & 1) == 1

    @pl.when(odd)
    def _():
        scores(qi - 1, 1, False)
        consume(jnp.where(qi == 1, qi, qi - 2), 0)
        consume(qi - 1, 1)

    @pl.when(jnp.logical_not(odd))
    def _():
        consume(jnp.where(qi == 0, qi, qi - 1), 0)

    for hp in range(N_PAIRS):
        halves = []
        for h in (2 * hp, 2 * hp + 1):
            o = acc_s[h * HEAD_DIM:(h + 1) * HEAD_DIM, :] / l_s[h:h + 1, :]
            ms = jnp.mean(o * o, axis=0, keepdims=True)
            halves.append(o * lax.rsqrt(ms + RMS_EPS))
        sl = slice(hp * PAIR, (hp + 1) * PAIR)
        o_ref[0, :, sl] = jnp.concatenate(halves, axis=0).T * gain_ref[:, sl]


def _fox_prompt(qa, ka, vt_all, gain, *, layer, blk):
    b, s, wide = qa.shape
    m = MIX_HALF
    nb = s // blk
    return pl.pallas_call(
        functools.partial(_fox_prompt_kernel, blk=blk),
        grid=(b, nb),
        in_specs=[
            pl.BlockSpec((1, blk, wide), lambda i, j: (i, j, 0)),
            pl.BlockSpec((1, s, wide), lambda i, j: (i, 0, 0)),
            pl.BlockSpec((1, 1, m, s), lambda i, j: (layer, i, 0, 0)),
            pl.BlockSpec((1, m), lambda i, j: (0, 0)),
        ],
        out_specs=pl.BlockSpec((1, blk, m), lambda i, j: (i, j, 0)),
        out_shape=jax.ShapeDtypeStruct((b, s, m), F32),
        scratch_shapes=[
            pltpu.VMEM((N_HEADS, blk), F32),
            pltpu.VMEM((N_HEADS, blk), F32),
            pltpu.VMEM((m, blk), F32),
            pltpu.VMEM((2, N_HEADS, blk, blk), F32),
        ],
        compiler_params=_params("parallel", "arbitrary"),
        name="fox_prompt",
    )(qa, ka, vt_all, gain)


def _fox_decode_kernel(pt_ref, q_ref, kn_ref, vn_ref, lfn_ref, lfnt_ref, gain_ref, hm_ref, bd_ref, *rest,
                       npg, nsteps, tnew):
    k_pages = rest[0:npg]
    v_pages = rest[npg:2 * npg]
    lft_ref = rest[2 * npg]
    o_ref, m_s, l_s, acc_s, carry_s, qbd_s, cn_s = rest[2 * npg + 1:]
    seq = pl.program_id(0)
    g = pl.program_id(1)
    nrow = tnew * N_HEADS
    page = k_pages[0].shape[3]
    li = lax.broadcasted_iota(jnp.int32, (LANES, LANES), 0)
    lj = lax.broadcasted_iota(jnp.int32, (LANES, LANES), 1)
    hm = hm_ref[...]

    def tile_rows(x):
        return jnp.concatenate([x] * tnew, axis=0)

    @pl.when(g == 0)
    def _():
        q = q_ref[0] * (HEAD_DIM ** -0.5)
        qbd = jnp.concatenate([jnp.broadcast_to(q[t:t + 1, :], hm.shape) * hm for t in range(tnew)], axis=0)
        qbd_s[...] = qbd.astype(BF16)
        cn = lfn_ref[0]
        trow = lax.broadcasted_iota(jnp.int32, cn.shape, 0)
        sh = 1
        while sh < tnew:
            cn = cn + jnp.where(trow >= sh, pltpu.roll(cn, sh, axis=0), 0.0)
            sh *= 2
        hsel = (lax.broadcasted_iota(jnp.int32, (N_HEADS, LANES), 0)
                == lax.broadcasted_iota(jnp.int32, (N_HEADS, LANES), 1)).astype(F32)
        cn_rows = jnp.concatenate(
            [jnp.sum(jnp.broadcast_to(cn[t:t + 1, :], hsel.shape) * hsel, axis=1, keepdims=True)
             for t in range(tnew)], axis=0)
        cn_s[...] = cn_rows
        cnt = _dot_x01(lfnt_ref[0], (li <= lj).astype(BF16))
        pad = jnp.zeros((page - tnew, MIX_HALF), F32)
        kn = jnp.concatenate([kn_ref[0], pad], axis=0).astype(BF16)
        vn = jnp.concatenate([vn_ref[0], pad], axis=0).astype(BF16)
        s = _dot_nt(qbd_s[...], kn) + (cn_rows - tile_rows(cnt))
        rtok = lax.broadcasted_iota(jnp.int32, (nrow, LANES), 0) >> 3
        ktok = lax.broadcasted_iota(jnp.int32, (nrow, LANES), 1)
        s = jnp.where(ktok <= rtok, s, NEG_INF)
        m = jnp.max(s, axis=1, keepdims=True)
        p = jnp.exp(s - m)
        m_s[...] = m
        l_s[...] = jnp.sum(p, axis=1, keepdims=True)
        acc_s[...] = _dot(p.astype(BF16), vn)
        carry_s[...] = jnp.zeros_like(carry_s)

    slots = range(npg)
    nr = npg * N_HEADS
    lf_all = jnp.concatenate([lft_ref[0, pt_ref[seq, (nsteps - 1 - g) * npg + i]] for i in slots], axis=0)
    ri = lax.broadcasted_iota(jnp.int32, (nr, nr), 0)
    rj = lax.broadcasted_iota(jnp.int32, (nr, nr), 1)
    later_page = (((ri & (N_HEADS - 1)) == (rj & (N_HEADS - 1))) & (rj > ri)).astype(BF16)
    total = jnp.sum(lf_all, axis=1, keepdims=True)
    after = jnp.sum(_dot_01x(later_page, lf_all), axis=1, keepdims=True)
    carry = carry_s[...]
    bias_all = _dot_x01(lf_all, (li > lj).astype(BF16)) + (after + jnp.concatenate([carry] * npg, axis=0))
    carry_s[...] = carry + (after + total)[0:N_HEADS]
    qbd = qbd_s[...]
    cn_rows = cn_s[...]
    kb = [k_pages[i][0, 0].astype(BF16) for i in slots]
    sc = [_dot(qbd, x) for x in kb]
    s = jnp.concatenate([x + (tile_rows(bias_all[i * N_HEADS:(i + 1) * N_HEADS, :]) + cn_rows)
                         for i, x in zip(slots, sc)], axis=1)
    m_old = m_s[...]
    m_new = jnp.maximum(m_old, jnp.max(s, axis=1, keepdims=True))
    alpha = jnp.exp(m_old - m_new)
    p = jnp.exp(s - m_new)
    m_s[...] = m_new
    l_s[...] = alpha * l_s[...] + jnp.sum(p, axis=1, keepdims=True)
    vb = [v_pages[i][0, 0].astype(BF16) for i in slots]
    pv = [_dot_nt(p[:, i * page:(i + 1) * page].astype(BF16), x) for i, x in zip(slots, vb)]
    while len(pv) > 1:
        pv = [a + b for a, b in zip(pv[0::2], pv[1::2])] + ([pv[-1]] if len(pv) % 2 else [])
    acc_s[...] = alpha * acc_s[...] + pv[0]

    @pl.when(g == nsteps - 1)
    def _():
        o_rows = acc_s[...] / l_s[...]
        o = jnp.concatenate(
            [jnp.sum(o_rows[t * N_HEADS:(t + 1) * N_HEADS, :] * hm, axis=0, keepdims=True) for t in range(tnew)],
            axis=0)
        ms = _dot_x01(o * o, bd_ref[...]) * (1.0 / HEAD_DIM)
        o_ref[0] = o * lax.rsqrt(ms + RMS_EPS) * gain_ref[...]


def _fox_decode(layer, page_table, q, kn, vn, lfn, lfn_t, cache_k, cache_v, cache_lft, gain, hm, bd, *, npg):
    b, tnew, m = q.shape
    n_pages = page_table.shape[1]
    page = cache_k.shape[3]
    nsteps = n_pages // npg
    nrow = tnew * N_HEADS

    def tok(i, g, pt):
        return (i, 0, 0)

    def fixed(i, g, pt):
        return (0, 0)

    def page_map(slot):
        return lambda i, g, pt: (layer, pt[i, (nsteps - 1 - g) * npg + slot], 0, 0)

    kv_specs = [pl.BlockSpec((1, 1, m, page), page_map(s)) for s in range(npg)]
    n_pool = cache_lft.shape[1]
    lf_spec = pl.BlockSpec((1, n_pool, N_HEADS, page), lambda i, g, pt: (layer, 0, 0, 0))
    grid_spec = pltpu.PrefetchScalarGridSpec(
        num_scalar_prefetch=1,
        grid=(b, nsteps),
        in_specs=[
            pl.BlockSpec((1, tnew, m), tok),
            pl.BlockSpec((1, tnew, m), tok),
            pl.BlockSpec((1, tnew, m), tok),
            pl.BlockSpec((1, tnew, LANES), tok),
            pl.BlockSpec((1, N_HEADS, LANES), tok),
            pl.BlockSpec((1, m), fixed),
            pl.BlockSpec((N_HEADS, m), fixed),
            pl.BlockSpec((m, m), fixed),
        ] + kv_specs + kv_specs + [lf_spec],
        out_specs=pl.BlockSpec((1, tnew, m), tok),
        scratch_shapes=[
            pltpu.VMEM((nrow, 1), F32),
            pltpu.VMEM((nrow, 1), F32),
            pltpu.VMEM((nrow, m), F32),
            pltpu.VMEM((N_HEADS, 1), F32),
            pltpu.VMEM((nrow, m), BF16),
            pltpu.VMEM((nrow, 1), F32),
        ],
    )
    return pl.pallas_call(
        functools.partial(_fox_decode_kernel, npg=npg, nsteps=nsteps, tnew=tnew),
        grid_spec=grid_spec,
        out_shape=jax.ShapeDtypeStruct((b, tnew, m), F32),
        compiler_params=_params("parallel", "arbitrary"),
        name="fox_decode",
    )(page_table, q, kn, vn, lfn, lfn_t, gain, hm, bd,
      *([cache_k] * npg), *([cache_v] * npg), cache_lft)


def _out_proj_kernel(x_ref, yr_ref, yf_ref, wr_ref, wf_ref, o_ref):
    o_ref[...] = (x_ref[...] + _dot(yr_ref[...].astype(BF16), wr_ref[...])
                  + _dot(yf_ref[...].astype(BF16), wf_ref[...]))


def _out_proj(x, yr, yf, wr, wf, *, tm):
    n, d = x.shape
    m = yr.shape[1]
    row = lambda i: (i, 0)
    fixed = lambda i: (0, 0)
    return pl.pallas_call(
        _out_proj_kernel,
        grid=(n // tm,),
        in_specs=[pl.BlockSpec((tm, d), row), pl.BlockSpec((tm, m), row), pl.BlockSpec((tm, m), row),
                  pl.BlockSpec((m, d), fixed), pl.BlockSpec((m, d), fixed)],
        out_specs=pl.BlockSpec((tm, d), row),
        out_shape=jax.ShapeDtypeStruct((n, d), F32),
        compiler_params=_params("parallel"),
        name="out_proj",
    )(x, yr, yf, wr, wf)


def _ple_kernel(x_ref, p_ref, g_ref, wg_ref, wu_ref, fg_ref, o_ref, *, final):
    x = x_ref[...]
    gate = _sigmoid(_dot(_rms(x, g_ref[...]).astype(BF16), wg_ref[...]))
    y = x + gate * _dot(p_ref[...].astype(BF16), wu_ref[...])
    o_ref[...] = _rms(y, fg_ref[...]) if final else y


def _ple(x, p, g, wg, wu, fg, *, tm, final):
    n, d = x.shape
    pd = p.shape[1]
    row = lambda i: (i, 0)
    fixed = lambda i: (0, 0)
    return pl.pallas_call(
        functools.partial(_ple_kernel, final=final),
        grid=(n // tm,),
        in_specs=[pl.BlockSpec((tm, d), row), pl.BlockSpec((tm, pd), row), pl.BlockSpec((1, d), fixed),
                  pl.BlockSpec((d, d), fixed), pl.BlockSpec((pd, d), fixed), pl.BlockSpec((1, d), fixed)],
        out_specs=pl.BlockSpec((tm, d), row),
        out_shape=jax.ShapeDtypeStruct((n, d), F32),
        compiler_params=_params("parallel"),
        name="ple",
    )(x, p, g, wg, wu, fg)


def _pad_cols(w, width):
    return jnp.pad(w, [(0, 0)] * (w.ndim - 1) + [(0, width - w.shape[-1])])


def _pad_rows(w, height):
    return jnp.pad(w, [(0, 0)] * (w.ndim - 2) + [(0, height - w.shape[-2]), (0, 0)])


def _pack_hr(h):
    m = MIX_HALF
    o = 3 * m
    return jnp.concatenate([
        h[..., :o],
        _pad_cols(h[..., o:o + DECAY_RANK], DECAY_PAD),
        _pad_cols(h[..., o + DECAY_RANK:o + DECAY_RANK + ICLR_RANK], ICLR_PAD),
        _pad_cols(h[..., o + DECAY_RANK + ICLR_RANK:], GATE_PAD)], axis=-1)


def _unpack_hr(h):
    m = MIX_HALF
    o = 3 * m
    return jnp.concatenate([
        h[..., :o],
        h[..., o:o + DECAY_RANK],
        h[..., o + DECAY_PAD:o + DECAY_PAD + ICLR_RANK],
        h[..., o + DECAY_PAD + ICLR_PAD:o + DECAY_PAD + ICLR_PAD + GATE_RANK]], axis=-1)


def _pair_state(s):
    b = s.shape[0]
    s = s.reshape(b, N_PAIRS, 2, HEAD_DIM, HEAD_DIM)
    z = jnp.zeros_like(s[:, :, 0])
    top = jnp.concatenate([s[:, :, 0], z], axis=-1)
    bot = jnp.concatenate([z, s[:, :, 1]], axis=-1)
    return jnp.concatenate([top, bot], axis=-2)


def _unpair_state(s):
    b = s.shape[0]
    d = HEAD_DIM
    return jnp.stack([s[:, :, :d, :d], s[:, :, d:, d:]], axis=2).reshape(b, N_HEADS, d, d)


def _block_diag01(n):
    i = jnp.arange(n) // HEAD_DIM
    return (i[:, None] == i[None, :]).astype(BF16)


def _rwkv_mix(hr3, prev0, s0, lp, *, n_valid):
    bsz, t, _ = hr3.shape
    chunk = REC_CHUNK
    tt = min(t, REC_TILE)
    nb = max(1, min(bsz, REC_TILE // tt))
    qc, y0, g, bonus, mm, hh = _rwkv_chunk(hr3, prev0, lp, nb=nb, tt=tt, chunk=chunk, n_valid=n_valid)
    return _rwkv_state(qc, y0, g, bonus, mm, hh, s0, lp, nb=min(bsz, STATE_BATCH), tt=min(t, STATE_TILE),
                       chunk=chunk)


def kernel(x_prompt, x_sample, cache_k, cache_v, cache_logf, state_wkv, state_shift, page_table, p_prompt, p_sample, ffn1_norm, ffn1_w_gate, ffn1_w_up, ffn1_w_down, mix_norm, w_in, rwkv_mu, rwkv_w0, rwkv_w_decay, rwkv_a0, rwkv_w_iclr, rwkv_w_gate, rwkv_k_k, rwkv_k_a, rwkv_r_k, rwkv_lnx_g, rwkv_lnx_b, fox_b_f, fox_out_norm, w_out, ffn2_norm, ffn2_w_gate, ffn2_w_up, ffn2_w_down, ple_norm, ple_w_gate, ple_w_up, final_norm):
    depth = w_in.shape[0]
    bp, sp, d = x_prompt.shape
    bs, ts, _ = x_sample.shape
    m = MIX_HALF
    n_pool, page = cache_k.shape[1], cache_k.shape[2]
    npr, nsm = bp * sp, bs * ts
    pd = p_prompt.shape[-1]

    fox_cols = w_in[:, :, RWKV_PROJ:]
    w_pack = jnp.concatenate([
        fox_cols[:, :, :3 * m],
        _pack_hr(w_in[:, :, :RWKV_PROJ]),
        _pad_cols(fox_cols[:, :, 3 * m:], LANES)], axis=-1).astype(BF16)
    b_f = _pad_cols(fox_b_f, LANES)[:, None, :]
    bd512 = _block_diag01(m)
    bd128 = _block_diag01(PAIR)
    hm = (jnp.arange(m)[None, :] // HEAD_DIM == jnp.arange(N_HEADS)[:, None]).astype(F32)
    bf = lambda w: w.astype(BF16)
    f1g, f1u, f1d = ffn1_w_gate, ffn1_w_up, ffn1_w_down
    f2g, f2u, f2d = ffn2_w_gate, ffn2_w_up, ffn2_w_down
    wo = bf(w_out)
    pg, pu = bf(ple_w_gate), bf(ple_w_up)
    wdec = bf(_pad_rows(rwkv_w_decay, DECAY_PAD))
    wicl = bf(_pad_rows(rwkv_w_iclr, ICLR_PAD))
    wgat = bf(_pad_rows(rwkv_w_gate, GATE_PAD))
    mu = _pack_hr(rwkv_mu)
    cache_kt = jnp.transpose(cache_k, (0, 1, 3, 4, 2)).reshape(depth, n_pool, m, page)
    cache_vt = jnp.transpose(cache_v, (0, 1, 3, 4, 2)).reshape(depth, n_pool, m, page)
    cache_lft = jnp.swapaxes(cache_logf, 2, 3)
    fg = final_norm[None, :]
    place_k, place_q = _bias_placement()

    tm_p = 1024 if npr % 1024 == 0 else ATT_BLOCK
    tm_ffn = FFN_TILE if npr % FFN_TILE == 0 else tm_p
    stacked = None
    tf = 256
    blk = ATT_BLOCK
    chunk = REC_CHUNK
    npg = min(DEC_PAGES_PER_STEP, page_table.shape[1])

    xp = x_prompt.reshape(npr, d)
    xs = x_sample.reshape(nsm, d)
    outs = [[] for _ in range(10)]
    for l in range(depth):
        lp = dict(mu=mu[l][None], w0=rwkv_w0[l][None], a0=rwkv_a0[l][None], k_k=rwkv_k_k[l][None],
                  k_a=rwkv_k_a[l][None], r_k=rwkv_r_k[l].reshape(1, m), w_decay=wdec[l], w_iclr=wicl[l],
                  w_gate=wgat[l], bd512=bd512, bd128=bd128, lnx_g=rwkv_lnx_g[l][None],
                  lnx_b=rwkv_lnx_b[l][None])
        gain = fox_out_norm[l].reshape(1, m)
        last = l == depth - 1

        xp = _ffn(xp, ffn1_norm[l][None], f1g[l], f1u[l], f1d[l], tm=tm_ffn, tf=tf)
        q, k, kt_all, vt_all, hr, lf, lft_all = _proj(
            xp, mix_norm[l][None], w_pack[l], b_f[l], tm=blk, seq=sp, prompt=True,
            layer=l, depth=depth, stacked=stacked)
        stacked = (kt_all, vt_all, lft_all)
        hr3 = hr.reshape(bp, sp, HR_WIDTH)
        y_r, s_fin = _rwkv_mix(hr3, jnp.zeros((bp, 1, HR_WIDTH), F32),
                               jnp.zeros((bp, N_PAIRS, PAIR, PAIR), F32), lp, n_valid=sp)
        qa, ka = _fox_bias(q.reshape(bp, sp, m), k.reshape(bp, sp, m), lf.reshape(bp, sp, LANES),
                           place_k, place_q, blk=blk)
        y_f = _fox_prompt(qa, ka, vt_all, gain, layer=l, blk=blk)
        xp = _out_proj(xp, y_r.reshape(npr, m), y_f.reshape(npr, m), wo[l, :m], wo[l, m:], tm=tm_p)
        xp = _ffn(xp, ffn2_norm[l][None], f2g[l], f2u[l], f2d[l], tm=tm_ffn, tf=tf)
        xp = _ple(xp, p_prompt[l].reshape(npr, pd), ple_norm[l][None], pg[l], pu[l], fg, tm=tm_p, final=last)
        outs[3].append(_unpair_state(s_fin))
        outs[4].append(_unpack_hr(hr3[:, -1, :]))

        xs = _ffn(xs, ffn1_norm[l][None], f1g[l], f1u[l], f1d[l], tm=nsm, tf=tf)
        q, k, v, hr, lf = _proj(xs, mix_norm[l][None], w_pack[l], b_f[l],
                                tm=nsm, seq=nsm, prompt=False)
        hr3 = hr.reshape(bs, ts, HR_WIDTH)
        hr_pad = jnp.pad(hr3, ((0, 0), (0, chunk - ts), (0, 0)))
        y_r, s_fin = _rwkv_mix(hr_pad, _pack_hr(state_shift[l])[:, None, :], _pair_state(state_wkv[l]), lp,
                               n_valid=ts)
        lf3 = lf.reshape(bs, ts, LANES)
        lf_t = _pad_cols(jnp.swapaxes(lf3[:, :, :N_HEADS], 1, 2), LANES)
        y_f = _fox_decode(l, page_table, q.reshape(bs, ts, m), k.reshape(bs, ts, m), v.reshape(bs, ts, m),
                          lf3, lf_t, cache_kt, cache_vt, cache_lft, gain, hm, bd512, npg=npg)
        xs = _out_proj(xs, y_r[:, :ts].reshape(nsm, m), y_f.reshape(nsm, m), wo[l, :m], wo[l, m:], tm=nsm)
        xs = _ffn(xs, ffn2_norm[l][None], f2g[l], f2u[l], f2d[l], tm=nsm, tf=tf)
        xs = _ple(xs, p_sample[l].reshape(nsm, pd), ple_norm[l][None], pg[l], pu[l], fg, tm=nsm, final=last)
        outs[5].append(k.reshape(bs, ts, N_HEADS, HEAD_DIM))
        outs[6].append(v.reshape(bs, ts, N_HEADS, HEAD_DIM))
        outs[7].append(lf3[:, :, :N_HEADS])
        outs[8].append(_unpair_state(s_fin))
        outs[9].append(_unpack_hr(hr3[:, -1, :]))

    st = [jnp.stack(o) for o in outs[3:]]
    k_p, v_p = (jnp.transpose(t.reshape(depth, bp, N_HEADS, HEAD_DIM, sp), (0, 1, 4, 2, 3)) for t in stacked[:2])
    lf_p = jnp.swapaxes(stacked[2], 2, 3)
    return (xp.reshape(bp, sp, d), xs.reshape(bs, ts, d), k_p, v_p, lf_p, *st)
```

```python
import functools

import jax
import jax.numpy as jnp
from jax import lax
from jax.experimental import pallas as pl
from jax.experimental.pallas import tpu as pltpu

F32 = jnp.float32
BF16 = jnp.bfloat16

HEAD_DIM = 64
N_HEADS = 8
MIX_HALF = N_HEADS * HEAD_DIM
PAIR = 2 * HEAD_DIM
N_PAIRS = N_HEADS // 2
DECAY_RANK = 64
ICLR_RANK = 64
GATE_RANK = 160
RWKV_PROJ = 3 * MIX_HALF + DECAY_RANK + ICLR_RANK + GATE_RANK
LANES = 128
DECAY_PAD = 128
ICLR_PAD = 128
GATE_PAD = 256
HR_WIDTH = 3 * MIX_HALF + DECAY_PAD + ICLR_PAD + GATE_PAD
RMS_EPS = 1e-6
GN_EPS = 64e-5
NEG_INF = -1e30
LOG2E = 1.4426950408889634
VMEM_LIMIT = 56 * 1024 * 1024

REC_CHUNK = 64
REC_TILE = 256
STATE_TILE = 128
STATE_BATCH = 8
ATT_BLOCK = 256
FFN_TILE = 2048
DEC_PAGES_PER_STEP = 32


def _params(*sem):
    return pltpu.CompilerParams(dimension_semantics=sem, vmem_limit_bytes=VMEM_LIMIT)


def _dot(a, b):
    return jnp.dot(a, b, preferred_element_type=F32)


def _dot_nt(a, b):
    return lax.dot_general(a, b, (((1,), (1,)), ((), ())), preferred_element_type=F32)


def _split3(x):
    x1 = x.astype(BF16)
    r1 = x - x1.astype(F32)
    x2 = r1.astype(BF16)
    r2 = r1 - x2.astype(F32)
    return x1, x2, r2.astype(BF16)


def _dot_x01(x, m01):
    x1, x2, x3 = _split3(x)
    return _dot(x1, m01) + _dot(x2, m01) + _dot(x3, m01)


def _group_sum(x, m01):
    x1 = x.astype(BF16)
    return _dot(x1, m01) + _dot((x - x1.astype(F32)).astype(BF16), m01)


def _dot_01x(m01, x):
    x1, x2, x3 = _split3(x)
    return _dot(m01, x1) + _dot(m01, x2) + _dot(m01, x3)


def _rms(x, g):
    ms = jnp.mean(x * x, axis=-1, keepdims=True)
    return x * lax.rsqrt(ms + RMS_EPS) * g


def _sigmoid(x):
    return 1.0 / (1.0 + jnp.exp(-x))


def _softplus(z):
    return jnp.maximum(z, 0.0) + jnp.log(1.0 + jnp.exp(-jnp.abs(z)))


def _ffn_kernel(x_ref, g_ref, wg_ref, wu_ref, wd_ref, o_ref, xn_ref, *, nf):
    j = pl.program_id(1)

    @pl.when(j == 0)
    def _():
        xn_ref[...] = _rms(x_ref[...], g_ref[...]).astype(BF16)
        o_ref[...] = jnp.zeros_like(o_ref)

    xn = xn_ref[...]
    gate = _dot(xn, wg_ref[0].astype(BF16))
    up = _dot(xn, wu_ref[0].astype(BF16))
    h = (gate * _sigmoid(gate) * up).astype(BF16)
    o_ref[...] += _dot(h, wd_ref[0].astype(BF16))

    @pl.when(j == nf - 1)
    def _():
        o_ref[...] = x_ref[...] + 0.5 * o_ref[...]


def _ffn(x, g, wg, wu, wd, *, layer, tm, tf):
    n, d = x.shape
    f = wg.shape[2]
    nf = f // tf
    return pl.pallas_call(
        functools.partial(_ffn_kernel, nf=nf),
        grid=(n // tm, nf),
        in_specs=[
            pl.BlockSpec((tm, d), lambda i, j: (i, 0)),
            pl.BlockSpec((1, d), lambda i, j: (0, 0)),
            pl.BlockSpec((1, d, tf), lambda i, j: (layer, 0, j)),
            pl.BlockSpec((1, d, tf), lambda i, j: (layer, 0, j)),
            pl.BlockSpec((1, tf, d), lambda i, j: (layer, j, 0)),
        ],
        out_specs=pl.BlockSpec((tm, d), lambda i, j: (i, 0)),
        out_shape=jax.ShapeDtypeStruct((n, d), F32),
        scratch_shapes=[pltpu.VMEM((tm, d), BF16)],
        compiler_params=_params("parallel", "arbitrary"),
        name="ffn",
    )(x, g, wg, wu, wd)


def _proj_kernel(x_ref, g_ref, w_ref, bf_ref, *refs, prompt, n_alias):
    out_refs = refs[n_alias:]
    xn = _rms(x_ref[...], g_ref[...]).astype(BF16)
    h = MIX_HALF
    if prompt:
        q_ref, k_ref, kt_ref, vt_ref, hr_ref, lf_ref, lft_ref, v_ref = out_refs
    else:
        q_ref, k_ref, v_ref, hr_ref, lf_ref = out_refs
    q_ref[...] = _dot(xn, w_ref[:, 0:h])
    k_ref[...] = _dot(xn, w_ref[:, h:2 * h])
    v_ref[...] = _dot(xn, w_ref[:, 2 * h:3 * h])
    hr_ref[...] = _dot(xn, w_ref[:, 3 * h:3 * h + HR_WIDTH])
    lf_ref[...] = -_softplus(-(_dot(xn, w_ref[:, 3 * h + HR_WIDTH:]) + bf_ref[...]))
    if prompt:
        kt_ref[0, 0] = k_ref[...].T
        vt_ref[0, 0] = v_ref[...].T
        lft_ref[0, 0] = lf_ref[...].T[0:N_HEADS, :]


def _proj(x, g, w, bf, *, tm, seq, prompt, layer=0, depth=1, stacked=None):
    n, d = x.shape
    h = MIX_HALF
    nj = seq // tm
    row = lambda i: (i, 0)
    fixed = lambda i: (0, 0)
    trans = lambda i: (layer, i // nj, 0, i % nj)
    rows = lambda width: (pl.BlockSpec((tm, width), row), jax.ShapeDtypeStruct((n, width), F32))
    cols = lambda height: (pl.BlockSpec((1, 1, height, tm), trans),
                           jax.ShapeDtypeStruct((depth, n // seq, height, seq), F32))
    if prompt:
        outs = [rows(h), rows(h), cols(h), cols(h), rows(HR_WIDTH), rows(LANES), cols(N_HEADS)]
    else:
        outs = [rows(h), rows(h), rows(h), rows(HR_WIDTH), rows(LANES)]
    stacked = tuple(stacked or ())
    n_in = 4
    aliases = {n_in + i: o for i, o in enumerate((2, 3, 6)[:len(stacked)])}
    return pl.pallas_call(
        functools.partial(_proj_kernel, prompt=prompt, n_alias=len(stacked)),
        grid=(n // tm,),
        in_specs=[
            pl.BlockSpec((tm, d), row),
            pl.BlockSpec((1, d), fixed),
            pl.BlockSpec(w.shape, fixed),
            pl.BlockSpec((1, LANES), fixed),
        ] + [pl.BlockSpec(memory_space=pl.ANY)] * len(stacked),
        out_specs=[o[0] for o in outs],
        out_shape=[o[1] for o in outs],
        scratch_shapes=[pltpu.VMEM((tm, h), F32)] if prompt else [],
        input_output_aliases=aliases,
        compiler_params=_params("parallel"),
        name="proj",
    )(x, g, w, bf, *stacked)


def _rwkv_chunk_kernel(h_ref, hp_ref, p0_ref, mu_ref, w0_ref, a0_ref, kk_ref, ka_ref, rk_ref,
                       wd_ref, wi_ref, wg_ref, bd_ref,
                       qc_o, y0_o, g_o, bo_o, m_o, hh_o, *, chunk, n_valid):
    j = pl.program_id(1)
    C = chunk
    C2 = 2 * C
    nb, tt, _ = h_ref.shape
    m = MIX_HALF
    bd = bd_ref[...]

    def mix_inputs(b):
        h = h_ref[b]
        prev = jnp.where(j == 0, p0_ref[b], hp_ref[b, 7:8, :])
        trow = lax.broadcasted_iota(jnp.int32, h.shape, 0)
        shifted = jnp.where(trow == 0, prev, pltpu.roll(h, 1, axis=0))
        hs = h + mu_ref[...] * (shifted - h)
        r = hs[:, 0:m]
        k = hs[:, m:2 * m]
        v = hs[:, 2 * m:3 * m]
        o = 3 * m
        d_decay = hs[:, o:o + DECAY_PAD]
        d_iclr = hs[:, o + DECAY_PAD:o + DECAY_PAD + ICLR_PAD]
        d_gate = hs[:, o + DECAY_PAD + ICLR_PAD:]
        w_log = -_softplus(-(w0_ref[...] + _dot(jnp.tanh(d_decay).astype(BF16), wd_ref[...]))) - 0.5
        lw = -jnp.exp(w_log)
        a = _sigmoid(a0_ref[...] + _dot(d_iclr.astype(BF16), wi_ref[...]))
        g_o[b] = _dot(_sigmoid(d_gate).astype(BF16), wg_ref[...])
        kk = k * kk_ref[...]
        kk = kk / jnp.maximum(jnp.sqrt(_group_sum(kk * kk, bd)), 1e-12)
        k = k * (1.0 + (a - 1.0) * ka_ref[...])
        bo_o[b] = _group_sum(r * k * rk_ref[...], bd) * v
        na = -kk
        bb = kk * a
        if n_valid < tt:
            ok = lax.broadcasted_iota(jnp.int32, (tt, m), 0) < n_valid
            zero = lambda x: jnp.where(ok, x, 0.0)
            r, k, v, lw, na, bb = zero(r), zero(k), zero(v), zero(lw), zero(na), zero(bb)
        return r, k, v, lw, na, bb

    row = lax.broadcasted_iota(jnp.int32, (C2, C2), 0)
    col = lax.broadcasted_iota(jnp.int32, (C2, C2), 1)
    rt = row & (C - 1)
    ct = col & (C - 1)
    strict = rt > ct
    incl = rt >= ct
    eye = (row == col).astype(F32)
    ti = lax.broadcasted_iota(jnp.int32, (C, C), 0)
    tj = lax.broadcasted_iota(jnp.int32, (C, C), 1)
    tril01 = (ti >= tj).astype(BF16)
    lane_lo = lax.broadcasted_iota(jnp.int32, (1, PAIR), 1) < HEAD_DIM

    def stack(x):
        return jnp.concatenate([jnp.where(lane_lo, x, 0.0), jnp.where(lane_lo, 0.0, x)], axis=0)

    prob = []
    for b in range(nb):
        r, k, v, lw, na, bb = mix_inputs(b)
        for ci in range(tt // C):
            rows = slice(ci * C, (ci + 1) * C)
            lwc = lw[rows]
            cum = _dot_01x(tril01, lwc)
            cum_end = cum[C - 1:C, :]
            e_neg = jnp.exp(-cum)
            e_end = jnp.exp(cum_end - cum)
            at = na[rows] * jnp.exp(cum - lwc)
            rt_ = r[rows] * jnp.exp(cum)
            bt = bb[rows] * e_neg
            kt = k[rows] * e_neg
            bh = bb[rows] * e_end
            kh = k[rows] * e_end
            wc = jnp.exp(cum_end)
            for hp in range(N_PAIRS):
                sl = slice(hp * PAIR, (hp + 1) * PAIR)
                v2 = stack(v[rows, sl])
                prob.append(dict(
                    b=b, ci=ci, sl=sl, hp=hp, rq=rt_[:, sl], wc=wc[:, sl],
                    at=stack(at[:, sl]).astype(BF16), rt=stack(rt_[:, sl]).astype(BF16),
                    bt=stack(bt[:, sl]).astype(BF16), kt=stack(kt[:, sl]).astype(BF16),
                    bh=stack(bh[:, sl]).astype(BF16), kh=stack(kh[:, sl]).astype(BF16),
                    v=v2.astype(BF16), vt=v2.T.astype(BF16)))

    mm = [_dot_nt(jnp.concatenate([p["at"], p["rt"]], axis=0), jnp.concatenate([p["bt"], p["kt"]], axis=0))
          for p in prob]
    n_ab = [jnp.where(strict, x[0:C2, 0:C2], 0.0) for x in mm]
    a_kr = [jnp.concatenate([jnp.where(strict, x[0:C2, C2:], 0.0),
                             jnp.where(incl, x[C2:, C2:], 0.0)], axis=0).astype(BF16) for x in mm]
    a_rb = [jnp.where(incl, x[C2:, 0:C2], 0.0).astype(BF16) for x in mm]
    kv = [_dot(x, p["v"]) for x, p in zip(a_kr, prob)]
    vtk = [_dot(p["vt"], p["kh"]) for p in prob]

    t_inv = [eye + jnp.where((rt >> 1) == (ct >> 1), x, 0.0) for x in n_ab]
    lvl = 1
    while (1 << lvl) < C:
        sel = ((rt >> (lvl + 1)) == (ct >> (lvl + 1))) & ((rt >> lvl) != (ct >> lvl))
        off = [jnp.where(sel, x, 0.0).astype(BF16) for x in n_ab]
        tb = [x.astype(BF16) for x in t_inv]
        xx = [_dot(x, y).astype(BF16) for x, y in zip(off, tb)]
        t_inv = [x + _dot(y, z) for x, y, z in zip(t_inv, tb, xx)]
        lvl += 1

    pu = [_dot(t.astype(BF16), jnp.concatenate([p["at"], x[0:C2].astype(BF16)], axis=1))
          for t, p, x in zip(t_inv, prob, kv)]
    ab = [_dot(x, y.astype(BF16)) for x, y in zip(a_rb, pu)]
    gb = [_dot(x.T.astype(BF16), p["bh"]) for x, p in zip(pu, prob)]
    for p, x_ab, x_kv, x_gb, x_vtk in zip(prob, ab, kv, gb, vtk):
        b = p["b"]
        rows = slice(p["ci"] * C, (p["ci"] + 1) * C)
        qc_o[b, rows, p["sl"]] = p["rq"] + x_ab[0:C, 0:PAIR] + x_ab[C:, 0:PAIR]
        y0 = x_ab[:, PAIR:] + x_kv[C2:]
        y0_o[b, rows, p["sl"]] = y0[0:C] + y0[C:]
        m_o[b, p["ci"], p["hp"]] = (x_gb[0:PAIR] + eye * p["wc"]).astype(BF16)
        hh_o[b, p["ci"], p["hp"]] = x_gb[PAIR:] + x_vtk


def _rwkv_chunk(hr, prev0, lp, *, nb, tt, chunk, n_valid):
    b, t, _ = hr.shape
    m = MIX_HALF
    g = tt // chunk
    tile = lambda i, j: (i, j, 0)
    fixed = lambda i, j: (0, 0)
    vec = pl.BlockSpec((1, m), fixed)
    tok = (pl.BlockSpec((nb, tt, m), tile), jax.ShapeDtypeStruct((b, t, m), F32))
    mat = lambda dt: (pl.BlockSpec((nb, g, N_PAIRS, PAIR, PAIR), lambda i, j: (i, j, 0, 0, 0)),
                      jax.ShapeDtypeStruct((b, t // chunk, N_PAIRS, PAIR, PAIR), dt))
    outs = [tok, tok, tok, tok, mat(BF16), mat(F32)]
    return pl.pallas_call(
        functools.partial(_rwkv_chunk_kernel, chunk=chunk, n_valid=n_valid),
        grid=(b // nb, t // tt),
        in_specs=[
            pl.BlockSpec((nb, tt, HR_WIDTH), tile),
            pl.BlockSpec((nb, 8, HR_WIDTH), lambda i, j: (i, jnp.maximum(j * (tt // 8) - 1, 0), 0)),
            pl.BlockSpec((nb, 1, HR_WIDTH), lambda i, j: (i, 0, 0)),
            pl.BlockSpec((1, HR_WIDTH), fixed),
            vec, vec, vec, vec, vec,
            pl.BlockSpec((DECAY_PAD, m), fixed),
            pl.BlockSpec((ICLR_PAD, m), fixed),
            pl.BlockSpec((GATE_PAD, m), fixed),
            pl.BlockSpec((m, m), fixed),
        ],
        out_specs=[o[0] for o in outs],
        out_shape=[o[1] for o in outs],
        compiler_params=_params("parallel", "arbitrary"),
        name="rwkv_chunk",
    )(hr, hr, prev0, lp["mu"], lp["w0"], lp["a0"], lp["k_k"], lp["k_a"], lp["r_k"],
      lp["w_decay"], lp["w_iclr"], lp["w_gate"], lp["bd512"])


def _rwkv_state_kernel(qc_ref, y0_ref, g_ref, bo_ref, m_ref, hh_ref, s0_ref, lng_ref, lnb_ref, bd_ref,
                       y_ref, sf_ref, s_ref, *, chunk, nsteps):
    j = pl.program_id(1)
    C = chunk
    nb, tt, _ = qc_ref.shape
    bd = bd_ref[...]

    @pl.when(j == 0)
    def _():
        s_ref[...] = s0_ref[...]

    cells = [(b, hp) for b in range(nb) for hp in range(N_PAIRS)]
    state = [s_ref[b, hp] for b, hp in cells]
    for ci in range(tt // C):
        rows = slice(ci * C, (ci + 1) * C)
        sb = [s.astype(BF16) for s in state]
        ys = [_dot_nt(qc_ref[b, rows, hp * PAIR:(hp + 1) * PAIR].astype(BF16), s)
              + y0_ref[b, rows, hp * PAIR:(hp + 1) * PAIR] for (b, hp), s in zip(cells, sb)]
        state = [_dot(s, m_ref[b, ci, hp]) + hh_ref[b, ci, hp] for (b, hp), s in zip(cells, sb)]
        means = [_group_sum(y, bd) * (1.0 / HEAD_DIM) for y in ys]
        yc = [y - mu for y, mu in zip(ys, means)]
        var = [_group_sum(y * y, bd) * (1.0 / HEAD_DIM) for y in yc]
        for (b, hp), y, vr in zip(cells, yc, var):
            sl = slice(hp * PAIR, (hp + 1) * PAIR)
            yn = y * lax.rsqrt(vr + GN_EPS) * lng_ref[:, sl] + lnb_ref[:, sl]
            y_ref[b, rows, sl] = (yn + bo_ref[b, rows, sl]) * g_ref[b, rows, sl]
    for (b, hp), s in zip(cells, state):
        s_ref[b, hp] = s

    @pl.when(j == nsteps - 1)
    def _():
        sf_ref[...] = s_ref[...]


def _rwkv_state(qc, y0, g, bonus, mm, hh, s0, lp, *, nb, tt, chunk):
    bsz, t, m = qc.shape
    gch = tt // chunk
    nsteps = t // tt
    tile = pl.BlockSpec((nb, tt, m), lambda i, j: (i, j, 0))
    mat = pl.BlockSpec((nb, gch, N_PAIRS, PAIR, PAIR), lambda i, j: (i, j, 0, 0, 0))
    st = pl.BlockSpec((nb, N_PAIRS, PAIR, PAIR), lambda i, j: (i, 0, 0, 0))
    vec = pl.BlockSpec((1, m), lambda i, j: (0, 0))
    return pl.pallas_call(
        functools.partial(_rwkv_state_kernel, chunk=chunk, nsteps=nsteps),
        grid=(bsz // nb, nsteps),
        in_specs=[tile, tile, tile, tile, mat, mat, st, vec, vec,
                  pl.BlockSpec((PAIR, PAIR), lambda i, j: (0, 0))],
        out_specs=[tile, st],
        out_shape=[jax.ShapeDtypeStruct((bsz, t, m), F32),
                   jax.ShapeDtypeStruct((bsz, N_PAIRS, PAIR, PAIR), F32)],
        scratch_shapes=[pltpu.VMEM((nb, N_PAIRS, PAIR, PAIR), F32)],
        compiler_params=_params("parallel", "arbitrary"),
        name="rwkv_state",
    )(qc, y0, g, bonus, mm, hh, s0, lp["lnx_g"], lp["lnx_b"], lp["bd128"])


def _bias_lane(head):
    return HEAD_DIM if head % 2 == 0 else 0


def _fox_bias_kernel(q_ref, k_ref, lf_ref, pk_ref, pq_ref, qa_ref, ka_ref, *, blk, nblk):
    ti = lax.broadcasted_iota(jnp.int32, (blk, blk), 0)
    tj = lax.broadcasted_iota(jnp.int32, (blk, blk), 1)
    tril01 = (ti >= tj).astype(BF16)
    lane = lax.broadcasted_iota(jnp.int32, (1, PAIR), 1)
    carry = jnp.zeros((1, LANES), F32)
    for i in range(nblk):
        sl = slice(i * blk, (i + 1) * blk)
        c = _dot_01x(tril01, lf_ref[0, sl, :]) + carry
        carry = c[blk - 1:blk, :]
        p1, p2, p3 = (jnp.where(lane < N_HEADS, x.astype(F32), 0.0) for x in _split3(c * LOG2E))
        pieces = (p1 + pltpu.roll(p2, N_HEADS, axis=1) + pltpu.roll(p3, 2 * N_HEADS, axis=1)).astype(BF16)
        for hp in range(N_PAIRS):
            ps = slice(hp * PAIR, (hp + 1) * PAIR)
            kp = k_ref[0, sl, ps]
            qp = q_ref[0, sl, ps] * (HEAD_DIM ** -0.5 * LOG2E)
            k_aug = _dot(pieces, pk_ref[hp])
            q_aug = _dot(pieces, pq_ref[hp])
            for half in range(2):
                h = 2 * hp + half
                own = (lane < HEAD_DIM) if half == 0 else (lane >= HEAD_DIM)
                off = lane - _bias_lane(h)
                hs = slice(h * PAIR, (h + 1) * PAIR)
                aug = slice(half * PAIR, (half + 1) * PAIR)
                ka_ref[0, sl, hs] = jnp.where(
                    own, kp, jnp.where((off >= 3) & (off < 6), 1.0, k_aug[:, aug])).astype(BF16)
                qa_ref[0, sl, hs] = jnp.where(
                    own, qp, jnp.where((off >= 0) & (off < 3), 1.0, q_aug[:, aug])).astype(BF16)


def _bias_placement():
    src = jnp.arange(LANES)
    piece, head = src // N_HEADS, src % N_HEADS
    dst = jnp.arange(2 * PAIR)
    pk, pq = [], []
    for hp in range(N_PAIRS):
        k_hit = jnp.zeros((LANES, 2 * PAIR), bool)
        q_hit = jnp.zeros((LANES, 2 * PAIR), bool)
        for half in range(2):
            h = 2 * hp + half
            mine = ((head == h) & (piece < 3))[:, None]
            lane0 = half * PAIR + _bias_lane(h)
            k_hit |= mine & (dst[None, :] == lane0 + piece[:, None])
            q_hit |= mine & (dst[None, :] == lane0 + 3 + piece[:, None])
        pk.append(jnp.where(k_hit, -1.0, 0.0))
        pq.append(jnp.where(q_hit, 1.0, 0.0))
    return jnp.stack(pk).astype(BF16), jnp.stack(pq).astype(BF16)


def _fox_bias(q, k, lf, place_k, place_q, *, blk):
    b, s, m = q.shape
    wide = N_HEADS * PAIR
    row = lambda width: pl.BlockSpec((1, s, width), lambda i: (i, 0, 0))
    place = pl.BlockSpec(place_k.shape, lambda i: (0, 0, 0))
    out = jax.ShapeDtypeStruct((b, s, wide), BF16)
    return pl.pallas_call(
        functools.partial(_fox_bias_kernel, blk=blk, nblk=s // blk),
        grid=(b,),
        in_specs=[row(m), row(m), row(LANES), place, place],
        out_specs=[row(wide), row(wide)],
        out_shape=[out, out],
        compiler_params=_params("parallel"),
        name="fox_bias",
    )(q, k, lf, place_k, place_q)


def _fox_prompt_kernel(qa_ref, ka_ref, vt_ref, gain_ref, o_ref, m_s, l_s, acc_s, st_s, *, blk):
    qi = pl.program_id(1)
    heads = range(N_HEADS)
    m_s[...] = jnp.full(m_s.shape, NEG_INF, F32)
    l_s[...] = jnp.zeros_like(l_s)
    acc_s[...] = jnp.zeros_like(acc_s)

    def keys(j):
        return pl.ds(pl.multiple_of(j * blk, blk), blk)

    def scores(j, slot, diagonal):
        ks = keys(j)
        st = [_dot_nt(ka_ref[0, ks, h * PAIR:(h + 1) * PAIR], qa_ref[0, :, h * PAIR:(h + 1) * PAIR])
              for h in heads]
        if diagonal:
            ki = lax.broadcasted_iota(jnp.int32, (blk, blk), 0)
            qj = lax.broadcasted_iota(jnp.int32, (blk, blk), 1)
            st = [jnp.where(ki <= qj, x, NEG_INF) for x in st]
        for h in heads:
            st_s[slot, h] = st[h]

    def consume(j, slot):
        ks = keys(j)
        st = [st_s[slot, h] for h in heads]
        m_old = [m_s[h:h + 1, :] for h in heads]
        m_new = [jnp.maximum(mo, jnp.max(x, axis=0, keepdims=True)) for mo, x in zip(m_old, st)]
        p = [jnp.exp2(x - mn) for x, mn in zip(st, m_new)]
        pv = [_dot(vt_ref[0, 0, h * HEAD_DIM:(h + 1) * HEAD_DIM, ks].astype(BF16), x.astype(BF16))
              for h, x in zip(heads, p)]
        for h in heads:
            alpha = jnp.exp2(m_old[h] - m_new[h])
            m_s[h:h + 1, :] = m_new[h]
            l_s[h:h + 1, :] = alpha * l_s[h:h + 1, :] + jnp.sum(p[h], axis=0, keepdims=True)
            rows = slice(h * HEAD_DIM, (h + 1) * HEAD_DIM)
            acc_s[rows, :] = alpha * acc_s[rows, :] + pv[h]

    scores(qi, 0, True)

    def body(u, carry):
        j = 2 * u
        scores(j, 1, False)
        consume(jnp.where(u == 0, qi, j - 1), 0)
        scores(j + 1, 0, False)
        consume(j, 1)
        return carry

    lax.fori_loop(0, qi // 2, body, 0)
    odd = (qi & 1) == 1

    @pl.when(odd)
    def _():
        scores(qi - 1, 1, False)
        consume(jnp.where(qi == 1, qi, qi - 2), 0)
        consume(qi - 1, 1)

    @pl.when(jnp.logical_not(odd))
    def _():
        consume(jnp.where(qi == 0, qi, qi - 1), 0)

    for hp in range(N_PAIRS):
        halves = []
        for h in (2 * hp, 2 * hp + 1):
            o = acc_s[h * HEAD_DIM:(h + 1) * HEAD_DIM, :] / l_s[h:h + 1, :]
            ms = jnp.mean(o * o, axis=0, keepdims=True)
            halves.append(o * lax.rsqrt(ms + RMS_EPS))
        sl = slice(hp * PAIR, (hp + 1) * PAIR)
        o_ref[0, :, sl] = jnp.concatenate(halves, axis=0).T * gain_ref[:, sl]


def _fox_prompt(qa, ka, vt_all, gain, *, layer, blk):
    b, s, wide = qa.shape
    m = MIX_HALF
    nb = s // blk
    return pl.pallas_call(
        functools.partial(_fox_prompt_kernel, blk=blk),
        grid=(b, nb),
        in_specs=[
            pl.BlockSpec((1, blk, wide), lambda i, j: (i, j, 0)),
            pl.BlockSpec((1, s, wide), lambda i, j: (i, 0, 0)),
            pl.BlockSpec((1, 1, m, s), lambda i, j: (layer, i, 0, 0)),
            pl.BlockSpec((1, m), lambda i, j: (0, 0)),
        ],
        out_specs=pl.BlockSpec((1, blk, m), lambda i, j: (i, j, 0)),
        out_shape=jax.ShapeDtypeStruct((b, s, m), F32),
        scratch_shapes=[
            pltpu.VMEM((N_HEADS, blk), F32),
            pltpu.VMEM((N_HEADS, blk), F32),
            pltpu.VMEM((m, blk), F32),
            pltpu.VMEM((2, N_HEADS, blk, blk), F32),
        ],
        compiler_params=_params("parallel", "arbitrary"),
        name="fox_prompt",
    )(qa, ka, vt_all, gain)


def _fox_decode_kernel(pt_ref, q_ref, kn_ref, vn_ref, lfn_ref, lfnt_ref, gain_ref, hm_ref, bd_ref, *rest,
                       npg, nsteps, tnew):
    k_pages = rest[0:npg]
    v_pages = rest[npg:2 * npg]
    lft_ref = rest[2 * npg]
    o_ref, m_s, l_s, acc_s, carry_s, qbd_s, cn_s = rest[2 * npg + 1:]
    seq = pl.program_id(0)
    g = pl.program_id(1)
    nrow = tnew * N_HEADS
    page = k_pages[0].shape[3]
    li = lax.broadcasted_iota(jnp.int32, (LANES, LANES), 0)
    lj = lax.broadcasted_iota(jnp.int32, (LANES, LANES), 1)
    hm = hm_ref[...]

    def tile_rows(x):
        return jnp.concatenate([x] * tnew, axis=0)

    @pl.when(g == 0)
    def _():
        q = q_ref[0] * (HEAD_DIM ** -0.5)
        qbd = jnp.concatenate([jnp.broadcast_to(q[t:t + 1, :], hm.shape) * hm for t in range(tnew)], axis=0)
        qbd_s[...] = qbd.astype(BF16)
        cn = lfn_ref[0]
        trow = lax.broadcasted_iota(jnp.int32, cn.shape, 0)
        sh = 1
        while sh < tnew:
            cn = cn + jnp.where(trow >= sh, pltpu.roll(cn, sh, axis=0), 0.0)
            sh *= 2
        hsel = (lax.broadcasted_iota(jnp.int32, (N_HEADS, LANES), 0)
                == lax.broadcasted_iota(jnp.int32, (N_HEADS, LANES), 1)).astype(F32)
        cn_rows = jnp.concatenate(
            [jnp.sum(jnp.broadcast_to(cn[t:t + 1, :], hsel.shape) * hsel, axis=1, keepdims=True)
             for t in range(tnew)], axis=0)
        cn_s[...] = cn_rows
        cnt = _dot_x01(lfnt_ref[0], (li <= lj).astype(BF16))
        pad = jnp.zeros((page - tnew, MIX_HALF), F32)
        kn = jnp.concatenate([kn_ref[0], pad], axis=0).astype(BF16)
        vn = jnp.concatenate([vn_ref[0], pad], axis=0).astype(BF16)
        s = _dot_nt(qbd_s[...], kn) + (cn_rows - tile_rows(cnt))
        rtok = lax.broadcasted_iota(jnp.int32, (nrow, LANES), 0) >> 3
        ktok = lax.broadcasted_iota(jnp.int32, (nrow, LANES), 1)
        s = jnp.where(ktok <= rtok, s, NEG_INF)
        m = jnp.max(s, axis=1, keepdims=True)
        p = jnp.exp(s - m)
        m_s[...] = m
        l_s[...] = jnp.sum(p, axis=1, keepdims=True)
        acc_s[...] = _dot(p.astype(BF16), vn)
        carry_s[...] = jnp.zeros_like(carry_s)

    slots = range(npg)
    nr = npg * N_HEADS
    lf_all = jnp.concatenate([lft_ref[0, pt_ref[seq, (nsteps - 1 - g) * npg + i]] for i in slots], axis=0)
    ri = lax.broadcasted_iota(jnp.int32, (nr, nr), 0)
    rj = lax.broadcasted_iota(jnp.int32, (nr, nr), 1)
    later_page = (((ri & (N_HEADS - 1)) == (rj & (N_HEADS - 1))) & (rj > ri)).astype(BF16)
    total = jnp.sum(lf_all, axis=1, keepdims=True)
    after = jnp.sum(_dot_01x(later_page, lf_all), axis=1, keepdims=True)
    carry = carry_s[...]
    bias_all = _dot_x01(lf_all, (li > lj).astype(BF16)) + (after + jnp.concatenate([carry] * npg, axis=0))
    carry_s[...] = carry + (after + total)[0:N_HEADS]
    qbd = qbd_s[...]
    cn_rows = cn_s[...]
    kb = [k_pages[i][0, 0].astype(BF16) for i in slots]
    sc = [_dot(qbd, x) for x in kb]
    s = jnp.concatenate([x + (tile_rows(bias_all[i * N_HEADS:(i + 1) * N_HEADS, :]) + cn_rows)
                         for i, x in zip(slots, sc)], axis=1)
    m_old = m_s[...]
    m_new = jnp.maximum(m_old, jnp.max(s, axis=1, keepdims=True))
    alpha = jnp.exp(m_old - m_new)
    p = jnp.exp(s - m_new)
    m_s[...] = m_new
    l_s[...] = alpha * l_s[...] + jnp.sum(p, axis=1, keepdims=True)
    vb = [v_pages[i][0, 0].astype(BF16) for i in slots]
    pv = [_dot_nt(p[:, i * page:(i + 1) * page].astype(BF16), x) for i, x in zip(slots, vb)]
    while len(pv) > 1:
        pv = [a + b for a, b in zip(pv[0::2], pv[1::2])] + ([pv[-1]] if len(pv) % 2 else [])
    acc_s[...] = alpha * acc_s[...] + pv[0]

    @pl.when(g == nsteps - 1)
    def _():
        o_rows = acc_s[...] / l_s[...]
        o = jnp.concatenate(
            [jnp.sum(o_rows[t * N_HEADS:(t + 1) * N_HEADS, :] * hm, axis=0, keepdims=True) for t in range(tnew)],
            axis=0)
        ms = _dot_x01(o * o, bd_ref[...]) * (1.0 / HEAD_DIM)
        o_ref[0] = o * lax.rsqrt(ms + RMS_EPS) * gain_ref[...]


def _fox_decode(layer, page_table, q, kn, vn, lfn, lfn_t, cache_k, cache_v, cache_lft, gain, hm, bd, *, npg):
    b, tnew, m = q.shape
    n_pages = page_table.shape[1]
    page = cache_k.shape[3]
    nsteps = n_pages // npg
    nrow = tnew * N_HEADS

    def tok(i, g, pt):
        return (i, 0, 0)

    def fixed(i, g, pt):
        return (0, 0)

    def page_map(slot):
        return lambda i, g, pt: (layer, pt[i, (nsteps - 1 - g) * npg + slot], 0, 0)

    kv_specs = [pl.BlockSpec((1, 1, m, page), page_map(s)) for s in range(npg)]
    n_pool = cache_lft.shape[1]
    lf_spec = pl.BlockSpec((1, n_pool, N_HEADS, page), lambda i, g, pt: (layer, 0, 0, 0),
                           pipeline_mode=pl.Buffered(1))
    grid_spec = pltpu.PrefetchScalarGridSpec(
        num_scalar_prefetch=1,
        grid=(b, nsteps),
        in_specs=[
            pl.BlockSpec((1, tnew, m), tok),
            pl.BlockSpec((1, tnew, m), tok),
            pl.BlockSpec((1, tnew, m), tok),
            pl.BlockSpec((1, tnew, LANES), tok),
            pl.BlockSpec((1, N_HEADS, LANES), tok),
            pl.BlockSpec((1, m), fixed),
            pl.BlockSpec((N_HEADS, m), fixed),
            pl.BlockSpec((m, m), fixed),
        ] + kv_specs + kv_specs + [lf_spec],
        out_specs=pl.BlockSpec((1, tnew, m), tok),
        scratch_shapes=[
            pltpu.VMEM((nrow, 1), F32),
            pltpu.VMEM((nrow, 1), F32),
            pltpu.VMEM((nrow, m), F32),
            pltpu.VMEM((N_HEADS, 1), F32),
            pltpu.VMEM((nrow, m), BF16),
            pltpu.VMEM((nrow, 1), F32),
        ],
    )
    return pl.pallas_call(
        functools.partial(_fox_decode_kernel, npg=npg, nsteps=nsteps, tnew=tnew),
        grid_spec=grid_spec,
        out_shape=jax.ShapeDtypeStruct((b, tnew, m), F32),
        compiler_params=_params("parallel", "arbitrary"),
        name="fox_decode",
    )(page_table, q, kn, vn, lfn, lfn_t, gain, hm, bd,
      *([cache_k] * npg), *([cache_v] * npg), cache_lft)


def _out_proj_kernel(x_ref, yr_ref, yf_ref, wr_ref, wf_ref, o_ref):
    o_ref[...] = (x_ref[...] + _dot(yr_ref[...].astype(BF16), wr_ref[...])
                  + _dot(yf_ref[...].astype(BF16), wf_ref[...]))


def _out_proj(x, yr, yf, wr, wf, *, tm):
    n, d = x.shape
    m = yr.shape[1]
    row = lambda i: (i, 0)
    fixed = lambda i: (0, 0)
    return pl.pallas_call(
        _out_proj_kernel,
        grid=(n // tm,),
        in_specs=[pl.BlockSpec((tm, d), row), pl.BlockSpec((tm, m), row), pl.BlockSpec((tm, m), row),
                  pl.BlockSpec((m, d), fixed), pl.BlockSpec((m, d), fixed)],
        out_specs=pl.BlockSpec((tm, d), row),
        out_shape=jax.ShapeDtypeStruct((n, d), F32),
        compiler_params=_params("parallel"),
        name="out_proj",
    )(x, yr, yf, wr, wf)


def _ple_kernel(x_ref, p_ref, g_ref, wg_ref, wu_ref, fg_ref, o_ref, *, final):
    x = x_ref[...]
    gate = _sigmoid(_dot(_rms(x, g_ref[...]).astype(BF16), wg_ref[...]))
    y = x + gate * _dot(p_ref[...].astype(BF16), wu_ref[...])
    o_ref[...] = _rms(y, fg_ref[...]) if final else y


def _ple(x, p, g, wg, wu, fg, *, tm, final):
    n, d = x.shape
    pd = p.shape[1]
    row = lambda i: (i, 0)
    fixed = lambda i: (0, 0)
    return pl.pallas_call(
        functools.partial(_ple_kernel, final=final),
        grid=(n // tm,),
        in_specs=[pl.BlockSpec((tm, d), row), pl.BlockSpec((tm, pd), row), pl.BlockSpec((1, d), fixed),
                  pl.BlockSpec((d, d), fixed), pl.BlockSpec((pd, d), fixed), pl.BlockSpec((1, d), fixed)],
        out_specs=pl.BlockSpec((tm, d), row),
        out_shape=jax.ShapeDtypeStruct((n, d), F32),
        compiler_params=_params("parallel"),
        name="ple",
    )(x, p, g, wg, wu, fg)


def _pad_cols(w, width):
    return jnp.pad(w, [(0, 0)] * (w.ndim - 1) + [(0, width - w.shape[-1])])


def _pad_rows(w, height):
    return jnp.pad(w, [(0, 0)] * (w.ndim - 2) + [(0, height - w.shape[-2]), (0, 0)])


def _pack_hr(h):
    m = MIX_HALF
    o = 3 * m
    return jnp.concatenate([
        h[..., :o],
        _pad_cols(h[..., o:o + DECAY_RANK], DECAY_PAD),
        _pad_cols(h[..., o + DECAY_RANK:o + DECAY_RANK + ICLR_RANK], ICLR_PAD),
        _pad_cols(h[..., o + DECAY_RANK + ICLR_RANK:], GATE_PAD)], axis=-1)


def _unpack_hr(h):
    m = MIX_HALF
    o = 3 * m
    return jnp.concatenate([
        h[..., :o],
        h[..., o:o + DECAY_RANK],
        h[..., o + DECAY_PAD:o + DECAY_PAD + ICLR_RANK],
        h[..., o + DECAY_PAD + ICLR_PAD:o + DECAY_PAD + ICLR_PAD + GATE_RANK]], axis=-1)


def _pair_state(s):
    b = s.shape[0]
    s = s.reshape(b, N_PAIRS, 2, HEAD_DIM, HEAD_DIM)
    z = jnp.zeros_like(s[:, :, 0])
    top = jnp.concatenate([s[:, :, 0], z], axis=-1)
    bot = jnp.concatenate([z, s[:, :, 1]], axis=-1)
    return jnp.concatenate([top, bot], axis=-2)


def _unpair_state(s):
    b = s.shape[0]
    d = HEAD_DIM
    return jnp.stack([s[:, :, :d, :d], s[:, :, d:, d:]], axis=2).reshape(b, N_HEADS, d, d)


def _block_diag01(n):
    i = jnp.arange(n) // HEAD_DIM
    return (i[:, None] == i[None, :]).astype(BF16)


def _rwkv_mix(hr3, prev0, s0, lp, *, n_valid):
    bsz, t, _ = hr3.shape
    chunk = REC_CHUNK
    tt = min(t, REC_TILE)
    nb = max(1, min(bsz, REC_TILE // tt))
    qc, y0, g, bonus, mm, hh = _rwkv_chunk(hr3, prev0, lp, nb=nb, tt=tt, chunk=chunk, n_valid=n_valid)
    return _rwkv_state(qc, y0, g, bonus, mm, hh, s0, lp, nb=min(bsz, STATE_BATCH), tt=min(t, STATE_TILE),
                       chunk=chunk)


def kernel(x_prompt, x_sample, cache_k, cache_v, cache_logf, state_wkv, state_shift, page_table, p_prompt, p_sample, ffn1_norm, ffn1_w_gate, ffn1_w_up, ffn1_w_down, mix_norm, w_in, rwkv_mu, rwkv_w0, rwkv_w_decay, rwkv_a0, rwkv_w_iclr, rwkv_w_gate, rwkv_k_k, rwkv_k_a, rwkv_r_k, rwkv_lnx_g, rwkv_lnx_b, fox_b_f, fox_out_norm, w_out, ffn2_norm, ffn2_w_gate, ffn2_w_up, ffn2_w_down, ple_norm, ple_w_gate, ple_w_up, final_norm):
    depth = w_in.shape[0]
    bp, sp, d = x_prompt.shape
    bs, ts, _ = x_sample.shape
    m = MIX_HALF
    n_pool, page = cache_k.shape[1], cache_k.shape[2]
    npr, nsm = bp * sp, bs * ts
    pd = p_prompt.shape[-1]

    fox_cols = w_in[:, :, RWKV_PROJ:]
    w_pack = jnp.concatenate([
        fox_cols[:, :, :3 * m],
        _pack_hr(w_in[:, :, :RWKV_PROJ]),
        _pad_cols(fox_cols[:, :, 3 * m:], LANES)], axis=-1).astype(BF16)
    b_f = _pad_cols(fox_b_f, LANES)[:, None, :]
    bd512 = _block_diag01(m)
    bd128 = _block_diag01(PAIR)
    hm = (jnp.arange(m)[None, :] // HEAD_DIM == jnp.arange(N_HEADS)[:, None]).astype(F32)
    bf = lambda w: w.astype(BF16)
    f1g, f1u, f1d = ffn1_w_gate, ffn1_w_up, ffn1_w_down
    f2g, f2u, f2d = ffn2_w_gate, ffn2_w_up, ffn2_w_down
    wo = bf(w_out)
    pg, pu = bf(ple_w_gate), bf(ple_w_up)
    wdec = bf(_pad_rows(rwkv_w_decay, DECAY_PAD))
    wicl = bf(_pad_rows(rwkv_w_iclr, ICLR_PAD))
    wgat = bf(_pad_rows(rwkv_w_gate, GATE_PAD))
    mu = _pack_hr(rwkv_mu)
    cache_kt = jnp.transpose(cache_k, (0, 1, 3, 4, 2)).reshape(depth, n_pool, m, page)
    cache_vt = jnp.transpose(cache_v, (0, 1, 3, 4, 2)).reshape(depth, n_pool, m, page)
    cache_lft = jnp.swapaxes(cache_logf, 2, 3)
    fg = final_norm[None, :]
    place_k, place_q = _bias_placement()

    tm_p = 1024 if npr % 1024 == 0 else ATT_BLOCK
    tm_ffn = FFN_TILE if npr % FFN_TILE == 0 else tm_p
    stacked = None
    tf = 256
    blk = ATT_BLOCK
    chunk = REC_CHUNK
    npg = min(DEC_PAGES_PER_STEP, page_table.shape[1])

    xp = x_prompt.reshape(npr, d)
    xs = x_sample.reshape(nsm, d)
    outs = [[] for _ in range(10)]
    for l in range(depth):
        lp = dict(mu=mu[l][None], w0=rwkv_w0[l][None], a0=rwkv_a0[l][None], k_k=rwkv_k_k[l][None],
                  k_a=rwkv_k_a[l][None], r_k=rwkv_r_k[l].reshape(1, m), w_decay=wdec[l], w_iclr=wicl[l],
                  w_gate=wgat[l], bd512=bd512, bd128=bd128, lnx_g=rwkv_lnx_g[l][None],
                  lnx_b=rwkv_lnx_b[l][None])
        gain = fox_out_norm[l].reshape(1, m)
        last = l == depth - 1

        xp = _ffn(xp, ffn1_norm[l][None], f1g, f1u, f1d, layer=l, tm=tm_ffn, tf=tf)
        q, k, kt_all, vt_all, hr, lf, lft_all = _proj(
            xp, mix_norm[l][None], w_pack[l], b_f[l], tm=blk, seq=sp, prompt=True,
            layer=l, depth=depth, stacked=stacked)
        stacked = (kt_all, vt_all, lft_all)
        hr3 = hr.reshape(bp, sp, HR_WIDTH)
        y_r, s_fin = _rwkv_mix(hr3, jnp.zeros((bp, 1, HR_WIDTH), F32),
                               jnp.zeros((bp, N_PAIRS, PAIR, PAIR), F32), lp, n_valid=sp)
        qa, ka = _fox_bias(q.reshape(bp, sp, m), k.reshape(bp, sp, m), lf.reshape(bp, sp, LANES),
                           place_k, place_q, blk=blk)
        y_f = _fox_prompt(qa, ka, vt_all, gain, layer=l, blk=blk)
        xp = _out_proj(xp, y_r.reshape(npr, m), y_f.reshape(npr, m), wo[l, :m], wo[l, m:], tm=tm_p)
        xp = _ffn(xp, ffn2_norm[l][None], f2g, f2u, f2d, layer=l, tm=tm_ffn, tf=tf)
        xp = _ple(xp, p_prompt[l].reshape(npr, pd), ple_norm[l][None], pg[l], pu[l], fg, tm=tm_p, final=last)
        outs[3].append(_unpair_state(s_fin))
        outs[4].append(_unpack_hr(hr3[:, -1, :]))

        xs = _ffn(xs, ffn1_norm[l][None], f1g, f1u, f1d, layer=l, tm=nsm, tf=tf)
        q, k, v, hr, lf = _proj(xs, mix_norm[l][None], w_pack[l], b_f[l],
                                tm=nsm, seq=nsm, prompt=False)
        hr3 = hr.reshape(bs, ts, HR_WIDTH)
        hr_pad = jnp.pad(hr3, ((0, 0), (0, chunk - ts), (0, 0)))
        y_r, s_fin = _rwkv_mix(hr_pad, _pack_hr(state_shift[l])[:, None, :], _pair_state(state_wkv[l]), lp,
                               n_valid=ts)
        lf3 = lf.reshape(bs, ts, LANES)
        lf_t = _pad_cols(jnp.swapaxes(lf3[:, :, :N_HEADS], 1, 2), LANES)
        y_f = _fox_decode(l, page_table, q.reshape(bs, ts, m), k.reshape(bs, ts, m), v.reshape(bs, ts, m),
                          lf3, lf_t, cache_kt, cache_vt, cache_lft, gain, hm, bd512, npg=npg)
        xs = _out_proj(xs, y_r[:, :ts].reshape(nsm, m), y_f.reshape(nsm, m), wo[l, :m], wo[l, m:], tm=nsm)
        xs = _ffn(xs, ffn2_norm[l][None], f2g, f2u, f2d, layer=l, tm=nsm, tf=tf)
        xs = _ple(xs, p_sample[l].reshape(nsm, pd), ple_norm[l][None], pg[l], pu[l], fg, tm=nsm, final=last)
        outs[5].append(k.reshape(bs, ts, N_HEADS, HEAD_DIM))
        outs[6].append(v.reshape(bs, ts, N_HEADS, HEAD_DIM))
        outs[7].append(lf3[:, :, :N_HEADS])
        outs[8].append(_unpair_state(s_fin))
        outs[9].append(_unpack_hr(hr3[:, -1, :]))

    st = [jnp.stack(o) for o in outs[3:]]
    k_p, v_p = (jnp.transpose(t.reshape(depth, bp, N_HEADS, HEAD_DIM, sp), (0, 1, 4, 2, 3)) for t in stacked[:2])
    lf_p = jnp.swapaxes(stacked[2], 2, 3)
    return (xp.reshape(bp, sp, d), xs.reshape(bs, ts, d), k_p, v_p, lf_p, *st)
```

```python
import functools

import jax
import jax.numpy as jnp
from jax import lax
from jax.experimental import pallas as pl
from jax.experimental.pallas import tpu as pltpu

F32 = jnp.float32
BF16 = jnp.bfloat16

HEAD_DIM = 64
N_HEADS = 8
MIX_HALF = N_HEADS * HEAD_DIM
PAIR = 2 * HEAD_DIM
N_PAIRS = N_HEADS // 2
DECAY_RANK = 64
ICLR_RANK = 64
GATE_RANK = 160
RWKV_PROJ = 3 * MIX_HALF + DECAY_RANK + ICLR_RANK + GATE_RANK
LANES = 128
DECAY_PAD = 128
ICLR_PAD = 128
GATE_PAD = 256
HR_WIDTH = 3 * MIX_HALF + DECAY_PAD + ICLR_PAD + GATE_PAD
RMS_EPS = 1e-6
GN_EPS = 64e-5
NEG_INF = -1e30
LOG2E = 1.4426950408889634
VMEM_LIMIT = 56 * 1024 * 1024

REC_CHUNK = 64
REC_TILE = 256
STATE_TILE = 128
STATE_BATCH = 8
ATT_BLOCK = 256
FFN_TILE = 2048
DEC_PAGES_PER_STEP = 32


def _params(*sem):
    return pltpu.CompilerParams(dimension_semantics=sem, vmem_limit_bytes=VMEM_LIMIT)


def _dot(a, b):
    return jnp.dot(a, b, preferred_element_type=F32)


def _dot_nt(a, b):
    return lax.dot_general(a, b, (((1,), (1,)), ((), ())), preferred_element_type=F32)


def _split3(x):
    x1 = x.astype(BF16)
    r1 = x - x1.astype(F32)
    x2 = r1.astype(BF16)
    r2 = r1 - x2.astype(F32)
    return x1, x2, r2.astype(BF16)


def _dot_x01(x, m01):
    x1, x2, x3 = _split3(x)
    return _dot(x1, m01) + _dot(x2, m01) + _dot(x3, m01)


def _group_sum(x, m01):
    x1 = x.astype(BF16)
    return _dot(x1, m01) + _dot((x - x1.astype(F32)).astype(BF16), m01)


def _dot_01x(m01, x):
    x1, x2, x3 = _split3(x)
    return _dot(m01, x1) + _dot(m01, x2) + _dot(m01, x3)


def _rms(x, g):
    ms = jnp.mean(x * x, axis=-1, keepdims=True)
    return x * lax.rsqrt(ms + RMS_EPS) * g


def _sigmoid(x):
    return 1.0 / (1.0 + jnp.exp(-x))


def _softplus(z):
    return jnp.maximum(z, 0.0) + jnp.log(1.0 + jnp.exp(-jnp.abs(z)))


def _ffn_kernel(x_ref, g_ref, wg_ref, wu_ref, wd_ref, o_ref, xn_ref, *, nf):
    j = pl.program_id(1)

    @pl.when(j == 0)
    def _():
        xn_ref[...] = _rms(x_ref[...], g_ref[...]).astype(BF16)
        o_ref[...] = jnp.zeros_like(o_ref)

    xn = xn_ref[...]
    gate = _dot(xn, wg_ref[0].astype(BF16))
    up = _dot(xn, wu_ref[0].astype(BF16))
    h = (gate * _sigmoid(gate) * up).astype(BF16)
    o_ref[...] += _dot(h, wd_ref[0].astype(BF16))

    @pl.when(j == nf - 1)
    def _():
        o_ref[...] = x_ref[...] + 0.5 * o_ref[...]


def _ffn(x, g, wg, wu, wd, *, layer, tm, tf):
    n, d = x.shape
    f = wg.shape[2]
    nf = f // tf
    return pl.pallas_call(
        functools.partial(_ffn_kernel, nf=nf),
        grid=(n // tm, nf),
        in_specs=[
            pl.BlockSpec((tm, d), lambda i, j: (i, 0)),
            pl.BlockSpec((1, d), lambda i, j: (0, 0)),
            pl.BlockSpec((1, d, tf), lambda i, j: (layer, 0, j)),
            pl.BlockSpec((1, d, tf), lambda i, j: (layer, 0, j)),
            pl.BlockSpec((1, tf, d), lambda i, j: (layer, j, 0)),
        ],
        out_specs=pl.BlockSpec((tm, d), lambda i, j: (i, 0)),
        out_shape=jax.ShapeDtypeStruct((n, d), F32),
        scratch_shapes=[pltpu.VMEM((tm, d), BF16)],
        compiler_params=_params("parallel", "arbitrary"),
        name="ffn",
    )(x, g, wg, wu, wd)


def _proj_kernel(x_ref, g_ref, w_ref, bf_ref, *refs, prompt, n_alias):
    out_refs = refs[n_alias:]
    xn = _rms(x_ref[...], g_ref[...]).astype(BF16)
    h = MIX_HALF
    if prompt:
        q_ref, k_ref, kt_ref, vt_ref, hr_ref, lf_ref, lft_ref, v_ref = out_refs
    else:
        q_ref, k_ref, v_ref, hr_ref, lf_ref = out_refs
    q_ref[...] = _dot(xn, w_ref[0, :, 0:h])
    k_ref[...] = _dot(xn, w_ref[0, :, h:2 * h])
    v_ref[...] = _dot(xn, w_ref[0, :, 2 * h:3 * h])
    hr_ref[...] = _dot(xn, w_ref[0, :, 3 * h:3 * h + HR_WIDTH])
    lf_ref[...] = -_softplus(-(_dot(xn, w_ref[0, :, 3 * h + HR_WIDTH:]) + bf_ref[...]))
    if prompt:
        kt_ref[0, 0] = k_ref[...].T
        vt_ref[0, 0] = v_ref[...].T
        lft_ref[0, 0] = lf_ref[...].T[0:N_HEADS, :]


def _proj(x, g, w, bf, *, tm, seq, prompt, layer=0, depth=1, stacked=None):
    n, d = x.shape
    h = MIX_HALF
    nj = seq // tm
    row = lambda i: (i, 0)
    fixed = lambda i: (0, 0)
    trans = lambda i: (layer, i // nj, 0, i % nj)
    rows = lambda width: (pl.BlockSpec((tm, width), row), jax.ShapeDtypeStruct((n, width), F32))
    cols = lambda height: (pl.BlockSpec((1, 1, height, tm), trans),
                           jax.ShapeDtypeStruct((depth, n // seq, height, seq), F32))
    if prompt:
        outs = [rows(h), rows(h), cols(h), cols(h), rows(HR_WIDTH), rows(LANES), cols(N_HEADS)]
    else:
        outs = [rows(h), rows(h), rows(h), rows(HR_WIDTH), rows(LANES)]
    stacked = tuple(stacked or ())
    n_in = 4
    aliases = {n_in + i: o for i, o in enumerate((2, 3, 6)[:len(stacked)])}
    return pl.pallas_call(
        functools.partial(_proj_kernel, prompt=prompt, n_alias=len(stacked)),
        grid=(n // tm,),
        in_specs=[
            pl.BlockSpec((tm, d), row),
            pl.BlockSpec((1, d), fixed),
            pl.BlockSpec((1,) + w.shape[1:], lambda i: (layer, 0, 0)),
            pl.BlockSpec((1, LANES), fixed),
        ] + [pl.BlockSpec(memory_space=pl.ANY)] * len(stacked),
        out_specs=[o[0] for o in outs],
        out_shape=[o[1] for o in outs],
        scratch_shapes=[pltpu.VMEM((tm, h), F32)] if prompt else [],
        input_output_aliases=aliases,
        compiler_params=_params("parallel"),
        name="proj",
    )(x, g, w, bf, *stacked)


def _rwkv_chunk_kernel(h_ref, hp_ref, p0_ref, mu_ref, w0_ref, a0_ref, kk_ref, ka_ref, rk_ref,
                       wd_ref, wi_ref, wg_ref, bd_ref,
                       qc_o, y0_o, g_o, bo_o, m_o, hh_o, *, chunk, n_valid):
    j = pl.program_id(1)
    C = chunk
    C2 = 2 * C
    nb, t_in, _ = h_ref.shape
    tt = g_o.shape[1]
    m = MIX_HALF
    bd = bd_ref[...]

    def mix_inputs(b):
        h = h_ref[b]
        if t_in < tt:
            h = jnp.concatenate([h, jnp.zeros((tt - t_in, HR_WIDTH), F32)], axis=0)
        prev = jnp.where(j == 0, p0_ref[b], hp_ref[b, 7:8, :])
        trow = lax.broadcasted_iota(jnp.int32, h.shape, 0)
        shifted = jnp.where(trow == 0, prev, pltpu.roll(h, 1, axis=0))
        hs = h + mu_ref[...] * (shifted - h)
        r = hs[:, 0:m]
        k = hs[:, m:2 * m]
        v = hs[:, 2 * m:3 * m]
        o = 3 * m
        d_decay = hs[:, o:o + DECAY_PAD]
        d_iclr = hs[:, o + DECAY_PAD:o + DECAY_PAD + ICLR_PAD]
        d_gate = hs[:, o + DECAY_PAD + ICLR_PAD:]
        w_log = -_softplus(-(w0_ref[...] + _dot(jnp.tanh(d_decay).astype(BF16), wd_ref[...]))) - 0.5
        lw = -jnp.exp(w_log)
        a = _sigmoid(a0_ref[...] + _dot(d_iclr.astype(BF16), wi_ref[...]))
        g_o[b] = _dot(_sigmoid(d_gate).astype(BF16), wg_ref[...])
        kk = k * kk_ref[...]
        kk = kk / jnp.maximum(jnp.sqrt(_group_sum(kk * kk, bd)), 1e-12)
        k = k * (1.0 + (a - 1.0) * ka_ref[...])
        bo_o[b] = _group_sum(r * k * rk_ref[...], bd) * v
        na = -kk
        bb = kk * a
        if n_valid < tt:
            ok = lax.broadcasted_iota(jnp.int32, (tt, m), 0) < n_valid
            zero = lambda x: jnp.where(ok, x, 0.0)
            r, k, v, lw, na, bb = zero(r), zero(k), zero(v), zero(lw), zero(na), zero(bb)
        return r, k, v, lw, na, bb

    row = lax.broadcasted_iota(jnp.int32, (C2, C2), 0)
    col = lax.broadcasted_iota(jnp.int32, (C2, C2), 1)
    rt = row & (C - 1)
    ct = col & (C - 1)
    strict = rt > ct
    incl = rt >= ct
    eye = (row == col).astype(F32)
    ti = lax.broadcasted_iota(jnp.int32, (C, C), 0)
    tj = lax.broadcasted_iota(jnp.int32, (C, C), 1)
    tril01 = (ti >= tj).astype(BF16)
    lane_lo = lax.broadcasted_iota(jnp.int32, (1, PAIR), 1) < HEAD_DIM

    def stack(x):
        return jnp.concatenate([jnp.where(lane_lo, x, 0.0), jnp.where(lane_lo, 0.0, x)], axis=0)

    prob = []
    for b in range(nb):
        r, k, v, lw, na, bb = mix_inputs(b)
        for ci in range(tt // C):
            rows = slice(ci * C, (ci + 1) * C)
            lwc = lw[rows]
            cum = _dot_01x(tril01, lwc)
            cum_end = cum[C - 1:C, :]
            e_neg = jnp.exp(-cum)
            e_end = jnp.exp(cum_end - cum)
            at = na[rows] * jnp.exp(cum - lwc)
            rt_ = r[rows] * jnp.exp(cum)
            bt = bb[rows] * e_neg
            kt = k[rows] * e_neg
            bh = bb[rows] * e_end
            kh = k[rows] * e_end
            wc = jnp.exp(cum_end)
            for hp in range(N_PAIRS):
                sl = slice(hp * PAIR, (hp + 1) * PAIR)
                v2 = stack(v[rows, sl])
                prob.append(dict(
                    b=b, ci=ci, sl=sl, hp=hp, rq=rt_[:, sl], wc=wc[:, sl],
                    at=stack(at[:, sl]).astype(BF16), rt=stack(rt_[:, sl]).astype(BF16),
                    bt=stack(bt[:, sl]).astype(BF16), kt=stack(kt[:, sl]).astype(BF16),
                    bh=stack(bh[:, sl]).astype(BF16), kh=stack(kh[:, sl]).astype(BF16),
                    v=v2.astype(BF16), vt=v2.T.astype(BF16)))

    mm = [_dot_nt(jnp.concatenate([p["at"], p["rt"]], axis=0), jnp.concatenate([p["bt"], p["kt"]], axis=0))
          for p in prob]
    n_ab = [jnp.where(strict, x[0:C2, 0:C2], 0.0) for x in mm]
    a_kr = [jnp.concatenate([jnp.where(strict, x[0:C2, C2:], 0.0),
                             jnp.where(incl, x[C2:, C2:], 0.0)], axis=0).astype(BF16) for x in mm]
    a_rb = [jnp.where(incl, x[C2:, 0:C2], 0.0).astype(BF16) for x in mm]
    kv = [_dot(x, p["v"]) for x, p in zip(a_kr, prob)]
    vtk = [_dot(p["vt"], p["kh"]) for p in prob]

    t_inv = [eye + jnp.where((rt >> 1) == (ct >> 1), x, 0.0) for x in n_ab]
    lvl = 1
    while (1 << lvl) < C:
        sel = ((rt >> (lvl + 1)) == (ct >> (lvl + 1))) & ((rt >> lvl) != (ct >> lvl))
        off = [jnp.where(sel, x, 0.0).astype(BF16) for x in n_ab]
        tb = [x.astype(BF16) for x in t_inv]
        xx = [_dot(x, y).astype(BF16) for x, y in zip(off, tb)]
        t_inv = [x + _dot(y, z) for x, y, z in zip(t_inv, tb, xx)]
        lvl += 1

    pu = [_dot(t.astype(BF16), jnp.concatenate([p["at"], x[0:C2].astype(BF16)], axis=1))
          for t, p, x in zip(t_inv, prob, kv)]
    ab = [_dot(x, y.astype(BF16)) for x, y in zip(a_rb, pu)]
    gb = [_dot(x.T.astype(BF16), p["bh"]) for x, p in zip(pu, prob)]
    for p, x_ab, x_kv, x_gb, x_vtk in zip(prob, ab, kv, gb, vtk):
        b = p["b"]
        rows = slice(p["ci"] * C, (p["ci"] + 1) * C)
        qc_o[b, rows, p["sl"]] = p["rq"] + x_ab[0:C, 0:PAIR] + x_ab[C:, 0:PAIR]
        y0 = x_ab[:, PAIR:] + x_kv[C2:]
        y0_o[b, rows, p["sl"]] = y0[0:C] + y0[C:]
        m_o[b, p["ci"], p["hp"]] = (x_gb[0:PAIR] + eye * p["wc"]).astype(BF16)
        hh_o[b, p["ci"], p["hp"]] = x_gb[PAIR:] + x_vtk


def _rwkv_chunk(hr, prev0, lp, *, nb, tt, chunk, n_valid):
    b, t_in, _ = hr.shape
    t = max(t_in, tt)
    m = MIX_HALF
    g = tt // chunk
    tile = lambda i, j: (i, j, 0)
    fixed = lambda i, j: (0, 0)
    vec = pl.BlockSpec((1, m), fixed)
    tok = (pl.BlockSpec((nb, tt, m), tile), jax.ShapeDtypeStruct((b, t, m), F32))
    mat = lambda dt: (pl.BlockSpec((nb, g, N_PAIRS, PAIR, PAIR), lambda i, j: (i, j, 0, 0, 0)),
                      jax.ShapeDtypeStruct((b, t // chunk, N_PAIRS, PAIR, PAIR), dt))
    outs = [tok, tok, tok, tok, mat(BF16), mat(F32)]
    return pl.pallas_call(
        functools.partial(_rwkv_chunk_kernel, chunk=chunk, n_valid=n_valid),
        grid=(b // nb, t // tt),
        in_specs=[
            pl.BlockSpec((nb, min(tt, t_in), HR_WIDTH), tile),
            pl.BlockSpec((nb, 8, HR_WIDTH), lambda i, j: (i, jnp.maximum(j * (tt // 8) - 1, 0), 0)),
            pl.BlockSpec((nb, 1, HR_WIDTH), lambda i, j: (i, 0, 0)),
            pl.BlockSpec((1, HR_WIDTH), fixed),
            vec, vec, vec, vec, vec,
            pl.BlockSpec((DECAY_PAD, m), fixed),
            pl.BlockSpec((ICLR_PAD, m), fixed),
            pl.BlockSpec((GATE_PAD, m), fixed),
            pl.BlockSpec((m, m), fixed),
        ],
        out_specs=[o[0] for o in outs],
        out_shape=[o[1] for o in outs],
        compiler_params=_params("parallel", "arbitrary"),
        name="rwkv_chunk",
    )(hr, hr, prev0, lp["mu"], lp["w0"], lp["a0"], lp["k_k"], lp["k_a"], lp["r_k"],
      lp["w_decay"], lp["w_iclr"], lp["w_gate"], lp["bd512"])


def _rwkv_state_kernel(qc_ref, y0_ref, g_ref, bo_ref, m_ref, hh_ref, s0_ref, lng_ref, lnb_ref, bd_ref,
                       y_ref, sf_ref, s_ref, *, chunk, nsteps):
    j = pl.program_id(1)
    C = chunk
    nb, tt, _ = qc_ref.shape
    bd = bd_ref[...]

    @pl.when(j == 0)
    def _():
        s_ref[...] = s0_ref[...]

    cells = [(b, hp) for b in range(nb) for hp in range(N_PAIRS)]
    state = [s_ref[b, hp] for b, hp in cells]
    for ci in range(tt // C):
        rows = slice(ci * C, (ci + 1) * C)
        sb = [s.astype(BF16) for s in state]
        ys = [_dot_nt(qc_ref[b, rows, hp * PAIR:(hp + 1) * PAIR].astype(BF16), s)
              + y0_ref[b, rows, hp * PAIR:(hp + 1) * PAIR] for (b, hp), s in zip(cells, sb)]
        state = [_dot(s, m_ref[b, ci, hp]) + hh_ref[b, ci, hp] for (b, hp), s in zip(cells, sb)]
        means = [_group_sum(y, bd) * (1.0 / HEAD_DIM) for y in ys]
        yc = [y - mu for y, mu in zip(ys, means)]
        var = [_group_sum(y * y, bd) * (1.0 / HEAD_DIM) for y in yc]
        for (b, hp), y, vr in zip(cells, yc, var):
            sl = slice(hp * PAIR, (hp + 1) * PAIR)
            yn = y * lax.rsqrt(vr + GN_EPS) * lng_ref[:, sl] + lnb_ref[:, sl]
            y_ref[b, rows, sl] = (yn + bo_ref[b, rows, sl]) * g_ref[b, rows, sl]
    for (b, hp), s in zip(cells, state):
        s_ref[b, hp] = s

    @pl.when(j == nsteps - 1)
    def _():
        sf_ref[...] = s_ref[...]


def _rwkv_state(qc, y0, g, bonus, mm, hh, s0, lp, *, nb, tt, chunk):
    bsz, t, m = qc.shape
    gch = tt // chunk
    nsteps = t // tt
    tile = pl.BlockSpec((nb, tt, m), lambda i, j: (i, j, 0))
    mat = pl.BlockSpec((nb, gch, N_PAIRS, PAIR, PAIR), lambda i, j: (i, j, 0, 0, 0))
    st = pl.BlockSpec((nb, N_PAIRS, PAIR, PAIR), lambda i, j: (i, 0, 0, 0))
    vec = pl.BlockSpec((1, m), lambda i, j: (0, 0))
    return pl.pallas_call(
        functools.partial(_rwkv_state_kernel, chunk=chunk, nsteps=nsteps),
        grid=(bsz // nb, nsteps),
        in_specs=[tile, tile, tile, tile, mat, mat, st, vec, vec,
                  pl.BlockSpec((PAIR, PAIR), lambda i, j: (0, 0))],
        out_specs=[tile, st],
        out_shape=[jax.ShapeDtypeStruct((bsz, t, m), F32),
                   jax.ShapeDtypeStruct((bsz, N_PAIRS, PAIR, PAIR), F32)],
        scratch_shapes=[pltpu.VMEM((nb, N_PAIRS, PAIR, PAIR), F32)],
        compiler_params=_params("parallel", "arbitrary"),
        name="rwkv_state",
    )(qc, y0, g, bonus, mm, hh, s0, lp["lnx_g"], lp["lnx_b"], lp["bd128"])


def _bias_lane(head):
    return HEAD_DIM if head % 2 == 0 else 0


def _fox_bias_kernel(q_ref, k_ref, lf_ref, pk_ref, pq_ref, qa_ref, ka_ref, *, blk, nblk):
    ti = lax.broadcasted_iota(jnp.int32, (blk, blk), 0)
    tj = lax.broadcasted_iota(jnp.int32, (blk, blk), 1)
    tril01 = (ti >= tj).astype(BF16)
    lane = lax.broadcasted_iota(jnp.int32, (1, PAIR), 1)
    carry = jnp.zeros((1, LANES), F32)
    for i in range(nblk):
        sl = slice(i * blk, (i + 1) * blk)
        c = _dot_01x(tril01, lf_ref[0, sl, :]) + carry
        carry = c[blk - 1:blk, :]
        p1, p2, p3 = (jnp.where(lane < N_HEADS, x.astype(F32), 0.0) for x in _split3(c * LOG2E))
        pieces = (p1 + pltpu.roll(p2, N_HEADS, axis=1) + pltpu.roll(p3, 2 * N_HEADS, axis=1)).astype(BF16)
        for hp in range(N_PAIRS):
            ps = slice(hp * PAIR, (hp + 1) * PAIR)
            kp = k_ref[0, sl, ps]
            qp = q_ref[0, sl, ps] * (HEAD_DIM ** -0.5 * LOG2E)
            k_aug = _dot(pieces, pk_ref[hp])
            q_aug = _dot(pieces, pq_ref[hp])
            for half in range(2):
                h = 2 * hp + half
                own = (lane < HEAD_DIM) if half == 0 else (lane >= HEAD_DIM)
                off = lane - _bias_lane(h)
                hs = slice(h * PAIR, (h + 1) * PAIR)
                aug = slice(half * PAIR, (half + 1) * PAIR)
                ka_ref[0, sl, hs] = jnp.where(
                    own, kp, jnp.where((off >= 3) & (off < 6), 1.0, k_aug[:, aug])).astype(BF16)
                qa_ref[0, sl, hs] = jnp.where(
                    own, qp, jnp.where((off >= 0) & (off < 3), 1.0, q_aug[:, aug])).astype(BF16)


def _bias_placement():
    src = jnp.arange(LANES)
    piece, head = src // N_HEADS, src % N_HEADS
    dst = jnp.arange(2 * PAIR)
    pk, pq = [], []
    for hp in range(N_PAIRS):
        k_hit = jnp.zeros((LANES, 2 * PAIR), bool)
        q_hit = jnp.zeros((LANES, 2 * PAIR), bool)
        for half in range(2):
            h = 2 * hp + half
            mine = ((head == h) & (piece < 3))[:, None]
            lane0 = half * PAIR + _bias_lane(h)
            k_hit |= mine & (dst[None, :] == lane0 + piece[:, None])
            q_hit |= mine & (dst[None, :] == lane0 + 3 + piece[:, None])
        pk.append(jnp.where(k_hit, -1.0, 0.0))
        pq.append(jnp.where(q_hit, 1.0, 0.0))
    return jnp.stack(pk).astype(BF16), jnp.stack(pq).astype(BF16)


def _fox_bias(q, k, lf, place_k, place_q, *, blk):
    b, s, m = q.shape
    wide = N_HEADS * PAIR
    row = lambda width: pl.BlockSpec((1, s, width), lambda i: (i, 0, 0))
    place = pl.BlockSpec(place_k.shape, lambda i: (0, 0, 0))
    out = jax.ShapeDtypeStruct((b, s, wide), BF16)
    return pl.pallas_call(
        functools.partial(_fox_bias_kernel, blk=blk, nblk=s // blk),
        grid=(b,),
        in_specs=[row(m), row(m), row(LANES), place, place],
        out_specs=[row(wide), row(wide)],
        out_shape=[out, out],
        compiler_params=_params("parallel"),
        name="fox_bias",
    )(q, k, lf, place_k, place_q)


def _fox_prompt_kernel(qa_ref, ka_ref, vt_ref, gain_ref, o_ref, m_s, l_s, acc_s, st_s, *, blk):
    qi = pl.program_id(1)
    heads = range(N_HEADS)
    m_s[...] = jnp.full(m_s.shape, NEG_INF, F32)
    l_s[...] = jnp.zeros_like(l_s)
    acc_s[...] = jnp.zeros_like(acc_s)

    def keys(j):
        return pl.ds(pl.multiple_of(j * blk, blk), blk)

    def scores(j, slot, diagonal):
        ks = keys(j)
        st = [_dot_nt(ka_ref[0, ks, h * PAIR:(h + 1) * PAIR], qa_ref[0, :, h * PAIR:(h + 1) * PAIR])
              for h in heads]
        if diagonal:
            ki = lax.broadcasted_iota(jnp.int32, (blk, blk), 0)
            qj = lax.broadcasted_iota(jnp.int32, (blk, blk), 1)
            st = [jnp.where(ki <= qj, x, NEG_INF) for x in st]
        for h in heads:
            st_s[slot, h] = st[h]

    def consume(j, slot):
        ks = keys(j)
        st = [st_s[slot, h] for h in heads]
        m_old = [m_s[h:h + 1, :] for h in heads]
        m_new = [jnp.maximum(mo, jnp.max(x, axis=0, keepdims=True)) for mo, x in zip(m_old, st)]
        p = [jnp.exp2(x - mn) for x, mn in zip(st, m_new)]
        pv = [_dot(vt_ref[0, 0, h * HEAD_DIM:(h + 1) * HEAD_DIM, ks].astype(BF16), x.astype(BF16))
              for h, x in zip(heads, p)]
        for h in heads:
            alpha = jnp.exp2(m_old[h] - m_new[h])
            m_s[h:h + 1, :] = m_new[h]
            l_s[h:h + 1, :] = alpha * l_s[h:h + 1, :] + jnp.sum(p[h], axis=0, keepdims=True)
            rows = slice(h * HEAD_DIM, (h + 1) * HEAD_DIM)
            acc_s[rows, :] = alpha * acc_s[rows, :] + pv[h]

    scores(qi, 0, True)

    def body(u, carry):
        j = 2 * u
        scores(j, 1, False)
        consume(jnp.where(u == 0, qi, j - 1), 0)
        scores(j + 1, 0, False)
        consume(j, 1)
        return carry

    lax.fori_loop(0, qi // 2, body, 0)
    odd = (qi & 1) == 1

    @pl.when(odd)
    def _():
        scores(qi - 1, 1, False)
        consume(jnp.where(qi == 1, qi, qi - 2), 0)
        consume(qi - 1, 1)

    @pl.when(jnp.logical_not(odd))
    def _():
        consume(jnp.where(qi == 0, qi, qi - 1), 0)

    for hp in range(N_PAIRS):
        halves = []
        for h in (2 * hp, 2 * hp + 1):
            o = acc_s[h * HEAD_DIM:(h + 1) * HEAD_DIM, :] / l_s[h:h + 1, :]
            ms = jnp.mean(o * o, axis=0, keepdims=True)
            halves.append(o * lax.rsqrt(ms + RMS_EPS))
        sl = slice(hp * PAIR, (hp + 1) * PAIR)
        o_ref[0, :, sl] = jnp.concatenate(halves, axis=0).T * gain_ref[:, sl]


def _fox_prompt(qa, ka, vt_all, gain, *, layer, blk):
    b, s, wide = qa.shape
    m = MIX_HALF
    nb = s // blk
    return pl.pallas_call(
        functools.partial(_fox_prompt_kernel, blk=blk),
        grid=(b, nb),
        in_specs=[
            pl.BlockSpec((1, blk, wide), lambda i, j: (i, j, 0)),
            pl.BlockSpec((1, s, wide), lambda i, j: (i, 0, 0)),
            pl.BlockSpec((1, 1, m, s), lambda i, j: (layer, i, 0, 0)),
            pl.BlockSpec((1, m), lambda i, j: (0, 0)),
        ],
        out_specs=pl.BlockSpec((1, blk, m), lambda i, j: (i, j, 0)),
        out_shape=jax.ShapeDtypeStruct((b, s, m), F32),
        scratch_shapes=[
            pltpu.VMEM((N_HEADS, blk), F32),
            pltpu.VMEM((N_HEADS, blk), F32),
            pltpu.VMEM((m, blk), F32),
            pltpu.VMEM((2, N_HEADS, blk, blk), F32),
        ],
        compiler_params=_params("parallel", "arbitrary"),
        name="fox_prompt",
    )(qa, ka, vt_all, gain)


def _fox_decode_kernel(pt_ref, q_ref, kn_ref, vn_ref, lfn_ref, lfnt_ref, gain_ref, hm_ref, bd_ref, *rest,
                       npg, nsteps, tnew):
    k_pages = rest[0:npg]
    v_pages = rest[npg:2 * npg]
    lft_ref = rest[2 * npg]
    o_ref, m_s, l_s, acc_s, carry_s, qbd_s, cn_s = rest[2 * npg + 1:]
    seq = pl.program_id(0)
    g = pl.program_id(1)
    nrow = tnew * N_HEADS
    page = k_pages[0].shape[3]
    li = lax.broadcasted_iota(jnp.int32, (LANES, LANES), 0)
    lj = lax.broadcasted_iota(jnp.int32, (LANES, LANES), 1)
    hm = hm_ref[...]

    def tile_rows(x):
        return jnp.concatenate([x] * tnew, axis=0)

    @pl.when(g == 0)
    def _():
        q = q_ref[0] * (HEAD_DIM ** -0.5)
        qbd = jnp.concatenate([jnp.broadcast_to(q[t:t + 1, :], hm.shape) * hm for t in range(tnew)], axis=0)
        qbd_s[...] = qbd.astype(BF16)
        cn = lfn_ref[0]
        trow = lax.broadcasted_iota(jnp.int32, cn.shape, 0)
        sh = 1
        while sh < tnew:
            cn = cn + jnp.where(trow >= sh, pltpu.roll(cn, sh, axis=0), 0.0)
            sh *= 2
        hsel = (lax.broadcasted_iota(jnp.int32, (N_HEADS, LANES), 0)
                == lax.broadcasted_iota(jnp.int32, (N_HEADS, LANES), 1)).astype(F32)
        cn_rows = jnp.concatenate(
            [jnp.sum(jnp.broadcast_to(cn[t:t + 1, :], hsel.shape) * hsel, axis=1, keepdims=True)
             for t in range(tnew)], axis=0)
        cn_s[...] = cn_rows
        cnt = _dot_x01(lfnt_ref[0], (li <= lj).astype(BF16))
        pad = jnp.zeros((page - tnew, MIX_HALF), F32)
        kn = jnp.concatenate([kn_ref[0], pad], axis=0).astype(BF16)
        vn = jnp.concatenate([vn_ref[0], pad], axis=0).astype(BF16)
        s = _dot_nt(qbd_s[...], kn) + (cn_rows - tile_rows(cnt))
        rtok = lax.broadcasted_iota(jnp.int32, (nrow, LANES), 0) >> 3
        ktok = lax.broadcasted_iota(jnp.int32, (nrow, LANES), 1)
        s = jnp.where(ktok <= rtok, s, NEG_INF)
        m = jnp.max(s, axis=1, keepdims=True)
        p = jnp.exp(s - m)
        m_s[...] = m
        l_s[...] = jnp.sum(p, axis=1, keepdims=True)
        acc_s[...] = _dot(p.astype(BF16), vn)
        carry_s[...] = jnp.zeros_like(carry_s)

    slots = range(npg)
    nr = npg * N_HEADS
    lf_all = jnp.concatenate([lft_ref[0, pt_ref[seq, (nsteps - 1 - g) * npg + i]] for i in slots], axis=0)
    ri = lax.broadcasted_iota(jnp.int32, (nr, nr), 0)
    rj = lax.broadcasted_iota(jnp.int32, (nr, nr), 1)
    later_page = (((ri & (N_HEADS - 1)) == (rj & (N_HEADS - 1))) & (rj > ri)).astype(BF16)
    total = jnp.sum(lf_all, axis=1, keepdims=True)
    after = jnp.sum(_dot_01x(later_page, lf_all), axis=1, keepdims=True)
    carry = carry_s[...]
    bias_all = _dot_x01(lf_all, (li > lj).astype(BF16)) + (after + jnp.concatenate([carry] * npg, axis=0))
    carry_s[...] = carry + (after + total)[0:N_HEADS]
    qbd = qbd_s[...]
    cn_rows = cn_s[...]
    kb = [k_pages[i][0, 0].astype(BF16) for i in slots]
    sc = [_dot(qbd, x) for x in kb]
    s = jnp.concatenate([x + (tile_rows(bias_all[i * N_HEADS:(i + 1) * N_HEADS, :]) + cn_rows)
                         for i, x in zip(slots, sc)], axis=1)
    m_old = m_s[...]
    m_new = jnp.maximum(m_old, jnp.max(s, axis=1, keepdims=True))
    alpha = jnp.exp(m_old - m_new)
    p = jnp.exp(s - m_new)
    m_s[...] = m_new
    l_s[...] = alpha * l_s[...] + jnp.sum(p, axis=1, keepdims=True)
    vb = [v_pages[i][0, 0].astype(BF16) for i in slots]
    pv = [_dot_nt(p[:, i * page:(i + 1) * page].astype(BF16), x) for i, x in zip(slots, vb)]
    while len(pv) > 1:
        pv = [a + b for a, b in zip(pv[0::2], pv[1::2])] + ([pv[-1]] if len(pv) % 2 else [])
    acc_s[...] = alpha * acc_s[...] + pv[0]

    @pl.when(g == nsteps - 1)
    def _():
        o_rows = acc_s[...] / l_s[...]
        o = jnp.concatenate(
            [jnp.sum(o_rows[t * N_HEADS:(t + 1) * N_HEADS, :] * hm, axis=0, keepdims=True) for t in range(tnew)],
            axis=0)
        ms = _dot_x01(o * o, bd_ref[...]) * (1.0 / HEAD_DIM)
        o_ref[0] = o * lax.rsqrt(ms + RMS_EPS) * gain_ref[...]


def _fox_decode(layer, page_table, q, kn, vn, lfn, lfn_t, cache_k, cache_v, cache_lft, gain, hm, bd, *, npg):
    b, tnew, m = q.shape
    n_pages = page_table.shape[1]
    page = cache_k.shape[3]
    nsteps = n_pages // npg
    nrow = tnew * N_HEADS

    def tok(i, g, pt):
        return (i, 0, 0)

    def fixed(i, g, pt):
        return (0, 0)

    def page_map(slot):
        return lambda i, g, pt: (layer, pt[i, (nsteps - 1 - g) * npg + slot], 0, 0)

    kv_specs = [pl.BlockSpec((1, 1, m, page), page_map(s)) for s in range(npg)]
    n_pool = cache_lft.shape[1]
    lf_spec = pl.BlockSpec((1, n_pool, N_HEADS, page), lambda i, g, pt: (layer, 0, 0, 0),
                           pipeline_mode=pl.Buffered(1))
    grid_spec = pltpu.PrefetchScalarGridSpec(
        num_scalar_prefetch=1,
        grid=(b, nsteps),
        in_specs=[
            pl.BlockSpec((1, tnew, m), tok),
            pl.BlockSpec((1, tnew, m), tok),
            pl.BlockSpec((1, tnew, m), tok),
            pl.BlockSpec((1, tnew, LANES), tok),
            pl.BlockSpec((1, N_HEADS, LANES), tok),
            pl.BlockSpec((1, m), fixed),
            pl.BlockSpec((N_HEADS, m), fixed),
            pl.BlockSpec((m, m), fixed),
        ] + kv_specs + kv_specs + [lf_spec],
        out_specs=pl.BlockSpec((1, tnew, m), tok),
        scratch_shapes=[
            pltpu.VMEM((nrow, 1), F32),
            pltpu.VMEM((nrow, 1), F32),
            pltpu.VMEM((nrow, m), F32),
            pltpu.VMEM((N_HEADS, 1), F32),
            pltpu.VMEM((nrow, m), BF16),
            pltpu.VMEM((nrow, 1), F32),
        ],
    )
    return pl.pallas_call(
        functools.partial(_fox_decode_kernel, npg=npg, nsteps=nsteps, tnew=tnew),
        grid_spec=grid_spec,
        out_shape=jax.ShapeDtypeStruct((b, tnew, m), F32),
        compiler_params=_params("parallel", "arbitrary"),
        name="fox_decode",
    )(page_table, q, kn, vn, lfn, lfn_t, gain, hm, bd,
      *([cache_k] * npg), *([cache_v] * npg), cache_lft)


def _out_proj_kernel(x_ref, yr_ref, yf_ref, wr_ref, wf_ref, o_ref):
    o_ref[...] = (x_ref[...] + _dot(yr_ref[...].astype(BF16), wr_ref[...])
                  + _dot(yf_ref[...].astype(BF16), wf_ref[...]))


def _out_proj(x, yr, yf, wr, wf, *, tm):
    n, d = x.shape
    m = yr.shape[1]
    row = lambda i: (i, 0)
    fixed = lambda i: (0, 0)
    return pl.pallas_call(
        _out_proj_kernel,
        grid=(n // tm,),
        in_specs=[pl.BlockSpec((tm, d), row), pl.BlockSpec((tm, m), row), pl.BlockSpec((tm, m), row),
                  pl.BlockSpec((m, d), fixed), pl.BlockSpec((m, d), fixed)],
        out_specs=pl.BlockSpec((tm, d), row),
        out_shape=jax.ShapeDtypeStruct((n, d), F32),
        compiler_params=_params("parallel"),
        name="out_proj",
    )(x, yr, yf, wr, wf)


def _ple_kernel(x_ref, p_ref, g_ref, wg_ref, wu_ref, fg_ref, o_ref, *, final):
    x = x_ref[...]
    gate = _sigmoid(_dot(_rms(x, g_ref[...]).astype(BF16), wg_ref[...]))
    y = x + gate * _dot(p_ref[0].astype(BF16), wu_ref[...])
    o_ref[...] = _rms(y, fg_ref[...]) if final else y


def _ple(x, p, g, wg, wu, fg, *, layer, tm, final):
    n, d = x.shape
    pd = p.shape[2]
    row = lambda i: (i, 0)
    fixed = lambda i: (0, 0)
    return pl.pallas_call(
        functools.partial(_ple_kernel, final=final),
        grid=(n // tm,),
        in_specs=[pl.BlockSpec((tm, d), row), pl.BlockSpec((1, tm, pd), lambda i: (layer, i, 0)),
                  pl.BlockSpec((1, d), fixed),
                  pl.BlockSpec((d, d), fixed), pl.BlockSpec((pd, d), fixed), pl.BlockSpec((1, d), fixed)],
        out_specs=pl.BlockSpec((tm, d), row),
        out_shape=jax.ShapeDtypeStruct((n, d), F32),
        compiler_params=_params("parallel"),
        name="ple",
    )(x, p, g, wg, wu, fg)


def _pad_cols(w, width):
    return jnp.pad(w, [(0, 0)] * (w.ndim - 1) + [(0, width - w.shape[-1])])


def _pad_rows(w, height):
    return jnp.pad(w, [(0, 0)] * (w.ndim - 2) + [(0, height - w.shape[-2]), (0, 0)])


def _pack_hr(h):
    m = MIX_HALF
    o = 3 * m
    return jnp.concatenate([
        h[..., :o],
        _pad_cols(h[..., o:o + DECAY_RANK], DECAY_PAD),
        _pad_cols(h[..., o + DECAY_RANK:o + DECAY_RANK + ICLR_RANK], ICLR_PAD),
        _pad_cols(h[..., o + DECAY_RANK + ICLR_RANK:], GATE_PAD)], axis=-1)


def _unpack_hr(h):
    m = MIX_HALF
    o = 3 * m
    return jnp.concatenate([
        h[..., :o],
        h[..., o:o + DECAY_RANK],
        h[..., o + DECAY_PAD:o + DECAY_PAD + ICLR_RANK],
        h[..., o + DECAY_PAD + ICLR_PAD:o + DECAY_PAD + ICLR_PAD + GATE_RANK]], axis=-1)


def _pair_state(s):
    b = s.shape[0]
    s = s.reshape(b, N_PAIRS, 2, HEAD_DIM, HEAD_DIM)
    z = jnp.zeros_like(s[:, :, 0])
    top = jnp.concatenate([s[:, :, 0], z], axis=-1)
    bot = jnp.concatenate([z, s[:, :, 1]], axis=-1)
    return jnp.concatenate([top, bot], axis=-2)


def _unpair_state(s):
    b = s.shape[0]
    d = HEAD_DIM
    return jnp.stack([s[:, :, :d, :d], s[:, :, d:, d:]], axis=2).reshape(b, N_HEADS, d, d)


def _block_diag01(n):
    i = jnp.arange(n) // HEAD_DIM
    return (i[:, None] == i[None, :]).astype(BF16)


def _rwkv_mix(hr3, prev0, s0, lp, *, n_valid):
    bsz, t, _ = hr3.shape
    chunk = REC_CHUNK
    t = max(t, chunk)
    tt = min(t, REC_TILE)
    nb = max(1, min(bsz, REC_TILE // tt))
    qc, y0, g, bonus, mm, hh = _rwkv_chunk(hr3, prev0, lp, nb=nb, tt=tt, chunk=chunk, n_valid=n_valid)
    return _rwkv_state(qc, y0, g, bonus, mm, hh, s0, lp, nb=min(bsz, STATE_BATCH), tt=min(t, STATE_TILE),
                       chunk=chunk)


def kernel(x_prompt, x_sample, cache_k, cache_v, cache_logf, state_wkv, state_shift, page_table, p_prompt, p_sample, ffn1_norm, ffn1_w_gate, ffn1_w_up, ffn1_w_down, mix_norm, w_in, rwkv_mu, rwkv_w0, rwkv_w_decay, rwkv_a0, rwkv_w_iclr, rwkv_w_gate, rwkv_k_k, rwkv_k_a, rwkv_r_k, rwkv_lnx_g, rwkv_lnx_b, fox_b_f, fox_out_norm, w_out, ffn2_norm, ffn2_w_gate, ffn2_w_up, ffn2_w_down, ple_norm, ple_w_gate, ple_w_up, final_norm):
    depth = w_in.shape[0]
    bp, sp, d = x_prompt.shape
    bs, ts, _ = x_sample.shape
    m = MIX_HALF
    n_pool, page = cache_k.shape[1], cache_k.shape[2]
    npr, nsm = bp * sp, bs * ts
    pd = p_prompt.shape[-1]

    fox_cols = w_in[:, :, RWKV_PROJ:]
    w_pack = jnp.concatenate([
        fox_cols[:, :, :3 * m],
        _pack_hr(w_in[:, :, :RWKV_PROJ]),
        _pad_cols(fox_cols[:, :, 3 * m:], LANES)], axis=-1).astype(BF16)
    b_f = _pad_cols(fox_b_f, LANES)[:, None, :]
    bd512 = _block_diag01(m)
    bd128 = _block_diag01(PAIR)
    hm = (jnp.arange(m)[None, :] // HEAD_DIM == jnp.arange(N_HEADS)[:, None]).astype(F32)
    bf = lambda w: w.astype(BF16)
    f1g, f1u, f1d = ffn1_w_gate, ffn1_w_up, ffn1_w_down
    f2g, f2u, f2d = ffn2_w_gate, ffn2_w_up, ffn2_w_down
    wo = bf(w_out)
    pg, pu = bf(ple_w_gate), bf(ple_w_up)
    wdec = bf(_pad_rows(rwkv_w_decay, DECAY_PAD))
    wicl = bf(_pad_rows(rwkv_w_iclr, ICLR_PAD))
    wgat = bf(_pad_rows(rwkv_w_gate, GATE_PAD))
    mu = _pack_hr(rwkv_mu)
    cache_kt = jnp.transpose(cache_k, (0, 1, 3, 4, 2)).reshape(depth, n_pool, m, page)
    cache_vt = jnp.transpose(cache_v, (0, 1, 3, 4, 2)).reshape(depth, n_pool, m, page)
    cache_lft = jnp.swapaxes(cache_logf, 2, 3)
    fg = final_norm[None, :]
    place_k, place_q = _bias_placement()

    tm_p = 1024 if npr % 1024 == 0 else ATT_BLOCK
    tm_ffn = FFN_TILE if npr % FFN_TILE == 0 else tm_p
    stacked = None
    tf = 256
    blk = ATT_BLOCK
    npg = min(DEC_PAGES_PER_STEP, page_table.shape[1])

    xp = x_prompt.reshape(npr, d)
    xs = x_sample.reshape(nsm, d)
    outs = [[] for _ in range(10)]
    for l in range(depth):
        lp = dict(mu=mu[l][None], w0=rwkv_w0[l][None], a0=rwkv_a0[l][None], k_k=rwkv_k_k[l][None],
                  k_a=rwkv_k_a[l][None], r_k=rwkv_r_k[l].reshape(1, m), w_decay=wdec[l], w_iclr=wicl[l],
                  w_gate=wgat[l], bd512=bd512, bd128=bd128, lnx_g=rwkv_lnx_g[l][None],
                  lnx_b=rwkv_lnx_b[l][None])
        gain = fox_out_norm[l].reshape(1, m)
        last = l == depth - 1

        xp = _ffn(xp, ffn1_norm[l][None], f1g, f1u, f1d, layer=l, tm=tm_ffn, tf=tf)
        q, k, kt_all, vt_all, hr, lf, lft_all = _proj(
            xp, mix_norm[l][None], w_pack, b_f[l], tm=blk, seq=sp, prompt=True,
            layer=l, depth=depth, stacked=stacked)
        stacked = (kt_all, vt_all, lft_all)
        hr3 = hr.reshape(bp, sp, HR_WIDTH)
        y_r, s_fin = _rwkv_mix(hr3, jnp.zeros((bp, 1, HR_WIDTH), F32),
                               jnp.zeros((bp, N_PAIRS, PAIR, PAIR), F32), lp, n_valid=sp)
        qa, ka = _fox_bias(q.reshape(bp, sp, m), k.reshape(bp, sp, m), lf.reshape(bp, sp, LANES),
                           place_k, place_q, blk=blk)
        y_f = _fox_prompt(qa, ka, vt_all, gain, layer=l, blk=blk)
        xp = _out_proj(xp, y_r.reshape(npr, m), y_f.reshape(npr, m), wo[l, :m], wo[l, m:], tm=tm_p)
        xp = _ffn(xp, ffn2_norm[l][None], f2g, f2u, f2d, layer=l, tm=tm_ffn, tf=tf)
        xp = _ple(xp, p_prompt.reshape(depth, npr, pd), ple_norm[l][None], pg[l], pu[l], fg,
                  layer=l, tm=tm_p, final=last)
        outs[3].append(_unpair_state(s_fin))
        outs[4].append(_unpack_hr(hr3[:, -1, :]))

        xs = _ffn(xs, ffn1_norm[l][None], f1g, f1u, f1d, layer=l, tm=nsm, tf=tf)
        q, k, v, hr, lf = _proj(xs, mix_norm[l][None], w_pack, b_f[l],
                                tm=nsm, seq=nsm, prompt=False, layer=l)
        hr3 = hr.reshape(bs, ts, HR_WIDTH)
        y_r, s_fin = _rwkv_mix(hr3, _pack_hr(state_shift[l])[:, None, :], _pair_state(state_wkv[l]), lp,
                               n_valid=ts)
        lf3 = lf.reshape(bs, ts, LANES)
        lf_t = _pad_cols(jnp.swapaxes(lf3[:, :, :N_HEADS], 1, 2), LANES)
        y_f = _fox_decode(l, page_table, q.reshape(bs, ts, m), k.reshape(bs, ts, m), v.reshape(bs, ts, m),
                          lf3, lf_t, cache_kt, cache_vt, cache_lft, gain, hm, bd512, npg=npg)
        xs = _out_proj(xs, y_r[:, :ts].reshape(nsm, m), y_f.reshape(nsm, m), wo[l, :m], wo[l, m:], tm=nsm)
        xs = _ffn(xs, ffn2_norm[l][None], f2g, f2u, f2d, layer=l, tm=nsm, tf=tf)
        xs = _ple(xs, p_sample.reshape(depth, nsm, pd), ple_norm[l][None], pg[l], pu[l], fg,
                  layer=l, tm=nsm, final=last)
        outs[5].append(k.reshape(bs, ts, N_HEADS, HEAD_DIM))
        outs[6].append(v.reshape(bs, ts, N_HEADS, HEAD_DIM))
        outs[7].append(lf3[:, :, :N_HEADS])
        outs[8].append(_unpair_state(s_fin))
        outs[9].append(_unpack_hr(hr3[:, -1, :]))

    st = [jnp.stack(o) for o in outs[3:]]
    k_p, v_p = (jnp.transpose(t.reshape(depth, bp, N_HEADS, HEAD_DIM, sp), (0, 1, 4, 2, 3)) for t in stacked[:2])
    lf_p = jnp.swapaxes(stacked[2], 2, 3)
    return (xp.reshape(bp, sp, d), xs.reshape(bs, ts, d), k_p, v_p, lf_p, *st)
```

```python
import functools

import jax
import jax.numpy as jnp
from jax import lax
from jax.experimental import pallas as pl
from jax.experimental.pallas import tpu as pltpu

F32 = jnp.float32
BF16 = jnp.bfloat16

HEAD_DIM = 64
N_HEADS = 8
MIX_HALF = N_HEADS * HEAD_DIM
PAIR = 2 * HEAD_DIM
N_PAIRS = N_HEADS // 2
DECAY_RANK = 64
ICLR_RANK = 64
GATE_RANK = 160
RWKV_PROJ = 3 * MIX_HALF + DECAY_RANK + ICLR_RANK + GATE_RANK
LANES = 128
DECAY_PAD = 128
ICLR_PAD = 128
GATE_PAD = 256
HR_WIDTH = 3 * MIX_HALF + DECAY_PAD + ICLR_PAD + GATE_PAD
RMS_EPS = 1e-6
GN_EPS = 64e-5
NEG_INF = -1e30
LOG2E = 1.4426950408889634
VMEM_LIMIT = 56 * 1024 * 1024

REC_CHUNK = 64
REC_TILE = 256
STATE_TILE = 128
STATE_BATCH = 8
ATT_BLOCK = 256
FFN_TILE = 2048
DEC_PAGES_PER_STEP = 32


def _params(*sem):
    return pltpu.CompilerParams(dimension_semantics=sem, vmem_limit_bytes=VMEM_LIMIT)


def _dot(a, b):
    return jnp.dot(a, b, preferred_element_type=F32)


def _dot_nt(a, b):
    return lax.dot_general(a, b, (((1,), (1,)), ((), ())), preferred_element_type=F32)


def _split3(x):
    x1 = x.astype(BF16)
    r1 = x - x1.astype(F32)
    x2 = r1.astype(BF16)
    r2 = r1 - x2.astype(F32)
    return x1, x2, r2.astype(BF16)


def _dot_x01(x, m01):
    x1, x2, x3 = _split3(x)
    return _dot(x1, m01) + _dot(x2, m01) + _dot(x3, m01)


def _group_sum(x, m01):
    x1 = x.astype(BF16)
    return _dot(x1, m01) + _dot((x - x1.astype(F32)).astype(BF16), m01)


def _dot_01x(m01, x):
    x1, x2, x3 = _split3(x)
    return _dot(m01, x1) + _dot(m01, x2) + _dot(m01, x3)


def _rms(x, g):
    ms = jnp.mean(x * x, axis=-1, keepdims=True)
    return x * lax.rsqrt(ms + RMS_EPS) * g


def _sigmoid(x):
    return 1.0 / (1.0 + jnp.exp(-x))


def _softplus(z):
    return jnp.maximum(z, 0.0) + jnp.log(1.0 + jnp.exp(-jnp.abs(z)))


def _ffn_kernel(x_ref, g_ref, wg_ref, wu_ref, wd_ref, o_ref, xn_ref, *, nf):
    j = pl.program_id(1)

    @pl.when(j == 0)
    def _():
        xn_ref[...] = _rms(x_ref[...], g_ref[...]).astype(BF16)
        o_ref[...] = jnp.zeros_like(o_ref)

    xn = xn_ref[...]
    gate = _dot(xn, wg_ref[0].astype(BF16))
    up = _dot(xn, wu_ref[0].astype(BF16))
    h = (gate * _sigmoid(gate) * up).astype(BF16)
    o_ref[...] += _dot(h, wd_ref[0].astype(BF16))

    @pl.when(j == nf - 1)
    def _():
        o_ref[...] = x_ref[...] + 0.5 * o_ref[...]


def _ffn(x, g, wg, wu, wd, *, layer, tm, tf):
    n, d = x.shape
    f = wg.shape[2]
    nf = f // tf
    return pl.pallas_call(
        functools.partial(_ffn_kernel, nf=nf),
        grid=(n // tm, nf),
        in_specs=[
            pl.BlockSpec((tm, d), lambda i, j: (i, 0)),
            pl.BlockSpec((1, d), lambda i, j: (0, 0)),
            pl.BlockSpec((1, d, tf), lambda i, j: (layer, 0, j)),
            pl.BlockSpec((1, d, tf), lambda i, j: (layer, 0, j)),
            pl.BlockSpec((1, tf, d), lambda i, j: (layer, j, 0)),
        ],
        out_specs=pl.BlockSpec((tm, d), lambda i, j: (i, 0)),
        out_shape=jax.ShapeDtypeStruct((n, d), F32),
        scratch_shapes=[pltpu.VMEM((tm, d), BF16)],
        compiler_params=_params("parallel", "arbitrary"),
        name="ffn",
    )(x, g, wg, wu, wd)


def _proj_kernel(x_ref, g_ref, w_ref, bf_ref, *refs, prompt, n_alias):
    out_refs = refs[n_alias:]
    xn = _rms(x_ref[...], g_ref[...]).astype(BF16)
    h = MIX_HALF
    if prompt:
        q_ref, k_ref, kt_ref, vt_ref, hr_ref, lf_ref, lft_ref, v_ref = out_refs
    else:
        q_ref, k_ref, v_ref, hr_ref, lf_ref = out_refs
    q_ref[...] = _dot(xn, w_ref[0, :, 0:h])
    k_ref[...] = _dot(xn, w_ref[0, :, h:2 * h])
    v_ref[...] = _dot(xn, w_ref[0, :, 2 * h:3 * h])
    hr_ref[...] = _dot(xn, w_ref[0, :, 3 * h:3 * h + HR_WIDTH])
    lf_ref[...] = -_softplus(-(_dot(xn, w_ref[0, :, 3 * h + HR_WIDTH:]) + bf_ref[...]))
    if prompt:
        kt_ref[0, 0] = k_ref[...].T
        vt_ref[0, 0] = v_ref[...].T
        lft_ref[0, 0] = lf_ref[...].T[0:N_HEADS, :]


def _proj(x, g, w, bf, *, tm, seq, prompt, layer=0, depth=1, stacked=None):
    n, d = x.shape
    h = MIX_HALF
    nj = seq // tm
    row = lambda i: (i, 0)
    fixed = lambda i: (0, 0)
    trans = lambda i: (layer, i // nj, 0, i % nj)
    rows = lambda width: (pl.BlockSpec((tm, width), row), jax.ShapeDtypeStruct((n, width), F32))
    cols = lambda height: (pl.BlockSpec((1, 1, height, tm), trans),
                           jax.ShapeDtypeStruct((depth, n // seq, height, seq), F32))
    if prompt:
        outs = [rows(h), rows(h), cols(h), cols(h), rows(HR_WIDTH), rows(LANES), cols(N_HEADS)]
    else:
        outs = [rows(h), rows(h), rows(h), rows(HR_WIDTH), rows(LANES)]
    stacked = tuple(stacked or ())
    n_in = 4
    aliases = {n_in + i: o for i, o in enumerate((2, 3, 6)[:len(stacked)])}
    return pl.pallas_call(
        functools.partial(_proj_kernel, prompt=prompt, n_alias=len(stacked)),
        grid=(n // tm,),
        in_specs=[
            pl.BlockSpec((tm, d), row),
            pl.BlockSpec((1, d), fixed),
            pl.BlockSpec((1,) + w.shape[1:], lambda i: (layer, 0, 0)),
            pl.BlockSpec((1, LANES), fixed),
        ] + [pl.BlockSpec(memory_space=pl.ANY)] * len(stacked),
        out_specs=[o[0] for o in outs],
        out_shape=[o[1] for o in outs],
        scratch_shapes=[pltpu.VMEM((tm, h), F32)] if prompt else [],
        input_output_aliases=aliases,
        compiler_params=_params("parallel"),
        name="proj",
    )(x, g, w, bf, *stacked)


def _rwkv_chunk_kernel(h_ref, hp_ref, p0_ref, mu_ref, w0_ref, a0_ref, kk_ref, ka_ref, rk_ref,
                       wd_ref, wi_ref, wg_ref, bd_ref,
                       qc_o, y0_o, g_o, bo_o, m_o, hh_o, *, chunk, n_valid):
    j = pl.program_id(1)
    C = chunk
    C2 = 2 * C
    nb, t_in, _ = h_ref.shape
    tt = g_o.shape[1]
    m = MIX_HALF
    bd = bd_ref[...]

    def mix_inputs(b):
        h = h_ref[b]
        if t_in < tt:
            h = jnp.concatenate([h, jnp.zeros((tt - t_in, HR_WIDTH), F32)], axis=0)
        prev = jnp.where(j == 0, p0_ref[b], hp_ref[b, 7:8, :])
        trow = lax.broadcasted_iota(jnp.int32, h.shape, 0)
        shifted = jnp.where(trow == 0, prev, pltpu.roll(h, 1, axis=0))
        hs = h + mu_ref[...] * (shifted - h)
        r = hs[:, 0:m]
        k = hs[:, m:2 * m]
        v = hs[:, 2 * m:3 * m]
        o = 3 * m
        d_decay = hs[:, o:o + DECAY_PAD]
        d_iclr = hs[:, o + DECAY_PAD:o + DECAY_PAD + ICLR_PAD]
        d_gate = hs[:, o + DECAY_PAD + ICLR_PAD:]
        w_log = -_softplus(-(w0_ref[...] + _dot(jnp.tanh(d_decay).astype(BF16), wd_ref[...]))) - 0.5
        lw = -jnp.exp(w_log)
        a = _sigmoid(a0_ref[...] + _dot(d_iclr.astype(BF16), wi_ref[...]))
        g_o[b] = _dot(_sigmoid(d_gate).astype(BF16), wg_ref[...])
        kk = k * kk_ref[...]
        kk = kk / jnp.maximum(jnp.sqrt(_group_sum(kk * kk, bd)), 1e-12)
        k = k * (1.0 + (a - 1.0) * ka_ref[...])
        bo_o[b] = _group_sum(r * k * rk_ref[...], bd) * v
        na = -kk
        bb = kk * a
        if n_valid < tt:
            ok = lax.broadcasted_iota(jnp.int32, (tt, m), 0) < n_valid
            zero = lambda x: jnp.where(ok, x, 0.0)
            r, k, v, lw, na, bb = zero(r), zero(k), zero(v), zero(lw), zero(na), zero(bb)
        return r, k, v, lw, na, bb

    row = lax.broadcasted_iota(jnp.int32, (C2, C2), 0)
    col = lax.broadcasted_iota(jnp.int32, (C2, C2), 1)
    rt = row & (C - 1)
    ct = col & (C - 1)
    strict = rt > ct
    incl = rt >= ct
    eye = (row == col).astype(F32)
    ti = lax.broadcasted_iota(jnp.int32, (C, C), 0)
    tj = lax.broadcasted_iota(jnp.int32, (C, C), 1)
    tril01 = (ti >= tj).astype(BF16)
    lane_lo = lax.broadcasted_iota(jnp.int32, (1, PAIR), 1) < HEAD_DIM

    def stack(x):
        return jnp.concatenate([jnp.where(lane_lo, x, 0.0), jnp.where(lane_lo, 0.0, x)], axis=0)

    prob = []
    for b in range(nb):
        r, k, v, lw, na, bb = mix_inputs(b)
        for ci in range(tt // C):
            rows = slice(ci * C, (ci + 1) * C)
            lwc = lw[rows]
            cum = _dot_01x(tril01, lwc)
            cum_end = cum[C - 1:C, :]
            e_neg = jnp.exp(-cum)
            e_end = jnp.exp(cum_end - cum)
            at = na[rows] * jnp.exp(cum - lwc)
            rt_ = r[rows] * jnp.exp(cum)
            bt = bb[rows] * e_neg
            kt = k[rows] * e_neg
            bh = bb[rows] * e_end
            kh = k[rows] * e_end
            wc = jnp.exp(cum_end)
            for hp in range(N_PAIRS):
                sl = slice(hp * PAIR, (hp + 1) * PAIR)
                v2 = stack(v[rows, sl])
                prob.append(dict(
                    b=b, ci=ci, sl=sl, hp=hp, rq=rt_[:, sl], wc=wc[:, sl],
                    at=stack(at[:, sl]).astype(BF16), rt=stack(rt_[:, sl]).astype(BF16),
                    bt=stack(bt[:, sl]).astype(BF16), kt=stack(kt[:, sl]).astype(BF16),
                    bh=stack(bh[:, sl]).astype(BF16), kh=stack(kh[:, sl]).astype(BF16),
                    v=v2.astype(BF16), vt=v2.T.astype(BF16)))

    mm = [_dot_nt(jnp.concatenate([p["at"], p["rt"]], axis=0), jnp.concatenate([p["bt"], p["kt"]], axis=0))
          for p in prob]
    n_ab = [jnp.where(strict, x[0:C2, 0:C2], 0.0) for x in mm]
    a_kr = [jnp.concatenate([jnp.where(strict, x[0:C2, C2:], 0.0),
                             jnp.where(incl, x[C2:, C2:], 0.0)], axis=0).astype(BF16) for x in mm]
    a_rb = [jnp.where(incl, x[C2:, 0:C2], 0.0).astype(BF16) for x in mm]
    kv = [_dot(x, p["v"]) for x, p in zip(a_kr, prob)]
    vtk = [_dot(p["vt"], p["kh"]) for p in prob]

    t_inv = [eye + jnp.where((rt >> 1) == (ct >> 1), x, 0.0) for x in n_ab]
    lvl = 1
    while (1 << lvl) < C:
        sel = ((rt >> (lvl + 1)) == (ct >> (lvl + 1))) & ((rt >> lvl) != (ct >> lvl))
        off = [jnp.where(sel, x, 0.0).astype(BF16) for x in n_ab]
        tb = [x.astype(BF16) for x in t_inv]
        xx = [_dot(x, y).astype(BF16) for x, y in zip(off, tb)]
        t_inv = [x + _dot(y, z) for x, y, z in zip(t_inv, tb, xx)]
        lvl += 1

    pu = [_dot(t.astype(BF16), jnp.concatenate([p["at"], x[0:C2].astype(BF16)], axis=1))
          for t, p, x in zip(t_inv, prob, kv)]
    ab = [_dot(x, y.astype(BF16)) for x, y in zip(a_rb, pu)]
    gb = [_dot(x.T.astype(BF16), p["bh"]) for x, p in zip(pu, prob)]
    for p, x_ab, x_kv, x_gb, x_vtk in zip(prob, ab, kv, gb, vtk):
        b = p["b"]
        rows = slice(p["ci"] * C, (p["ci"] + 1) * C)
        qc_o[b, rows, p["sl"]] = (p["rq"] + x_ab[0:C, 0:PAIR] + x_ab[C:, 0:PAIR]).astype(BF16)
        y0 = x_ab[:, PAIR:] + x_kv[C2:]
        y0_o[b, rows, p["sl"]] = y0[0:C] + y0[C:]
        m_o[b, p["ci"], p["hp"]] = (x_gb[0:PAIR] + eye * p["wc"]).astype(BF16)
        hh_o[b, p["ci"], p["hp"]] = x_gb[PAIR:] + x_vtk


def _rwkv_chunk(hr, prev0, lp, *, nb, tt, chunk, n_valid):
    b, t_in, _ = hr.shape
    t = max(t_in, tt)
    m = MIX_HALF
    g = tt // chunk
    tile = lambda i, j: (i, j, 0)
    fixed = lambda i, j: (0, 0)
    vec = pl.BlockSpec((1, m), fixed)
    tok = (pl.BlockSpec((nb, tt, m), tile), jax.ShapeDtypeStruct((b, t, m), F32))
    tok16 = (tok[0], jax.ShapeDtypeStruct((b, t, m), BF16))
    mat = lambda dt: (pl.BlockSpec((nb, g, N_PAIRS, PAIR, PAIR), lambda i, j: (i, j, 0, 0, 0)),
                      jax.ShapeDtypeStruct((b, t // chunk, N_PAIRS, PAIR, PAIR), dt))
    outs = [tok16, tok, tok, tok, mat(BF16), mat(F32)]
    return pl.pallas_call(
        functools.partial(_rwkv_chunk_kernel, chunk=chunk, n_valid=n_valid),
        grid=(b // nb, t // tt),
        in_specs=[
            pl.BlockSpec((nb, min(tt, t_in), HR_WIDTH), tile),
            pl.BlockSpec((nb, 8, HR_WIDTH), lambda i, j: (i, jnp.maximum(j * (tt // 8) - 1, 0), 0)),
            pl.BlockSpec((nb, 1, HR_WIDTH), lambda i, j: (i, 0, 0)),
            pl.BlockSpec((1, HR_WIDTH), fixed),
            vec, vec, vec, vec, vec,
            pl.BlockSpec((DECAY_PAD, m), fixed),
            pl.BlockSpec((ICLR_PAD, m), fixed),
            pl.BlockSpec((GATE_PAD, m), fixed),
            pl.BlockSpec((m, m), fixed),
        ],
        out_specs=[o[0] for o in outs],
        out_shape=[o[1] for o in outs],
        compiler_params=_params("parallel", "arbitrary"),
        name="rwkv_chunk",
    )(hr, hr, prev0, lp["mu"], lp["w0"], lp["a0"], lp["k_k"], lp["k_a"], lp["r_k"],
      lp["w_decay"], lp["w_iclr"], lp["w_gate"], lp["bd512"])


def _rwkv_state_kernel(qc_ref, y0_ref, g_ref, bo_ref, m_ref, hh_ref, s0_ref, lng_ref, lnb_ref, bd_ref,
                       y_ref, sf_ref, s_ref, *, chunk, nsteps):
    j = pl.program_id(1)
    C = chunk
    nb, tt, _ = qc_ref.shape
    bd = bd_ref[...]

    @pl.when(j == 0)
    def _():
        s_ref[...] = s0_ref[...]

    cells = [(b, hp) for b in range(nb) for hp in range(N_PAIRS)]
    state = [s_ref[b, hp] for b, hp in cells]
    for ci in range(tt // C):
        rows = slice(ci * C, (ci + 1) * C)
        sb = [s.astype(BF16) for s in state]
        ys = [_dot_nt(qc_ref[b, rows, hp * PAIR:(hp + 1) * PAIR].astype(BF16), s)
              + y0_ref[b, rows, hp * PAIR:(hp + 1) * PAIR] for (b, hp), s in zip(cells, sb)]
        state = [_dot(s, m_ref[b, ci, hp]) + hh_ref[b, ci, hp] for (b, hp), s in zip(cells, sb)]
        means = [_group_sum(y, bd) * (1.0 / HEAD_DIM) for y in ys]
        yc = [y - mu for y, mu in zip(ys, means)]
        var = [_group_sum(y * y, bd) * (1.0 / HEAD_DIM) for y in yc]
        for (b, hp), y, vr in zip(cells, yc, var):
            sl = slice(hp * PAIR, (hp + 1) * PAIR)
            yn = y * lax.rsqrt(vr + GN_EPS) * lng_ref[:, sl] + lnb_ref[:, sl]
            y_ref[b, rows, sl] = ((yn + bo_ref[b, rows, sl]) * g_ref[b, rows, sl]).astype(BF16)
    for (b, hp), s in zip(cells, state):
        s_ref[b, hp] = s

    @pl.when(j == nsteps - 1)
    def _():
        sf_ref[...] = s_ref[...]


def _rwkv_state(qc, y0, g, bonus, mm, hh, s0, lp, *, nb, tt, chunk):
    bsz, t, m = qc.shape
    gch = tt // chunk
    nsteps = t // tt
    tile = pl.BlockSpec((nb, tt, m), lambda i, j: (i, j, 0))
    mat = pl.BlockSpec((nb, gch, N_PAIRS, PAIR, PAIR), lambda i, j: (i, j, 0, 0, 0))
    st = pl.BlockSpec((nb, N_PAIRS, PAIR, PAIR), lambda i, j: (i, 0, 0, 0))
    vec = pl.BlockSpec((1, m), lambda i, j: (0, 0))
    return pl.pallas_call(
        functools.partial(_rwkv_state_kernel, chunk=chunk, nsteps=nsteps),
        grid=(bsz // nb, nsteps),
        in_specs=[tile, tile, tile, tile, mat, mat, st, vec, vec,
                  pl.BlockSpec((PAIR, PAIR), lambda i, j: (0, 0))],
        out_specs=[tile, st],
        out_shape=[jax.ShapeDtypeStruct((bsz, t, m), BF16),
                   jax.ShapeDtypeStruct((bsz, N_PAIRS, PAIR, PAIR), F32)],
        scratch_shapes=[pltpu.VMEM((nb, N_PAIRS, PAIR, PAIR), F32)],
        compiler_params=_params("parallel", "arbitrary"),
        name="rwkv_state",
    )(qc, y0, g, bonus, mm, hh, s0, lp["lnx_g"], lp["lnx_b"], lp["bd128"])


def _bias_lane(head):
    return HEAD_DIM if head % 2 == 0 else 0


def _fox_bias_kernel(q_ref, k_ref, lf_ref, pk_ref, pq_ref, qa_ref, ka_ref, *, blk, nblk):
    ti = lax.broadcasted_iota(jnp.int32, (blk, blk), 0)
    tj = lax.broadcasted_iota(jnp.int32, (blk, blk), 1)
    tril01 = (ti >= tj).astype(BF16)
    lane = lax.broadcasted_iota(jnp.int32, (1, PAIR), 1)
    carry = jnp.zeros((1, LANES), F32)
    for i in range(nblk):
        sl = slice(i * blk, (i + 1) * blk)
        c = _dot_01x(tril01, lf_ref[0, sl, :]) + carry
        carry = c[blk - 1:blk, :]
        p1, p2, p3 = (jnp.where(lane < N_HEADS, x.astype(F32), 0.0) for x in _split3(c * LOG2E))
        pieces = (p1 + pltpu.roll(p2, N_HEADS, axis=1) + pltpu.roll(p3, 2 * N_HEADS, axis=1)).astype(BF16)
        for hp in range(N_PAIRS):
            ps = slice(hp * PAIR, (hp + 1) * PAIR)
            kp = k_ref[0, sl, ps]
            qp = q_ref[0, sl, ps] * (HEAD_DIM ** -0.5 * LOG2E)
            k_aug = _dot(pieces, pk_ref[hp])
            q_aug = _dot(pieces, pq_ref[hp])
            for half in range(2):
                h = 2 * hp + half
                own = (lane < HEAD_DIM) if half == 0 else (lane >= HEAD_DIM)
                off = lane - _bias_lane(h)
                hs = slice(h * PAIR, (h + 1) * PAIR)
                aug = slice(half * PAIR, (half + 1) * PAIR)
                ka_ref[0, sl, hs] = jnp.where(
                    own, kp, jnp.where((off >= 3) & (off < 6), 1.0, k_aug[:, aug])).astype(BF16)
                qa_ref[0, sl, hs] = jnp.where(
                    own, qp, jnp.where((off >= 0) & (off < 3), 1.0, q_aug[:, aug])).astype(BF16)


def _bias_placement():
    src = jnp.arange(LANES)
    piece, head = src // N_HEADS, src % N_HEADS
    dst = jnp.arange(2 * PAIR)
    pk, pq = [], []
    for hp in range(N_PAIRS):
        k_hit = jnp.zeros((LANES, 2 * PAIR), bool)
        q_hit = jnp.zeros((LANES, 2 * PAIR), bool)
        for half in range(2):
            h = 2 * hp + half
            mine = ((head == h) & (piece < 3))[:, None]
            lane0 = half * PAIR + _bias_lane(h)
            k_hit |= mine & (dst[None, :] == lane0 + piece[:, None])
            q_hit |= mine & (dst[None, :] == lane0 + 3 + piece[:, None])
        pk.append(jnp.where(k_hit, -1.0, 0.0))
        pq.append(jnp.where(q_hit, 1.0, 0.0))
    return jnp.stack(pk).astype(BF16), jnp.stack(pq).astype(BF16)


def _fox_bias(q, k, lf, place_k, place_q, *, blk):
    b, s, m = q.shape
    wide = N_HEADS * PAIR
    row = lambda width: pl.BlockSpec((1, s, width), lambda i: (i, 0, 0))
    place = pl.BlockSpec(place_k.shape, lambda i: (0, 0, 0))
    out = jax.ShapeDtypeStruct((b, s, wide), BF16)
    return pl.pallas_call(
        functools.partial(_fox_bias_kernel, blk=blk, nblk=s // blk),
        grid=(b,),
        in_specs=[row(m), row(m), row(LANES), place, place],
        out_specs=[row(wide), row(wide)],
        out_shape=[out, out],
        compiler_params=_params("parallel"),
        name="fox_bias",
    )(q, k, lf, place_k, place_q)


def _fox_prompt_kernel(qa_ref, ka_ref, vt_ref, gain_ref, o_ref, m_s, l_s, acc_s, st_s, *, blk):
    qi = pl.program_id(1)
    heads = range(N_HEADS)
    m_s[...] = jnp.full(m_s.shape, NEG_INF, F32)
    l_s[...] = jnp.zeros_like(l_s)
    acc_s[...] = jnp.zeros_like(acc_s)

    def keys(j):
        return pl.ds(pl.multiple_of(j * blk, blk), blk)

    def scores(j, slot, diagonal):
        ks = keys(j)
        st = [_dot_nt(ka_ref[0, ks, h * PAIR:(h + 1) * PAIR], qa_ref[0, :, h * PAIR:(h + 1) * PAIR])
              for h in heads]
        if diagonal:
            ki = lax.broadcasted_iota(jnp.int32, (blk, blk), 0)
            qj = lax.broadcasted_iota(jnp.int32, (blk, blk), 1)
            st = [jnp.where(ki <= qj, x, NEG_INF) for x in st]
        for h in heads:
            st_s[slot, h] = st[h]

    def consume(j, slot):
        ks = keys(j)
        st = [st_s[slot, h] for h in heads]
        m_old = [m_s[h:h + 1, :] for h in heads]
        m_new = [jnp.maximum(mo, jnp.max(x, axis=0, keepdims=True)) for mo, x in zip(m_old, st)]
        p = [jnp.exp2(x - mn) for x, mn in zip(st, m_new)]
        pv = [_dot(vt_ref[0, 0, h * HEAD_DIM:(h + 1) * HEAD_DIM, ks].astype(BF16), x.astype(BF16))
              for h, x in zip(heads, p)]
        for h in heads:
            alpha = jnp.exp2(m_old[h] - m_new[h])
            m_s[h:h + 1, :] = m_new[h]
            l_s[h:h + 1, :] = alpha * l_s[h:h + 1, :] + jnp.sum(p[h], axis=0, keepdims=True)
            rows = slice(h * HEAD_DIM, (h + 1) * HEAD_DIM)
            acc_s[rows, :] = alpha * acc_s[rows, :] + pv[h]

    scores(qi, 0, True)

    def body(u, carry):
        j = 2 * u
        scores(j, 1, False)
        consume(jnp.where(u == 0, qi, j - 1), 0)
        scores(j + 1, 0, False)
        consume(j, 1)
        return carry

    lax.fori_loop(0, qi // 2, body, 0)
    odd = (qi & 1) == 1

    @pl.when(odd)
    def _():
        scores(qi - 1, 1, False)
        consume(jnp.where(qi == 1, qi, qi - 2), 0)
        consume(qi - 1, 1)

    @pl.when(jnp.logical_not(odd))
    def _():
        consume(jnp.where(qi == 0, qi, qi - 1), 0)

    for hp in range(N_PAIRS):
        halves = []
        for h in (2 * hp, 2 * hp + 1):
            o = acc_s[h * HEAD_DIM:(h + 1) * HEAD_DIM, :] / l_s[h:h + 1, :]
            ms = jnp.mean(o * o, axis=0, keepdims=True)
            halves.append(o * lax.rsqrt(ms + RMS_EPS))
        sl = slice(hp * PAIR, (hp + 1) * PAIR)
        o_ref[0, :, sl] = (jnp.concatenate(halves, axis=0).T * gain_ref[:, sl]).astype(BF16)


def _fox_prompt(qa, ka, vt_all, gain, *, layer, blk):
    b, s, wide = qa.shape
    m = MIX_HALF
    nb = s // blk
    return pl.pallas_call(
        functools.partial(_fox_prompt_kernel, blk=blk),
        grid=(b, nb),
        in_specs=[
            pl.BlockSpec((1, blk, wide), lambda i, j: (i, j, 0)),
            pl.BlockSpec((1, s, wide), lambda i, j: (i, 0, 0)),
            pl.BlockSpec((1, 1, m, s), lambda i, j: (layer, i, 0, 0)),
            pl.BlockSpec((1, m), lambda i, j: (0, 0)),
        ],
        out_specs=pl.BlockSpec((1, blk, m), lambda i, j: (i, j, 0)),
        out_shape=jax.ShapeDtypeStruct((b, s, m), BF16),
        scratch_shapes=[
            pltpu.VMEM((N_HEADS, blk), F32),
            pltpu.VMEM((N_HEADS, blk), F32),
            pltpu.VMEM((m, blk), F32),
            pltpu.VMEM((2, N_HEADS, blk, blk), F32),
        ],
        compiler_params=_params("parallel", "arbitrary"),
        name="fox_prompt",
    )(qa, ka, vt_all, gain)


def _fox_decode_kernel(pt_ref, q_ref, kn_ref, vn_ref, lfn_ref, lfnt_ref, gain_ref, hm_ref, bd_ref, *rest,
                       npg, nsteps, tnew):
    k_pages = rest[0:npg]
    v_pages = rest[npg:2 * npg]
    lft_ref = rest[2 * npg]
    o_ref, m_s, l_s, acc_s, carry_s, qbd_s, cn_s = rest[2 * npg + 1:]
    seq = pl.program_id(0)
    g = pl.program_id(1)
    nrow = tnew * N_HEADS
    page = k_pages[0].shape[3]
    li = lax.broadcasted_iota(jnp.int32, (LANES, LANES), 0)
    lj = lax.broadcasted_iota(jnp.int32, (LANES, LANES), 1)
    hm = hm_ref[...]

    def tile_rows(x):
        return jnp.concatenate([x] * tnew, axis=0)

    @pl.when(g == 0)
    def _():
        q = q_ref[0] * (HEAD_DIM ** -0.5)
        qbd = jnp.concatenate([jnp.broadcast_to(q[t:t + 1, :], hm.shape) * hm for t in range(tnew)], axis=0)
        qbd_s[...] = qbd.astype(BF16)
        cn = lfn_ref[0]
        trow = lax.broadcasted_iota(jnp.int32, cn.shape, 0)
        sh = 1
        while sh < tnew:
            cn = cn + jnp.where(trow >= sh, pltpu.roll(cn, sh, axis=0), 0.0)
            sh *= 2
        hsel = (lax.broadcasted_iota(jnp.int32, (N_HEADS, LANES), 0)
                == lax.broadcasted_iota(jnp.int32, (N_HEADS, LANES), 1)).astype(F32)
        cn_rows = jnp.concatenate(
            [jnp.sum(jnp.broadcast_to(cn[t:t + 1, :], hsel.shape) * hsel, axis=1, keepdims=True)
             for t in range(tnew)], axis=0)
        cn_s[...] = cn_rows
        cnt = _dot_x01(lfnt_ref[0], (li <= lj).astype(BF16))
        pad = jnp.zeros((page - tnew, MIX_HALF), F32)
        kn = jnp.concatenate([kn_ref[0], pad], axis=0).astype(BF16)
        vn = jnp.concatenate([vn_ref[0], pad], axis=0).astype(BF16)
        s = _dot_nt(qbd_s[...], kn) + (cn_rows - tile_rows(cnt))
        rtok = lax.broadcasted_iota(jnp.int32, (nrow, LANES), 0) >> 3
        ktok = lax.broadcasted_iota(jnp.int32, (nrow, LANES), 1)
        s = jnp.where(ktok <= rtok, s, NEG_INF)
        m = jnp.max(s, axis=1, keepdims=True)
        p = jnp.exp(s - m)
        m_s[...] = m
        l_s[...] = jnp.sum(p, axis=1, keepdims=True)
        acc_s[...] = _dot(p.astype(BF16), vn)
        carry_s[...] = jnp.zeros_like(carry_s)

    slots = range(npg)
    nr = npg * N_HEADS
    lf_all = jnp.concatenate([lft_ref[0, pt_ref[seq, (nsteps - 1 - g) * npg + i]] for i in slots], axis=0)
    ri = lax.broadcasted_iota(jnp.int32, (nr, nr), 0)
    rj = lax.broadcasted_iota(jnp.int32, (nr, nr), 1)
    later_page = (((ri & (N_HEADS - 1)) == (rj & (N_HEADS - 1))) & (rj > ri)).astype(BF16)
    total = jnp.sum(lf_all, axis=1, keepdims=True)
    after = jnp.sum(_dot_01x(later_page, lf_all), axis=1, keepdims=True)
    carry = carry_s[...]
    bias_all = _dot_x01(lf_all, (li > lj).astype(BF16)) + (after + jnp.concatenate([carry] * npg, axis=0))
    carry_s[...] = carry + (after + total)[0:N_HEADS]
    qbd = qbd_s[...]
    cn_rows = cn_s[...]
    kb = [k_pages[i][0, 0].astype(BF16) for i in slots]
    sc = [_dot(qbd, x) for x in kb]
    s = jnp.concatenate([x + (tile_rows(bias_all[i * N_HEADS:(i + 1) * N_HEADS, :]) + cn_rows)
                         for i, x in zip(slots, sc)], axis=1)
    m_old = m_s[...]
    m_new = jnp.maximum(m_old, jnp.max(s, axis=1, keepdims=True))
    alpha = jnp.exp(m_old - m_new)
    p = jnp.exp(s - m_new)
    m_s[...] = m_new
    l_s[...] = alpha * l_s[...] + jnp.sum(p, axis=1, keepdims=True)
    vb = [v_pages[i][0, 0].astype(BF16) for i in slots]
    pv = [_dot_nt(p[:, i * page:(i + 1) * page].astype(BF16), x) for i, x in zip(slots, vb)]
    while len(pv) > 1:
        pv = [a + b for a, b in zip(pv[0::2], pv[1::2])] + ([pv[-1]] if len(pv) % 2 else [])
    acc_s[...] = alpha * acc_s[...] + pv[0]

    @pl.when(g == nsteps - 1)
    def _():
        o_rows = acc_s[...] / l_s[...]
        o = jnp.concatenate(
            [jnp.sum(o_rows[t * N_HEADS:(t + 1) * N_HEADS, :] * hm, axis=0, keepdims=True) for t in range(tnew)],
            axis=0)
        ms = _dot_x01(o * o, bd_ref[...]) * (1.0 / HEAD_DIM)
        o_ref[0] = o * lax.rsqrt(ms + RMS_EPS) * gain_ref[...]


def _fox_decode(layer, page_table, q, kn, vn, lfn, lfn_t, cache_k, cache_v, cache_lft, gain, hm, bd, *, npg):
    b, tnew, m = q.shape
    n_pages = page_table.shape[1]
    page = cache_k.shape[3]
    nsteps = n_pages // npg
    nrow = tnew * N_HEADS

    def tok(i, g, pt):
        return (i, 0, 0)

    def fixed(i, g, pt):
        return (0, 0)

    def page_map(slot):
        return lambda i, g, pt: (layer, pt[i, (nsteps - 1 - g) * npg + slot], 0, 0)

    kv_specs = [pl.BlockSpec((1, 1, m, page), page_map(s)) for s in range(npg)]
    n_pool = cache_lft.shape[1]
    lf_spec = pl.BlockSpec((1, n_pool, N_HEADS, page), lambda i, g, pt: (layer, 0, 0, 0),
                           pipeline_mode=pl.Buffered(1))
    grid_spec = pltpu.PrefetchScalarGridSpec(
        num_scalar_prefetch=1,
        grid=(b, nsteps),
        in_specs=[
            pl.BlockSpec((1, tnew, m), tok),
            pl.BlockSpec((1, tnew, m), tok),
            pl.BlockSpec((1, tnew, m), tok),
            pl.BlockSpec((1, tnew, LANES), tok),
            pl.BlockSpec((1, N_HEADS, LANES), tok),
            pl.BlockSpec((1, m), fixed),
            pl.BlockSpec((N_HEADS, m), fixed),
            pl.BlockSpec((m, m), fixed),
        ] + kv_specs + kv_specs + [lf_spec],
        out_specs=pl.BlockSpec((1, tnew, m), tok),
        scratch_shapes=[
            pltpu.VMEM((nrow, 1), F32),
            pltpu.VMEM((nrow, 1), F32),
            pltpu.VMEM((nrow, m), F32),
            pltpu.VMEM((N_HEADS, 1), F32),
            pltpu.VMEM((nrow, m), BF16),
            pltpu.VMEM((nrow, 1), F32),
        ],
    )
    return pl.pallas_call(
        functools.partial(_fox_decode_kernel, npg=npg, nsteps=nsteps, tnew=tnew),
        grid_spec=grid_spec,
        out_shape=jax.ShapeDtypeStruct((b, tnew, m), F32),
        compiler_params=_params("parallel", "arbitrary"),
        name="fox_decode",
    )(page_table, q, kn, vn, lfn, lfn_t, gain, hm, bd,
      *([cache_k] * npg), *([cache_v] * npg), cache_lft)


def _out_proj_kernel(x_ref, yr_ref, yf_ref, wr_ref, wf_ref, o_ref):
    o_ref[...] = (x_ref[...] + _dot(yr_ref[...].astype(BF16), wr_ref[...])
                  + _dot(yf_ref[...].astype(BF16), wf_ref[...]))


def _out_proj(x, yr, yf, wr, wf, *, tm):
    n, d = x.shape
    m = yr.shape[1]
    row = lambda i: (i, 0)
    fixed = lambda i: (0, 0)
    return pl.pallas_call(
        _out_proj_kernel,
        grid=(n // tm,),
        in_specs=[pl.BlockSpec((tm, d), row), pl.BlockSpec((tm, m), row), pl.BlockSpec((tm, m), row),
                  pl.BlockSpec((m, d), fixed), pl.BlockSpec((m, d), fixed)],
        out_specs=pl.BlockSpec((tm, d), row),
        out_shape=jax.ShapeDtypeStruct((n, d), F32),
        compiler_params=_params("parallel"),
        name="out_proj",
    )(x, yr, yf, wr, wf)


def _ple_kernel(x_ref, p_ref, g_ref, wg_ref, wu_ref, fg_ref, o_ref, *, final):
    x = x_ref[...]
    gate = _sigmoid(_dot(_rms(x, g_ref[...]).astype(BF16), wg_ref[...]))
    y = x + gate * _dot(p_ref[0].astype(BF16), wu_ref[...])
    o_ref[...] = _rms(y, fg_ref[...]) if final else y


def _ple(x, p, g, wg, wu, fg, *, layer, tm, final):
    n, d = x.shape
    pd = p.shape[2]
    row = lambda i: (i, 0)
    fixed = lambda i: (0, 0)
    return pl.pallas_call(
        functools.partial(_ple_kernel, final=final),
        grid=(n // tm,),
        in_specs=[pl.BlockSpec((tm, d), row), pl.BlockSpec((1, tm, pd), lambda i: (layer, i, 0)),
                  pl.BlockSpec((1, d), fixed),
                  pl.BlockSpec((d, d), fixed), pl.BlockSpec((pd, d), fixed), pl.BlockSpec((1, d), fixed)],
        out_specs=pl.BlockSpec((tm, d), row),
        out_shape=jax.ShapeDtypeStruct((n, d), F32),
        compiler_params=_params("parallel"),
        name="ple",
    )(x, p, g, wg, wu, fg)


def _pad_cols(w, width):
    return jnp.pad(w, [(0, 0)] * (w.ndim - 1) + [(0, width - w.shape[-1])])


def _pad_rows(w, height):
    return jnp.pad(w, [(0, 0)] * (w.ndim - 2) + [(0, height - w.shape[-2]), (0, 0)])


def _pack_hr(h):
    m = MIX_HALF
    o = 3 * m
    return jnp.concatenate([
        h[..., :o],
        _pad_cols(h[..., o:o + DECAY_RANK], DECAY_PAD),
        _pad_cols(h[..., o + DECAY_RANK:o + DECAY_RANK + ICLR_RANK], ICLR_PAD),
        _pad_cols(h[..., o + DECAY_RANK + ICLR_RANK:], GATE_PAD)], axis=-1)


def _unpack_hr(h):
    m = MIX_HALF
    o = 3 * m
    return jnp.concatenate([
        h[..., :o],
        h[..., o:o + DECAY_RANK],
        h[..., o + DECAY_PAD:o + DECAY_PAD + ICLR_RANK],
        h[..., o + DECAY_PAD + ICLR_PAD:o + DECAY_PAD + ICLR_PAD + GATE_RANK]], axis=-1)


def _pair_state(s):
    b = s.shape[0]
    s = s.reshape(b, N_PAIRS, 2, HEAD_DIM, HEAD_DIM)
    z = jnp.zeros_like(s[:, :, 0])
    top = jnp.concatenate([s[:, :, 0], z], axis=-1)
    bot = jnp.concatenate([z, s[:, :, 1]], axis=-1)
    return jnp.concatenate([top, bot], axis=-2)


def _unpair_state(s):
    b = s.shape[0]
    d = HEAD_DIM
    return jnp.stack([s[:, :, :d, :d], s[:, :, d:, d:]], axis=2).reshape(b, N_HEADS, d, d)


def _block_diag01(n):
    i = jnp.arange(n) // HEAD_DIM
    return (i[:, None] == i[None, :]).astype(BF16)


def _rwkv_mix(hr3, prev0, s0, lp, *, n_valid):
    bsz, t, _ = hr3.shape
    chunk = REC_CHUNK
    t = max(t, chunk)
    tt = min(t, REC_TILE)
    nb = max(1, min(bsz, REC_TILE // tt))
    qc, y0, g, bonus, mm, hh = _rwkv_chunk(hr3, prev0, lp, nb=nb, tt=tt, chunk=chunk, n_valid=n_valid)
    return _rwkv_state(qc, y0, g, bonus, mm, hh, s0, lp, nb=min(bsz, STATE_BATCH), tt=min(t, STATE_TILE),
                       chunk=chunk)


def kernel(x_prompt, x_sample, cache_k, cache_v, cache_logf, state_wkv, state_shift, page_table, p_prompt, p_sample, ffn1_norm, ffn1_w_gate, ffn1_w_up, ffn1_w_down, mix_norm, w_in, rwkv_mu, rwkv_w0, rwkv_w_decay, rwkv_a0, rwkv_w_iclr, rwkv_w_gate, rwkv_k_k, rwkv_k_a, rwkv_r_k, rwkv_lnx_g, rwkv_lnx_b, fox_b_f, fox_out_norm, w_out, ffn2_norm, ffn2_w_gate, ffn2_w_up, ffn2_w_down, ple_norm, ple_w_gate, ple_w_up, final_norm):
    depth = w_in.shape[0]
    bp, sp, d = x_prompt.shape
    bs, ts, _ = x_sample.shape
    m = MIX_HALF
    n_pool, page = cache_k.shape[1], cache_k.shape[2]
    npr, nsm = bp * sp, bs * ts
    pd = p_prompt.shape[-1]

    fox_cols = w_in[:, :, RWKV_PROJ:]
    w_pack = jnp.concatenate([
        fox_cols[:, :, :3 * m],
        _pack_hr(w_in[:, :, :RWKV_PROJ]),
        _pad_cols(fox_cols[:, :, 3 * m:], LANES)], axis=-1).astype(BF16)
    b_f = _pad_cols(fox_b_f, LANES)[:, None, :]
    bd512 = _block_diag01(m)
    bd128 = _block_diag01(PAIR)
    hm = (jnp.arange(m)[None, :] // HEAD_DIM == jnp.arange(N_HEADS)[:, None]).astype(F32)
    bf = lambda w: w.astype(BF16)
    f1g, f1u, f1d = ffn1_w_gate, ffn1_w_up, ffn1_w_down
    f2g, f2u, f2d = ffn2_w_gate, ffn2_w_up, ffn2_w_down
    wo = bf(w_out)
    pg, pu = bf(ple_w_gate), bf(ple_w_up)
    wdec = bf(_pad_rows(rwkv_w_decay, DECAY_PAD))
    wicl = bf(_pad_rows(rwkv_w_iclr, ICLR_PAD))
    wgat = bf(_pad_rows(rwkv_w_gate, GATE_PAD))
    mu = _pack_hr(rwkv_mu)
    cache_kt = jnp.transpose(cache_k, (0, 1, 3, 4, 2)).reshape(depth, n_pool, m, page)
    cache_vt = jnp.transpose(cache_v, (0, 1, 3, 4, 2)).reshape(depth, n_pool, m, page)
    cache_lft = jnp.swapaxes(cache_logf, 2, 3)
    fg = final_norm[None, :]
    place_k, place_q = _bias_placement()

    tm_p = 1024 if npr % 1024 == 0 else ATT_BLOCK
    tm_ffn = FFN_TILE if npr % FFN_TILE == 0 else tm_p
    stacked = None
    tf = 256
    blk = ATT_BLOCK
    npg = min(DEC_PAGES_PER_STEP, page_table.shape[1])

    xp = x_prompt.reshape(npr, d)
    xs = x_sample.reshape(nsm, d)
    outs = [[] for _ in range(10)]
    for l in range(depth):
        lp = dict(mu=mu[l][None], w0=rwkv_w0[l][None], a0=rwkv_a0[l][None], k_k=rwkv_k_k[l][None],
                  k_a=rwkv_k_a[l][None], r_k=rwkv_r_k[l].reshape(1, m), w_decay=wdec[l], w_iclr=wicl[l],
                  w_gate=wgat[l], bd512=bd512, bd128=bd128, lnx_g=rwkv_lnx_g[l][None],
                  lnx_b=rwkv_lnx_b[l][None])
        gain = fox_out_norm[l].reshape(1, m)
        last = l == depth - 1

        xp = _ffn(xp, ffn1_norm[l][None], f1g, f1u, f1d, layer=l, tm=tm_ffn, tf=tf)
        q, k, kt_all, vt_all, hr, lf, lft_all = _proj(
            xp, mix_norm[l][None], w_pack, b_f[l], tm=blk, seq=sp, prompt=True,
            layer=l, depth=depth, stacked=stacked)
        stacked = (kt_all, vt_all, lft_all)
        hr3 = hr.reshape(bp, sp, HR_WIDTH)
        y_r, s_fin = _rwkv_mix(hr3, jnp.zeros((bp, 1, HR_WIDTH), F32),
                               jnp.zeros((bp, N_PAIRS, PAIR, PAIR), F32), lp, n_valid=sp)
        qa, ka = _fox_bias(q.reshape(bp, sp, m), k.reshape(bp, sp, m), lf.reshape(bp, sp, LANES),
                           place_k, place_q, blk=blk)
        y_f = _fox_prompt(qa, ka, vt_all, gain, layer=l, blk=blk)
        xp = _out_proj(xp, y_r.reshape(npr, m), y_f.reshape(npr, m), wo[l, :m], wo[l, m:], tm=tm_p)
        xp = _ffn(xp, ffn2_norm[l][None], f2g, f2u, f2d, layer=l, tm=tm_ffn, tf=tf)
        xp = _ple(xp, p_prompt.reshape(depth, npr, pd), ple_norm[l][None], pg[l], pu[l], fg,
                  layer=l, tm=tm_p, final=last)
        outs[3].append(_unpair_state(s_fin))
        outs[4].append(_unpack_hr(hr3[:, -1, :]))

        xs = _ffn(xs, ffn1_norm[l][None], f1g, f1u, f1d, layer=l, tm=nsm, tf=tf)
        q, k, v, hr, lf = _proj(xs, mix_norm[l][None], w_pack, b_f[l],
                                tm=nsm, seq=nsm, prompt=False, layer=l)
        hr3 = hr.reshape(bs, ts, HR_WIDTH)
        y_r, s_fin = _rwkv_mix(hr3, _pack_hr(state_shift[l])[:, None, :], _pair_state(state_wkv[l]), lp,
                               n_valid=ts)
        lf3 = lf.reshape(bs, ts, LANES)
        lf_t = _pad_cols(jnp.swapaxes(lf3[:, :, :N_HEADS], 1, 2), LANES)
        y_f = _fox_decode(l, page_table, q.reshape(bs, ts, m), k.reshape(bs, ts, m), v.reshape(bs, ts, m),
                          lf3, lf_t, cache_kt, cache_vt, cache_lft, gain, hm, bd512, npg=npg)
        xs = _out_proj(xs, y_r[:, :ts].reshape(nsm, m), y_f.reshape(nsm, m), wo[l, :m], wo[l, m:], tm=nsm)
        xs = _ffn(xs, ffn2_norm[l][None], f2g, f2u, f2d, layer=l, tm=nsm, tf=tf)
        xs = _ple(xs, p_sample.reshape(depth, nsm, pd), ple_norm[l][None], pg[l], pu[l], fg,
                  layer=l, tm=nsm, final=last)
        outs[5].append(k.reshape(bs, ts, N_HEADS, HEAD_DIM))
        outs[6].append(v.reshape(bs, ts, N_HEADS, HEAD_DIM))
        outs[7].append(lf3[:, :, :N_HEADS])
        outs[8].append(_unpair_state(s_fin))
        outs[9].append(_unpack_hr(hr3[:, -1, :]))

    st = [jnp.stack(o) for o in outs[3:]]
    k_p, v_p = (jnp.transpose(t.reshape(depth, bp, N_HEADS, HEAD_DIM, sp), (0, 1, 4, 2, 3)) for t in stacked[:2])
    lf_p = jnp.swapaxes(stacked[2], 2, 3)
    return (xp.reshape(bp, sp, d), xs.reshape(bs, ts, d), k_p, v_p, lf_p, *st)
```

```python
import functools

import jax
import jax.numpy as jnp
from jax import lax
from jax.experimental import pallas as pl
from jax.experimental.pallas import tpu as pltpu

F32 = jnp.float32
BF16 = jnp.bfloat16

HEAD_DIM = 64
N_HEADS = 8
MIX_HALF = N_HEADS * HEAD_DIM
PAIR = 2 * HEAD_DIM
N_PAIRS = N_HEADS // 2
DECAY_RANK = 64
ICLR_RANK = 64
GATE_RANK = 160
RWKV_PROJ = 3 * MIX_HALF + DECAY_RANK + ICLR_RANK + GATE_RANK
LANES = 128
DECAY_PAD = 128
ICLR_PAD = 128
GATE_PAD = 256
HR_WIDTH = 3 * MIX_HALF + DECAY_PAD + ICLR_PAD + GATE_PAD
RMS_EPS = 1e-6
GN_EPS = 64e-5
NEG_INF = -1e30
LOG2E = 1.4426950408889634
VMEM_LIMIT = 56 * 1024 * 1024

REC_CHUNK = 64
REC_TILE = 256
STATE_TILE = 128
STATE_BATCH = 8
ATT_BLOCK = 256
FFN_TILE = 2048
DEC_PAGES_PER_STEP = 32


def _params(*sem):
    return pltpu.CompilerParams(dimension_semantics=sem, vmem_limit_bytes=VMEM_LIMIT)


def _dot(a, b):
    return jnp.dot(a, b, preferred_element_type=F32)


def _dot_nt(a, b):
    return lax.dot_general(a, b, (((1,), (1,)), ((), ())), preferred_element_type=F32)


def _split3(x):
    x1 = x.astype(BF16)
    r1 = x - x1.astype(F32)
    x2 = r1.astype(BF16)
    r2 = r1 - x2.astype(F32)
    return x1, x2, r2.astype(BF16)


def _dot_x01(x, m01):
    x1, x2, x3 = _split3(x)
    return _dot(x1, m01) + _dot(x2, m01) + _dot(x3, m01)


def _group_sum(x, m01):
    x1 = x.astype(BF16)
    return _dot(x1, m01) + _dot((x - x1.astype(F32)).astype(BF16), m01)


def _dot_01x(m01, x):
    x1, x2, x3 = _split3(x)
    return _dot(m01, x1) + _dot(m01, x2) + _dot(m01, x3)


def _rms(x, g):
    ms = jnp.mean(x * x, axis=-1, keepdims=True)
    return x * lax.rsqrt(ms + RMS_EPS) * g


def _sigmoid(x):
    return 1.0 / (1.0 + jnp.exp(-x))


def _softplus(z):
    return jnp.maximum(z, 0.0) + jnp.log(1.0 + jnp.exp(-jnp.abs(z)))


def _ffn_kernel(x_ref, g_ref, wg_ref, wu_ref, wd_ref, o_ref, xn_ref, *, nf):
    j = pl.program_id(1)

    @pl.when(j == 0)
    def _():
        xn_ref[...] = _rms(x_ref[...], g_ref[...]).astype(BF16)
        o_ref[...] = jnp.zeros_like(o_ref)

    xn = xn_ref[...]
    gate = _dot(xn, wg_ref[0].astype(BF16))
    up = _dot(xn, wu_ref[0].astype(BF16))
    h = (gate * _sigmoid(gate) * up).astype(BF16)
    o_ref[...] += _dot(h, wd_ref[0].astype(BF16))

    @pl.when(j == nf - 1)
    def _():
        o_ref[...] = x_ref[...] + 0.5 * o_ref[...]


def _ffn(x, g, wg, wu, wd, *, layer, tm, tf):
    n, d = x.shape
    f = wg.shape[2]
    nf = f // tf
    return pl.pallas_call(
        functools.partial(_ffn_kernel, nf=nf),
        grid=(n // tm, nf),
        in_specs=[
            pl.BlockSpec((tm, d), lambda i, j: (i, 0)),
            pl.BlockSpec((1, d), lambda i, j: (0, 0)),
            pl.BlockSpec((1, d, tf), lambda i, j: (layer, 0, j)),
            pl.BlockSpec((1, d, tf), lambda i, j: (layer, 0, j)),
            pl.BlockSpec((1, tf, d), lambda i, j: (layer, j, 0)),
        ],
        out_specs=pl.BlockSpec((tm, d), lambda i, j: (i, 0)),
        out_shape=jax.ShapeDtypeStruct((n, d), F32),
        scratch_shapes=[pltpu.VMEM((tm, d), BF16)],
        compiler_params=_params("parallel", "arbitrary"),
        name="ffn",
    )(x, g, wg, wu, wd)


def _proj_kernel(x_ref, g_ref, w_ref, bf_ref, *refs, prompt, n_alias):
    out_refs = refs[n_alias:]
    xn = _rms(x_ref[...], g_ref[...]).astype(BF16)
    h = MIX_HALF
    if prompt:
        q_ref, k_ref, kt_ref, vt_ref, hr_ref, lf_ref, lft_ref, v_ref = out_refs
    else:
        q_ref, k_ref, v_ref, hr_ref, lf_ref = out_refs
    q_ref[...] = _dot(xn, w_ref[0, :, 0:h])
    k_ref[...] = _dot(xn, w_ref[0, :, h:2 * h])
    v_ref[...] = _dot(xn, w_ref[0, :, 2 * h:3 * h])
    hr_ref[...] = _dot(xn, w_ref[0, :, 3 * h:3 * h + HR_WIDTH])
    lf_ref[...] = -_softplus(-(_dot(xn, w_ref[0, :, 3 * h + HR_WIDTH:]) + bf_ref[...]))
    if prompt:
        kt_ref[0, 0] = k_ref[...].T
        vt_ref[0, 0] = v_ref[...].T
        lft_ref[0, 0] = lf_ref[...].T[0:N_HEADS, :]


def _proj(x, g, w, bf, *, tm, seq, prompt, layer=0, depth=1, stacked=None):
    n, d = x.shape
    h = MIX_HALF
    nj = seq // tm
    row = lambda i: (i, 0)
    fixed = lambda i: (0, 0)
    trans = lambda i: (layer, i // nj, 0, i % nj)
    rows = lambda width: (pl.BlockSpec((tm, width), row), jax.ShapeDtypeStruct((n, width), F32))
    cols = lambda height: (pl.BlockSpec((1, 1, height, tm), trans),
                           jax.ShapeDtypeStruct((depth, n // seq, height, seq), F32))
    if prompt:
        outs = [rows(h), rows(h), cols(h), cols(h), rows(HR_WIDTH), rows(LANES), cols(N_HEADS)]
    else:
        outs = [rows(h), rows(h), rows(h), rows(HR_WIDTH), rows(LANES)]
    stacked = tuple(stacked or ())
    n_in = 4
    aliases = {n_in + i: o for i, o in enumerate((2, 3, 6)[:len(stacked)])}
    return pl.pallas_call(
        functools.partial(_proj_kernel, prompt=prompt, n_alias=len(stacked)),
        grid=(n // tm,),
        in_specs=[
            pl.BlockSpec((tm, d), row),
            pl.BlockSpec((1, d), fixed),
            pl.BlockSpec((1,) + w.shape[1:], lambda i: (layer, 0, 0)),
            pl.BlockSpec((1, LANES), fixed),
        ] + [pl.BlockSpec(memory_space=pl.ANY)] * len(stacked),
        out_specs=[o[0] for o in outs],
        out_shape=[o[1] for o in outs],
        scratch_shapes=[pltpu.VMEM((tm, h), F32)] if prompt else [],
        input_output_aliases=aliases,
        compiler_params=_params("parallel"),
        name="proj",
    )(x, g, w, bf, *stacked)


def _rwkv_chunk_kernel(h_ref, hp_ref, p0_ref, mu_ref, w0_ref, a0_ref, kk_ref, ka_ref, rk_ref,
                       wd_ref, wi_ref, wg_ref, bd_ref,
                       qc_o, y0_o, g_o, bo_o, m_o, hh_o, *, chunk, n_valid):
    j = pl.program_id(1)
    C = chunk
    C2 = 2 * C
    nb, t_in, _ = h_ref.shape
    tt = g_o.shape[1]
    m = MIX_HALF
    bd = bd_ref[...]

    def mix_inputs(b):
        h = h_ref[b]
        if t_in < tt:
            h = jnp.concatenate([h, jnp.zeros((tt - t_in, HR_WIDTH), F32)], axis=0)
        prev = jnp.where(j == 0, p0_ref[b], hp_ref[b, 7:8, :])
        trow = lax.broadcasted_iota(jnp.int32, h.shape, 0)
        shifted = jnp.where(trow == 0, prev, pltpu.roll(h, 1, axis=0))
        hs = h + mu_ref[...] * (shifted - h)
        r = hs[:, 0:m]
        k = hs[:, m:2 * m]
        v = hs[:, 2 * m:3 * m]
        o = 3 * m
        d_decay = hs[:, o:o + DECAY_PAD]
        d_iclr = hs[:, o + DECAY_PAD:o + DECAY_PAD + ICLR_PAD]
        d_gate = hs[:, o + DECAY_PAD + ICLR_PAD:]
        w_log = -_softplus(-(w0_ref[...] + _dot(jnp.tanh(d_decay).astype(BF16), wd_ref[...]))) - 0.5
        lw = -jnp.exp(w_log)
        a = _sigmoid(a0_ref[...] + _dot(d_iclr.astype(BF16), wi_ref[...]))
        g_o[b] = _dot(_sigmoid(d_gate).astype(BF16), wg_ref[...])
        kk = k * kk_ref[...]
        kk = kk / jnp.maximum(jnp.sqrt(_group_sum(kk * kk, bd)), 1e-12)
        k = k * (1.0 + (a - 1.0) * ka_ref[...])
        bo_o[b] = _group_sum(r * k * rk_ref[...], bd) * v
        na = -kk
        bb = kk * a
        if n_valid < tt:
            ok = lax.broadcasted_iota(jnp.int32, (tt, m), 0) < n_valid
            zero = lambda x: jnp.where(ok, x, 0.0)
            r, k, v, lw, na, bb = zero(r), zero(k), zero(v), zero(lw), zero(na), zero(bb)
        return r, k, v, lw, na, bb

    row = lax.broadcasted_iota(jnp.int32, (C2, C2), 0)
    col = lax.broadcasted_iota(jnp.int32, (C2, C2), 1)
    rt = row & (C - 1)
    ct = col & (C - 1)
    strict = rt > ct
    incl = rt >= ct
    eye = (row == col).astype(F32)
    ti = lax.broadcasted_iota(jnp.int32, (C, C), 0)
    tj = lax.broadcasted_iota(jnp.int32, (C, C), 1)
    tril01 = (ti >= tj).astype(BF16)
    lane_lo = lax.broadcasted_iota(jnp.int32, (1, PAIR), 1) < HEAD_DIM

    def stack(x):
        return jnp.concatenate([jnp.where(lane_lo, x, 0.0), jnp.where(lane_lo, 0.0, x)], axis=0)

    prob = []
    for b in range(nb):
        r, k, v, lw, na, bb = mix_inputs(b)
        for ci in range(tt // C):
            rows = slice(ci * C, (ci + 1) * C)
            lwc = lw[rows]
            cum = _dot_01x(tril01, lwc)
            cum_end = cum[C - 1:C, :]
            e_neg = jnp.exp(-cum)
            e_end = jnp.exp(cum_end - cum)
            at = na[rows] * jnp.exp(cum - lwc)
            rt_ = r[rows] * jnp.exp(cum)
            bt = bb[rows] * e_neg
            kt = k[rows] * e_neg
            bh = bb[rows] * e_end
            kh = k[rows] * e_end
            wc = jnp.exp(cum_end)
            for hp in range(N_PAIRS):
                sl = slice(hp * PAIR, (hp + 1) * PAIR)
                v2 = stack(v[rows, sl])
                prob.append(dict(
                    b=b, ci=ci, sl=sl, hp=hp, rq=rt_[:, sl], wc=wc[:, sl],
                    at=stack(at[:, sl]).astype(BF16), rt=stack(rt_[:, sl]).astype(BF16),
                    bt=stack(bt[:, sl]).astype(BF16), kt=stack(kt[:, sl]).astype(BF16),
                    bh=stack(bh[:, sl]).astype(BF16), kh=stack(kh[:, sl]).astype(BF16),
                    v=v2.astype(BF16), vt=v2.T.astype(BF16)))

    mm = [_dot_nt(jnp.concatenate([p["at"], p["rt"]], axis=0), jnp.concatenate([p["bt"], p["kt"]], axis=0))
          for p in prob]
    n_ab = [jnp.where(strict, x[0:C2, 0:C2], 0.0) for x in mm]
    a_kr = [jnp.concatenate([jnp.where(strict, x[0:C2, C2:], 0.0),
                             jnp.where(incl, x[C2:, C2:], 0.0)], axis=0).astype(BF16) for x in mm]
    a_rb = [jnp.where(incl, x[C2:, 0:C2], 0.0).astype(BF16) for x in mm]
    kv = [_dot(x, p["v"]) for x, p in zip(a_kr, prob)]
    vtk = [_dot(p["vt"], p["kh"]) for p in prob]

    t_inv = [eye + jnp.where((rt >> 1) == (ct >> 1), x, 0.0) for x in n_ab]
    lvl = 1
    while (1 << lvl) < C:
        sel = ((rt >> (lvl + 1)) == (ct >> (lvl + 1))) & ((rt >> lvl) != (ct >> lvl))
        off = [jnp.where(sel, x, 0.0).astype(BF16) for x in n_ab]
        tb = [x.astype(BF16) for x in t_inv]
        xx = [_dot(x, y).astype(BF16) for x, y in zip(off, tb)]
        t_inv = [x + _dot(y, z) for x, y, z in zip(t_inv, tb, xx)]
        lvl += 1

    pu = [_dot(t.astype(BF16), jnp.concatenate([p["at"], x[0:C2].astype(BF16)], axis=1))
          for t, p, x in zip(t_inv, prob, kv)]
    ab = [_dot(x, y.astype(BF16)) for x, y in zip(a_rb, pu)]
    gb = [_dot(x.T.astype(BF16), p["bh"]) for x, p in zip(pu, prob)]
    for p, x_ab, x_kv, x_gb, x_vtk in zip(prob, ab, kv, gb, vtk):
        b = p["b"]
        rows = slice(p["ci"] * C, (p["ci"] + 1) * C)
        qc_o[b, rows, p["sl"]] = (p["rq"] + x_ab[0:C, 0:PAIR] + x_ab[C:, 0:PAIR]).astype(BF16)
        y0 = x_ab[:, PAIR:] + x_kv[C2:]
        y0_o[b, rows, p["sl"]] = y0[0:C] + y0[C:]
        m_o[b, p["ci"], p["hp"]] = (x_gb[0:PAIR] + eye * p["wc"]).astype(BF16)
        hh_o[b, p["ci"], p["hp"]] = x_gb[PAIR:] + x_vtk


def _rwkv_chunk(hr, prev0, lp, *, nb, tt, chunk, n_valid):
    b, t_in, _ = hr.shape
    t = max(t_in, tt)
    m = MIX_HALF
    g = tt // chunk
    tile = lambda i, j: (i, j, 0)
    fixed = lambda i, j: (0, 0)
    vec = pl.BlockSpec((1, m), fixed)
    tok = (pl.BlockSpec((nb, tt, m), tile), jax.ShapeDtypeStruct((b, t, m), F32))
    tok16 = (tok[0], jax.ShapeDtypeStruct((b, t, m), BF16))
    mat = lambda dt: (pl.BlockSpec((nb, g, N_PAIRS, PAIR, PAIR), lambda i, j: (i, j, 0, 0, 0)),
                      jax.ShapeDtypeStruct((b, t // chunk, N_PAIRS, PAIR, PAIR), dt))
    outs = [tok16, tok, tok, tok, mat(BF16), mat(F32)]
    return pl.pallas_call(
        functools.partial(_rwkv_chunk_kernel, chunk=chunk, n_valid=n_valid),
        grid=(b // nb, t // tt),
        in_specs=[
            pl.BlockSpec((nb, min(tt, t_in), HR_WIDTH), tile),
            pl.BlockSpec((nb, 8, HR_WIDTH), lambda i, j: (i, jnp.maximum(j * (tt // 8) - 1, 0), 0)),
            pl.BlockSpec((nb, 1, HR_WIDTH), lambda i, j: (i, 0, 0)),
            pl.BlockSpec((1, HR_WIDTH), fixed),
            vec, vec, vec, vec, vec,
            pl.BlockSpec((DECAY_PAD, m), fixed),
            pl.BlockSpec((ICLR_PAD, m), fixed),
            pl.BlockSpec((GATE_PAD, m), fixed),
            pl.BlockSpec((m, m), fixed),
        ],
        out_specs=[o[0] for o in outs],
        out_shape=[o[1] for o in outs],
        compiler_params=_params("parallel", "arbitrary"),
        name="rwkv_chunk",
    )(hr, hr, prev0, lp["mu"], lp["w0"], lp["a0"], lp["k_k"], lp["k_a"], lp["r_k"],
      lp["w_decay"], lp["w_iclr"], lp["w_gate"], lp["bd512"])


def _rwkv_state_kernel(qc_ref, y0_ref, g_ref, bo_ref, m_ref, hh_ref, s0_ref, lng_ref, lnb_ref, bd_ref,
                       y_ref, sf_ref, s_ref, *, chunk, nsteps):
    j = pl.program_id(1)
    C = chunk
    nb, tt, _ = qc_ref.shape
    bd = bd_ref[...]

    @pl.when(j == 0)
    def _():
        s_ref[...] = s0_ref[...]

    cells = [(b, hp) for b in range(nb) for hp in range(N_PAIRS)]
    state = [s_ref[b, hp] for b, hp in cells]
    for ci in range(tt // C):
        rows = slice(ci * C, (ci + 1) * C)
        sb = [s.astype(BF16) for s in state]
        ys = [_dot_nt(qc_ref[b, rows, hp * PAIR:(hp + 1) * PAIR].astype(BF16), s)
              + y0_ref[b, rows, hp * PAIR:(hp + 1) * PAIR] for (b, hp), s in zip(cells, sb)]
        state = [_dot(s, m_ref[b, ci, hp]) + hh_ref[b, ci, hp] for (b, hp), s in zip(cells, sb)]
        means = [_group_sum(y, bd) * (1.0 / HEAD_DIM) for y in ys]
        yc = [y - mu for y, mu in zip(ys, means)]
        var = [_group_sum(y * y, bd) * (1.0 / HEAD_DIM) for y in yc]
        for (b, hp), y, vr in zip(cells, yc, var):
            sl = slice(hp * PAIR, (hp + 1) * PAIR)
            yn = y * lax.rsqrt(vr + GN_EPS) * lng_ref[:, sl] + lnb_ref[:, sl]
            y_ref[b, rows, sl] = ((yn + bo_ref[b, rows, sl]) * g_ref[b, rows, sl]).astype(BF16)
    for (b, hp), s in zip(cells, state):
        s_ref[b, hp] = s

    @pl.when(j == nsteps - 1)
    def _():
        sf_ref[...] = s_ref[...]


def _rwkv_state(qc, y0, g, bonus, mm, hh, s0, lp, *, nb, tt, chunk):
    bsz, t, m = qc.shape
    gch = tt // chunk
    nsteps = t // tt
    tile = pl.BlockSpec((nb, tt, m), lambda i, j: (i, j, 0))
    mat = pl.BlockSpec((nb, gch, N_PAIRS, PAIR, PAIR), lambda i, j: (i, j, 0, 0, 0))
    st = pl.BlockSpec((nb, N_PAIRS, PAIR, PAIR), lambda i, j: (i, 0, 0, 0))
    vec = pl.BlockSpec((1, m), lambda i, j: (0, 0))
    return pl.pallas_call(
        functools.partial(_rwkv_state_kernel, chunk=chunk, nsteps=nsteps),
        grid=(bsz // nb, nsteps),
        in_specs=[tile, tile, tile, tile, mat, mat, st, vec, vec,
                  pl.BlockSpec((PAIR, PAIR), lambda i, j: (0, 0))],
        out_specs=[tile, st],
        out_shape=[jax.ShapeDtypeStruct((bsz, t, m), BF16),
                   jax.ShapeDtypeStruct((bsz, N_PAIRS, PAIR, PAIR), F32)],
        scratch_shapes=[pltpu.VMEM((nb, N_PAIRS, PAIR, PAIR), F32)],
        compiler_params=_params("parallel", "arbitrary"),
        name="rwkv_state",
    )(qc, y0, g, bonus, mm, hh, s0, lp["lnx_g"], lp["lnx_b"], lp["bd128"])


def _bias_lane(head):
    return HEAD_DIM if head % 2 == 0 else 0


def _fox_bias_kernel(q_ref, k_ref, lf_ref, pk_ref, pq_ref, qa_ref, ka_ref, *, blk, nblk):
    ti = lax.broadcasted_iota(jnp.int32, (blk, blk), 0)
    tj = lax.broadcasted_iota(jnp.int32, (blk, blk), 1)
    tril01 = (ti >= tj).astype(BF16)
    lane = lax.broadcasted_iota(jnp.int32, (1, PAIR), 1)
    carry = jnp.zeros((1, LANES), F32)
    for i in range(nblk):
        sl = slice(i * blk, (i + 1) * blk)
        c = _dot_01x(tril01, lf_ref[0, sl, :]) + carry
        carry = c[blk - 1:blk, :]
        p1, p2, p3 = (jnp.where(lane < N_HEADS, x.astype(F32), 0.0) for x in _split3(c * LOG2E))
        pieces = (p1 + pltpu.roll(p2, N_HEADS, axis=1) + pltpu.roll(p3, 2 * N_HEADS, axis=1)).astype(BF16)
        for hp in range(N_PAIRS):
            ps = slice(hp * PAIR, (hp + 1) * PAIR)
            kp = k_ref[0, sl, ps]
            qp = q_ref[0, sl, ps] * (HEAD_DIM ** -0.5 * LOG2E)
            k_aug = _dot(pieces, pk_ref[hp])
            q_aug = _dot(pieces, pq_ref[hp])
            for half in range(2):
                h = 2 * hp + half
                own = (lane < HEAD_DIM) if half == 0 else (lane >= HEAD_DIM)
                off = lane - _bias_lane(h)
                hs = slice(h * PAIR, (h + 1) * PAIR)
                aug = slice(half * PAIR, (half + 1) * PAIR)
                ka_ref[0, sl, hs] = jnp.where(
                    own, kp, jnp.where((off >= 3) & (off < 6), 1.0, k_aug[:, aug])).astype(BF16)
                qa_ref[0, sl, hs] = jnp.where(
                    own, qp, jnp.where((off >= 0) & (off < 3), 1.0, q_aug[:, aug])).astype(BF16)


def _bias_placement():
    src = jnp.arange(LANES)
    piece, head = src // N_HEADS, src % N_HEADS
    dst = jnp.arange(2 * PAIR)
    pk, pq = [], []
    for hp in range(N_PAIRS):
        k_hit = jnp.zeros((LANES, 2 * PAIR), bool)
        q_hit = jnp.zeros((LANES, 2 * PAIR), bool)
        for half in range(2):
            h = 2 * hp + half
            mine = ((head == h) & (piece < 3))[:, None]
            lane0 = half * PAIR + _bias_lane(h)
            k_hit |= mine & (dst[None, :] == lane0 + piece[:, None])
            q_hit |= mine & (dst[None, :] == lane0 + 3 + piece[:, None])
        pk.append(jnp.where(k_hit, -1.0, 0.0))
        pq.append(jnp.where(q_hit, 1.0, 0.0))
    return jnp.stack(pk).astype(BF16), jnp.stack(pq).astype(BF16)


def _fox_bias(q, k, lf, place_k, place_q, *, blk):
    b, s, m = q.shape
    wide = N_HEADS * PAIR
    row = lambda width: pl.BlockSpec((1, s, width), lambda i: (i, 0, 0))
    place = pl.BlockSpec(place_k.shape, lambda i: (0, 0, 0))
    out = jax.ShapeDtypeStruct((b, s, wide), BF16)
    return pl.pallas_call(
        functools.partial(_fox_bias_kernel, blk=blk, nblk=s // blk),
        grid=(b,),
        in_specs=[row(m), row(m), row(LANES), place, place],
        out_specs=[row(wide), row(wide)],
        out_shape=[out, out],
        compiler_params=_params("parallel"),
        name="fox_bias",
    )(q, k, lf, place_k, place_q)


def _fox_prompt_kernel(qa_ref, ka_ref, vt_ref, gain_ref, o_ref, m_s, l_s, acc_s, st_s, *, blk):
    qi = pl.program_id(1)
    heads = range(N_HEADS)
    m_s[...] = jnp.full(m_s.shape, NEG_INF, F32)
    l_s[...] = jnp.zeros_like(l_s)
    acc_s[...] = jnp.zeros_like(acc_s)

    def keys(j):
        return pl.ds(pl.multiple_of(j * blk, blk), blk)

    def scores(j, slot, diagonal):
        ks = keys(j)
        st = [_dot_nt(ka_ref[0, ks, h * PAIR:(h + 1) * PAIR], qa_ref[0, :, h * PAIR:(h + 1) * PAIR])
              for h in heads]
        if diagonal:
            ki = lax.broadcasted_iota(jnp.int32, (blk, blk), 0)
            qj = lax.broadcasted_iota(jnp.int32, (blk, blk), 1)
            st = [jnp.where(ki <= qj, x, NEG_INF) for x in st]
        for h in heads:
            st_s[slot, h] = st[h]

    def consume(j, slot):
        ks = keys(j)
        st = [st_s[slot, h] for h in heads]
        m_old = [m_s[h:h + 1, :] for h in heads]
        m_new = [jnp.maximum(mo, jnp.max(x, axis=0, keepdims=True)) for mo, x in zip(m_old, st)]
        p = [jnp.exp2(x - mn) for x, mn in zip(st, m_new)]
        pv = [_dot(vt_ref[0, 0, h * HEAD_DIM:(h + 1) * HEAD_DIM, ks].astype(BF16), x.astype(BF16))
              for h, x in zip(heads, p)]
        for h in heads:
            alpha = jnp.exp2(m_old[h] - m_new[h])
            m_s[h:h + 1, :] = m_new[h]
            l_s[h:h + 1, :] = alpha * l_s[h:h + 1, :] + jnp.sum(p[h], axis=0, keepdims=True)
            rows = slice(h * HEAD_DIM, (h + 1) * HEAD_DIM)
            acc_s[rows, :] = alpha * acc_s[rows, :] + pv[h]

    scores(qi, 0, True)

    def body(u, carry):
        j = 2 * u
        scores(j, 1, False)
        consume(jnp.where(u == 0, qi, j - 1), 0)
        scores(j + 1, 0, False)
        consume(j, 1)
        return carry

    lax.fori_loop(0, qi // 2, body, 0)
    odd = (qi & 1) == 1

    @pl.when(odd)
    def _():
        scores(qi - 1, 1, False)
        consume(jnp.where(qi == 1, qi, qi - 2), 0)
        consume(qi - 1, 1)

    @pl.when(jnp.logical_not(odd))
    def _():
        consume(jnp.where(qi == 0, qi, qi - 1), 0)

    for hp in range(N_PAIRS):
        halves = []
        for h in (2 * hp, 2 * hp + 1):
            o = acc_s[h * HEAD_DIM:(h + 1) * HEAD_DIM, :] / l_s[h:h + 1, :]
            ms = jnp.mean(o * o, axis=0, keepdims=True)
            halves.append(o * lax.rsqrt(ms + RMS_EPS))
        sl = slice(hp * PAIR, (hp + 1) * PAIR)
        o_ref[0, :, sl] = (jnp.concatenate(halves, axis=0).T * gain_ref[:, sl]).astype(BF16)


def _fox_prompt(qa, ka, vt_all, gain, *, layer, blk):
    b, s, wide = qa.shape
    m = MIX_HALF
    nb = s // blk
    return pl.pallas_call(
        functools.partial(_fox_prompt_kernel, blk=blk),
        grid=(b, nb),
        in_specs=[
            pl.BlockSpec((1, blk, wide), lambda i, j: (i, j, 0)),
            pl.BlockSpec((1, s, wide), lambda i, j: (i, 0, 0)),
            pl.BlockSpec((1, 1, m, s), lambda i, j: (layer, i, 0, 0)),
            pl.BlockSpec((1, m), lambda i, j: (0, 0)),
        ],
        out_specs=pl.BlockSpec((1, blk, m), lambda i, j: (i, j, 0)),
        out_shape=jax.ShapeDtypeStruct((b, s, m), BF16),
        scratch_shapes=[
            pltpu.VMEM((N_HEADS, blk), F32),
            pltpu.VMEM((N_HEADS, blk), F32),
            pltpu.VMEM((m, blk), F32),
            pltpu.VMEM((2, N_HEADS, blk, blk), F32),
        ],
        compiler_params=_params("parallel", "arbitrary"),
        name="fox_prompt",
    )(qa, ka, vt_all, gain)


def _fox_decode_kernel(pt_ref, q_ref, kn_ref, vn_ref, lfn_ref, lfnt_ref, gain_ref, hm_ref, bd_ref, *rest,
                       npg, nsteps, tnew):
    k_pages = rest[0:npg]
    v_pages = rest[npg:2 * npg]
    lft_ref = rest[2 * npg]
    o_ref, m_s, l_s, acc_s, carry_s, qbd_s, cn_s = rest[2 * npg + 1:]
    seq = pl.program_id(0)
    g = pl.program_id(1)
    nrow = tnew * N_HEADS
    page = k_pages[0].shape[3]
    li = lax.broadcasted_iota(jnp.int32, (LANES, LANES), 0)
    lj = lax.broadcasted_iota(jnp.int32, (LANES, LANES), 1)
    hm = hm_ref[...]

    def tile_rows(x):
        return jnp.concatenate([x] * tnew, axis=0)

    @pl.when(g == 0)
    def _():
        q = q_ref[0] * (HEAD_DIM ** -0.5)
        qbd = jnp.concatenate([jnp.broadcast_to(q[t:t + 1, :], hm.shape) * hm for t in range(tnew)], axis=0)
        qbd_s[...] = qbd.astype(BF16)
        cn = lfn_ref[0]
        trow = lax.broadcasted_iota(jnp.int32, cn.shape, 0)
        sh = 1
        while sh < tnew:
            cn = cn + jnp.where(trow >= sh, pltpu.roll(cn, sh, axis=0), 0.0)
            sh *= 2
        hsel = (lax.broadcasted_iota(jnp.int32, (N_HEADS, LANES), 0)
                == lax.broadcasted_iota(jnp.int32, (N_HEADS, LANES), 1)).astype(F32)
        cn_rows = jnp.concatenate(
            [jnp.sum(jnp.broadcast_to(cn[t:t + 1, :], hsel.shape) * hsel, axis=1, keepdims=True)
             for t in range(tnew)], axis=0)
        cn_s[...] = cn_rows
        cnt = _dot_x01(lfnt_ref[0], (li <= lj).astype(BF16))
        pad = jnp.zeros((page - tnew, MIX_HALF), F32)
        kn = jnp.concatenate([kn_ref[0], pad], axis=0).astype(BF16)
        vn = jnp.concatenate([vn_ref[0], pad], axis=0).astype(BF16)
        s = _dot_nt(qbd_s[...], kn) + (cn_rows - tile_rows(cnt))
        rtok = lax.broadcasted_iota(jnp.int32, (nrow, LANES), 0) >> 3
        ktok = lax.broadcasted_iota(jnp.int32, (nrow, LANES), 1)
        s = jnp.where(ktok <= rtok, s, NEG_INF)
        m = jnp.max(s, axis=1, keepdims=True)
        p = jnp.exp(s - m)
        m_s[...] = m
        l_s[...] = jnp.sum(p, axis=1, keepdims=True)
        acc_s[...] = _dot(p.astype(BF16), vn)
        carry_s[...] = jnp.zeros_like(carry_s)

    slots = range(npg)
    nr = npg * N_HEADS
    lf_all = jnp.concatenate([lft_ref[0, pt_ref[seq, (nsteps - 1 - g) * npg + i]] for i in slots], axis=0)
    ri = lax.broadcasted_iota(jnp.int32, (nr, nr), 0)
    rj = lax.broadcasted_iota(jnp.int32, (nr, nr), 1)
    later_page = (((ri & (N_HEADS - 1)) == (rj & (N_HEADS - 1))) & (rj > ri)).astype(BF16)
    total = jnp.sum(lf_all, axis=1, keepdims=True)
    after = jnp.sum(_dot_01x(later_page, lf_all), axis=1, keepdims=True)
    carry = carry_s[...]
    bias_all = _dot_x01(lf_all, (li > lj).astype(BF16)) + (after + jnp.concatenate([carry] * npg, axis=0))
    carry_s[...] = carry + (after + total)[0:N_HEADS]
    qbd = qbd_s[...]
    cn_rows = cn_s[...]
    kb = [k_pages[i][0, 0].astype(BF16) for i in slots]
    sc = [_dot(qbd, x) for x in kb]
    s = jnp.concatenate([x + (tile_rows(bias_all[i * N_HEADS:(i + 1) * N_HEADS, :]) + cn_rows)
                         for i, x in zip(slots, sc)], axis=1)
    m_old = m_s[...]
    m_new = jnp.maximum(m_old, jnp.max(s, axis=1, keepdims=True))
    alpha = jnp.exp(m_old - m_new)
    p = jnp.exp(s - m_new)
    m_s[...] = m_new
    l_s[...] = alpha * l_s[...] + jnp.sum(p, axis=1, keepdims=True)
    vb = [v_pages[i][0, 0].astype(BF16) for i in slots]
    pv = [_dot_nt(p[:, i * page:(i + 1) * page].astype(BF16), x) for i, x in zip(slots, vb)]
    while len(pv) > 1:
        pv = [a + b for a, b in zip(pv[0::2], pv[1::2])] + ([pv[-1]] if len(pv) % 2 else [])
    acc_s[...] = alpha * acc_s[...] + pv[0]

    @pl.when(g == nsteps - 1)
    def _():
        o_rows = acc_s[...] / l_s[...]
        o = jnp.concatenate(
            [jnp.sum(o_rows[t * N_HEADS:(t + 1) * N_HEADS, :] * hm, axis=0, keepdims=True) for t in range(tnew)],
            axis=0)
        ms = _dot_x01(o * o, bd_ref[...]) * (1.0 / HEAD_DIM)
        o_ref[0] = o * lax.rsqrt(ms + RMS_EPS) * gain_ref[...]


def _fox_decode(layer, page_table, q, kn, vn, lfn, lfn_t, cache_k, cache_v, cache_lft, gain, hm, bd, *, npg):
    b, tnew, m = q.shape
    n_pages = page_table.shape[1]
    page = cache_k.shape[3]
    nsteps = n_pages // npg
    nrow = tnew * N_HEADS

    def tok(i, g, pt):
        return (i, 0, 0)

    def fixed(i, g, pt):
        return (0, 0)

    def page_map(slot):
        return lambda i, g, pt: (layer, pt[i, (nsteps - 1 - g) * npg + slot], 0, 0)

    kv_specs = [pl.BlockSpec((1, 1, m, page), page_map(s)) for s in range(npg)]
    n_pool = cache_lft.shape[1]
    lf_spec = pl.BlockSpec((1, n_pool, N_HEADS, page), lambda i, g, pt: (layer, 0, 0, 0),
                           pipeline_mode=pl.Buffered(1))
    grid_spec = pltpu.PrefetchScalarGridSpec(
        num_scalar_prefetch=1,
        grid=(b, nsteps),
        in_specs=[
            pl.BlockSpec((1, tnew, m), tok),
            pl.BlockSpec((1, tnew, m), tok),
            pl.BlockSpec((1, tnew, m), tok),
            pl.BlockSpec((1, tnew, LANES), tok),
            pl.BlockSpec((1, N_HEADS, LANES), tok),
            pl.BlockSpec((1, m), fixed),
            pl.BlockSpec((N_HEADS, m), fixed),
            pl.BlockSpec((m, m), fixed),
        ] + kv_specs + kv_specs + [lf_spec],
        out_specs=pl.BlockSpec((1, tnew, m), tok),
        scratch_shapes=[
            pltpu.VMEM((nrow, 1), F32),
            pltpu.VMEM((nrow, 1), F32),
            pltpu.VMEM((nrow, m), F32),
            pltpu.VMEM((N_HEADS, 1), F32),
            pltpu.VMEM((nrow, m), BF16),
            pltpu.VMEM((nrow, 1), F32),
        ],
    )
    return pl.pallas_call(
        functools.partial(_fox_decode_kernel, npg=npg, nsteps=nsteps, tnew=tnew),
        grid_spec=grid_spec,
        out_shape=jax.ShapeDtypeStruct((b, tnew, m), F32),
        compiler_params=_params("parallel", "arbitrary"),
        name="fox_decode",
    )(page_table, q, kn, vn, lfn, lfn_t, gain, hm, bd,
      *([cache_k] * npg), *([cache_v] * npg), cache_lft)


def _out_proj_kernel(x_ref, yr_ref, yf_ref, wr_ref, wf_ref, o_ref):
    o_ref[...] = (x_ref[...] + _dot(yr_ref[...].astype(BF16), wr_ref[...])
                  + _dot(yf_ref[...].astype(BF16), wf_ref[...]))


def _out_proj(x, yr, yf, wr, wf, *, tm):
    n, d = x.shape
    m = yr.shape[1]
    row = lambda i: (i, 0)
    fixed = lambda i: (0, 0)
    return pl.pallas_call(
        _out_proj_kernel,
        grid=(n // tm,),
        in_specs=[pl.BlockSpec((tm, d), row), pl.BlockSpec((tm, m), row), pl.BlockSpec((tm, m), row),
                  pl.BlockSpec((m, d), fixed), pl.BlockSpec((m, d), fixed)],
        out_specs=pl.BlockSpec((tm, d), row),
        out_shape=jax.ShapeDtypeStruct((n, d), F32),
        compiler_params=_params("parallel"),
        name="out_proj",
    )(x, yr, yf, wr, wf)


def _ple_kernel(x_ref, p_ref, g_ref, wg_ref, wu_ref, fg_ref, o_ref, *, final):
    x = x_ref[...]
    gate = _sigmoid(_dot(_rms(x, g_ref[...]).astype(BF16), wg_ref[...]))
    y = x + gate * _dot(p_ref[0].astype(BF16), wu_ref[...])
    o_ref[...] = _rms(y, fg_ref[...]) if final else y


def _ple(x, p, g, wg, wu, fg, *, layer, tm, final):
    n, d = x.shape
    pd = p.shape[2]
    row = lambda i: (i, 0)
    fixed = lambda i: (0, 0)
    return pl.pallas_call(
        functools.partial(_ple_kernel, final=final),
        grid=(n // tm,),
        in_specs=[pl.BlockSpec((tm, d), row), pl.BlockSpec((1, tm, pd), lambda i: (layer, i, 0)),
                  pl.BlockSpec((1, d), fixed),
                  pl.BlockSpec((d, d), fixed), pl.BlockSpec((pd, d), fixed), pl.BlockSpec((1, d), fixed)],
        out_specs=pl.BlockSpec((tm, d), row),
        out_shape=jax.ShapeDtypeStruct((n, d), F32),
        compiler_params=_params("parallel"),
        name="ple",
    )(x, p, g, wg, wu, fg)


def _pad_cols(w, width):
    return jnp.pad(w, [(0, 0)] * (w.ndim - 1) + [(0, width - w.shape[-1])])


def _pad_rows(w, height):
    return jnp.pad(w, [(0, 0)] * (w.ndim - 2) + [(0, height - w.shape[-2]), (0, 0)])


def _pack_hr(h):
    m = MIX_HALF
    o = 3 * m
    return jnp.concatenate([
        h[..., :o],
        _pad_cols(h[..., o:o + DECAY_RANK], DECAY_PAD),
        _pad_cols(h[..., o + DECAY_RANK:o + DECAY_RANK + ICLR_RANK], ICLR_PAD),
        _pad_cols(h[..., o + DECAY_RANK + ICLR_RANK:], GATE_PAD)], axis=-1)


def _unpack_hr(h):
    m = MIX_HALF
    o = 3 * m
    return jnp.concatenate([
        h[..., :o],
        h[..., o:o + DECAY_RANK],
        h[..., o + DECAY_PAD:o + DECAY_PAD + ICLR_RANK],
        h[..., o + DECAY_PAD + ICLR_PAD:o + DECAY_PAD + ICLR_PAD + GATE_RANK]], axis=-1)


def _pair_state(s):
    b = s.shape[0]
    s = s.reshape(b, N_PAIRS, 2, HEAD_DIM, HEAD_DIM)
    z = jnp.zeros_like(s[:, :, 0])
    top = jnp.concatenate([s[:, :, 0], z], axis=-1)
    bot = jnp.concatenate([z, s[:, :, 1]], axis=-1)
    return jnp.concatenate([top, bot], axis=-2)


def _unpair_state(s):
    b = s.shape[0]
    d = HEAD_DIM
    return jnp.stack([s[:, :, :d, :d], s[:, :, d:, d:]], axis=2).reshape(b, N_HEADS, d, d)


def _block_diag01(n):
    i = jnp.arange(n) // HEAD_DIM
    return (i[:, None] == i[None, :]).astype(BF16)


def _rwkv_mix(hr3, prev0, s0, lp, *, n_valid):
    bsz, t, _ = hr3.shape
    chunk = REC_CHUNK
    t = max(t, chunk)
    tt = min(t, REC_TILE)
    nb = max(1, min(bsz, REC_TILE // tt))
    qc, y0, g, bonus, mm, hh = _rwkv_chunk(hr3, prev0, lp, nb=nb, tt=tt, chunk=chunk, n_valid=n_valid)
    return _rwkv_state(qc, y0, g, bonus, mm, hh, s0, lp, nb=min(bsz, STATE_BATCH), tt=min(t, STATE_TILE),
                       chunk=chunk)


def kernel(x_prompt, x_sample, cache_k, cache_v, cache_logf, state_wkv, state_shift, page_table, p_prompt, p_sample, ffn1_norm, ffn1_w_gate, ffn1_w_up, ffn1_w_down, mix_norm, w_in, rwkv_mu, rwkv_w0, rwkv_w_decay, rwkv_a0, rwkv_w_iclr, rwkv_w_gate, rwkv_k_k, rwkv_k_a, rwkv_r_k, rwkv_lnx_g, rwkv_lnx_b, fox_b_f, fox_out_norm, w_out, ffn2_norm, ffn2_w_gate, ffn2_w_up, ffn2_w_down, ple_norm, ple_w_gate, ple_w_up, final_norm):
    depth = w_in.shape[0]
    bp, sp, d = x_prompt.shape
    bs, ts, _ = x_sample.shape
    m = MIX_HALF
    n_pool, page = cache_k.shape[1], cache_k.shape[2]
    npr, nsm = bp * sp, bs * ts
    pd = p_prompt.shape[-1]

    fox_cols = w_in[:, :, RWKV_PROJ:]
    w_pack = jnp.concatenate([
        fox_cols[:, :, :3 * m],
        _pack_hr(w_in[:, :, :RWKV_PROJ]),
        _pad_cols(fox_cols[:, :, 3 * m:], LANES)], axis=-1).astype(BF16)
    b_f = _pad_cols(fox_b_f, LANES)[:, None, :]
    bd512 = _block_diag01(m)
    bd128 = _block_diag01(PAIR)
    hm = (jnp.arange(m)[None, :] // HEAD_DIM == jnp.arange(N_HEADS)[:, None]).astype(F32)
    bf = lambda w: w.astype(BF16)
    f1g, f1u, f1d = ffn1_w_gate, ffn1_w_up, ffn1_w_down
    f2g, f2u, f2d = ffn2_w_gate, ffn2_w_up, ffn2_w_down
    wo = bf(w_out)
    pg, pu = bf(ple_w_gate), bf(ple_w_up)
    wdec = bf(_pad_rows(rwkv_w_decay, DECAY_PAD))
    wicl = bf(_pad_rows(rwkv_w_iclr, ICLR_PAD))
    wgat = bf(_pad_rows(rwkv_w_gate, GATE_PAD))
    mu = _pack_hr(rwkv_mu)
    cache_kt = jnp.transpose(cache_k, (0, 1, 3, 4, 2)).reshape(depth, n_pool, m, page)
    cache_vt = jnp.transpose(cache_v, (0, 1, 3, 4, 2)).reshape(depth, n_pool, m, page)
    cache_lft = jnp.swapaxes(cache_logf, 2, 3)
    fg = final_norm[None, :]
    place_k, place_q = _bias_placement()

    tm_p = 1024 if npr % 1024 == 0 else ATT_BLOCK
    tm_ffn = FFN_TILE if npr % FFN_TILE == 0 else tm_p
    stacked = (jnp.zeros((depth, bp, m, sp), F32), jnp.zeros((depth, bp, m, sp), F32),
               jnp.zeros((depth, bp, N_HEADS, sp), F32))
    tf = 256
    blk = ATT_BLOCK
    npg = min(DEC_PAGES_PER_STEP, page_table.shape[1])

    xp = x_prompt.reshape(npr, d)
    xs = x_sample.reshape(nsm, d)
    outs = [[] for _ in range(10)]
    for l in range(depth):
        lp = dict(mu=mu[l][None], w0=rwkv_w0[l][None], a0=rwkv_a0[l][None], k_k=rwkv_k_k[l][None],
                  k_a=rwkv_k_a[l][None], r_k=rwkv_r_k[l].reshape(1, m), w_decay=wdec[l], w_iclr=wicl[l],
                  w_gate=wgat[l], bd512=bd512, bd128=bd128, lnx_g=rwkv_lnx_g[l][None],
                  lnx_b=rwkv_lnx_b[l][None])
        gain = fox_out_norm[l].reshape(1, m)
        last = l == depth - 1

        xp = _ffn(xp, ffn1_norm[l][None], f1g, f1u, f1d, layer=l, tm=tm_ffn, tf=tf)
        q, k, kt_all, vt_all, hr, lf, lft_all = _proj(
            xp, mix_norm[l][None], w_pack, b_f[l], tm=blk, seq=sp, prompt=True,
            layer=l, depth=depth, stacked=stacked)
        stacked = (kt_all, vt_all, lft_all)
        hr3 = hr.reshape(bp, sp, HR_WIDTH)
        y_r, s_fin = _rwkv_mix(hr3, jnp.zeros((bp, 1, HR_WIDTH), F32),
                               jnp.zeros((bp, N_PAIRS, PAIR, PAIR), F32), lp, n_valid=sp)
        qa, ka = _fox_bias(q.reshape(bp, sp, m), k.reshape(bp, sp, m), lf.reshape(bp, sp, LANES),
                           place_k, place_q, blk=blk)
        y_f = _fox_prompt(qa, ka, vt_all, gain, layer=l, blk=blk)
        xp = _out_proj(xp, y_r.reshape(npr, m), y_f.reshape(npr, m), wo[l, :m], wo[l, m:], tm=tm_p)
        xp = _ffn(xp, ffn2_norm[l][None], f2g, f2u, f2d, layer=l, tm=tm_ffn, tf=tf)
        xp = _ple(xp, p_prompt.reshape(depth, npr, pd), ple_norm[l][None], pg[l], pu[l], fg,
                  layer=l, tm=tm_p, final=last)
        outs[3].append(_unpair_state(s_fin))
        outs[4].append(_unpack_hr(hr3[:, -1, :]))

        xs = _ffn(xs, ffn1_norm[l][None], f1g, f1u, f1d, layer=l, tm=nsm, tf=tf)
        q, k, v, hr, lf = _proj(xs, mix_norm[l][None], w_pack, b_f[l],
                                tm=nsm, seq=nsm, prompt=False, layer=l)
        hr3 = hr.reshape(bs, ts, HR_WIDTH)
        y_r, s_fin = _rwkv_mix(hr3, _pack_hr(state_shift[l])[:, None, :], _pair_state(state_wkv[l]), lp,
                               n_valid=ts)
        lf3 = lf.reshape(bs, ts, LANES)
        lf_t = _pad_cols(jnp.swapaxes(lf3[:, :, :N_HEADS], 1, 2), LANES)
        y_f = _fox_decode(l, page_table, q.reshape(bs, ts, m), k.reshape(bs, ts, m), v.reshape(bs, ts, m),
                          lf3, lf_t, cache_kt, cache_vt, cache_lft, gain, hm, bd512, npg=npg)
        xs = _out_proj(xs, y_r[:, :ts].reshape(nsm, m), y_f.reshape(nsm, m), wo[l, :m], wo[l, m:], tm=nsm)
        xs = _ffn(xs, ffn2_norm[l][None], f2g, f2u, f2d, layer=l, tm=nsm, tf=tf)
        xs = _ple(xs, p_sample.reshape(depth, nsm, pd), ple_norm[l][None], pg[l], pu[l], fg,
                  layer=l, tm=nsm, final=last)
        outs[5].append(k.reshape(bs, ts, N_HEADS, HEAD_DIM))
        outs[6].append(v.reshape(bs, ts, N_HEADS, HEAD_DIM))
        outs[7].append(lf3[:, :, :N_HEADS])
        outs[8].append(_unpair_state(s_fin))
        outs[9].append(_unpack_hr(hr3[:, -1, :]))

    st = [jnp.stack(o) for o in outs[3:]]
    k_p, v_p = (jnp.transpose(t.reshape(depth, bp, N_HEADS, HEAD_DIM, sp), (0, 1, 4, 2, 3)) for t in stacked[:2])
    lf_p = jnp.swapaxes(stacked[2], 2, 3)
    return (xp.reshape(bp, sp, d), xs.reshape(bs, ts, d), k_p, v_p, lf_p, *st)
```
